```python
import math
import jax, jax.numpy as jnp
from jax import lax
import numpy as np

D_MODEL = 2048
BATCH = 2
SEQ = 4096
DEPTH = 1

SSM_D_INNER = 1024
SSM_HEAD_DIM = 64
SSM_HEADS = SSM_D_INNER // SSM_HEAD_DIM
SSM_GROUPS = 4
SSM_STATE = 128
SSM_CONV = 4
SSM_CHUNK = 128
SSM_CONV_DIM = SSM_D_INNER + 2 * SSM_GROUPS * SSM_STATE

NSA_HEADS = 16
NSA_KV_HEADS = 4
NSA_HEAD_DIM = 64
NSA_Q_PER_KV = NSA_HEADS // NSA_KV_HEADS
NSA_D = NSA_HEADS * NSA_HEAD_DIM
KV_D = NSA_KV_HEADS * NSA_HEAD_DIM
CMP_BLOCK = 32
CMP_STRIDE = 16
CMP_HIDDEN = 512
SEL_BLOCK = 64
N_SELECT = 16
WINDOW = 512
Q_BLOCK = 128
ROPE_THETA = 500000.0
ROPE_DIM = NSA_HEAD_DIM // 4
SEL_FORCE_SCORE = 1.0e4

MIX_D = SSM_D_INNER + NSA_D
IN_PROJ_D = SSM_D_INNER + SSM_CONV_DIM + SSM_HEADS + NSA_D + 6 * KV_D + 3 * NSA_HEADS

N_EXPERTS = 32
TOP_K = 4
D_FF = 2048
SWIGLU_LIMIT = 7.0
SWIGLU_ALPHA = 1.702
MOE_BLOCK = 128
NORM_EPS = 1e-5

kernel_name = "hymba_ssd_nsa_moe_block"


def rmsnorm(x, w):
    xf = x.astype(jnp.float32)
    y = xf * lax.rsqrt(jnp.mean(xf * xf, axis=-1, keepdims=True) + NORM_EPS)
    return (y * w.astype(jnp.float32)).astype(x.dtype)


def masked_softmax(s, mask):
    s = jnp.where(mask, s.astype(jnp.float32), -jnp.inf)
    m = jnp.max(s, axis=-1, keepdims=True)
    m = jnp.where(jnp.isfinite(m), m, 0.0)
    e = jnp.where(mask, jnp.exp(s - m), 0.0)
    return e / jnp.maximum(jnp.sum(e, axis=-1, keepdims=True), 1e-30)


def rope_tables(pos):
    inv = ROPE_THETA ** (-jnp.arange(0, ROPE_DIM, 2, dtype=jnp.float32) / ROPE_DIM)
    ang = pos.astype(jnp.float32)[:, None] * inv[None, :]
    return jnp.cos(ang), jnp.sin(ang)


def partial_rope(x, cos, sin):
    half = ROPE_DIM // 2
    c = cos[:, None, :].astype(x.dtype)
    s = sin[:, None, :].astype(x.dtype)
    x1, x2, rest = x[..., :half], x[..., half:ROPE_DIM], x[..., ROPE_DIM:]
    return jnp.concatenate([x1 * c - x2 * s, x1 * s + x2 * c, rest], axis=-1)


def causal_depthwise_conv(u, w, b):
    out = lax.conv_general_dilated(
        u, w[:, None, :].astype(u.dtype), window_strides=(1,),
        padding=[(SSM_CONV - 1, 0)], dimension_numbers=("NWC", "WIO", "NWC"),
        feature_group_count=u.shape[-1])
    return out + b.astype(u.dtype)


def ssd_chunked_scan(x, dt, a, bm, cm):
    Bsz, S, H, P = x.shape
    N = bm.shape[-1]
    L = SSM_CHUNK
    nc = S // L
    X = (x.astype(jnp.float32) * dt[..., None]).reshape(Bsz, nc, L, H, P)
    Bc = bm.astype(jnp.float32).reshape(Bsz, nc, L, H, N)
    Cc = cm.astype(jnp.float32).reshape(Bsz, nc, L, H, N)
    a_cs = jnp.cumsum((dt * a).reshape(Bsz, nc, L, H), axis=2).transpose(0, 3, 1, 2)
    causal = jnp.tril(jnp.ones((L, L), dtype=bool))
    seg = a_cs[..., :, None] - a_cs[..., None, :]
    decay = jnp.exp(jnp.where(causal, seg, -jnp.inf))
    scores = jnp.einsum("bclhn,bcshn->bhcls", Cc, Bc) * decay
    y_diag = jnp.einsum("bhcls,bcshp->bclhp", scores, X)
    decay_to_end = jnp.exp(a_cs[..., -1:] - a_cs)
    states = jnp.einsum("bclhn,bhcl,bclhp->bchpn", Bc, decay_to_end, X)
    chunk_decay = jnp.exp(a_cs[..., -1])

    def step(h, inp):
        s_c, d_c = inp
        return d_c[..., None, None] * h + s_c, h

    _, prev = lax.scan(step, jnp.zeros((Bsz, H, P, N), jnp.float32),
                       (states.transpose(1, 0, 2, 3, 4), chunk_decay.transpose(2, 0, 1)))
    y_off = jnp.einsum("bclhn,cbhpn,bhcl->bclhp", Cc, prev, jnp.exp(a_cs))
    return (y_diag + y_off).reshape(Bsz, S, H, P).astype(x.dtype)


def gated_group_rmsnorm(y, z, w):
    g = (y * jax.nn.silu(z)).astype(jnp.float32)
    gs = g.reshape(*g.shape[:-1], SSM_GROUPS, -1)
    gs = gs * lax.rsqrt(jnp.mean(gs * gs, axis=-1, keepdims=True) + NORM_EPS)
    return (gs.reshape(g.shape) * w.astype(jnp.float32)).astype(y.dtype)


def ssd_mixer(z, xbc, dt_raw, conv_w, conv_b, dt_bias, a_log, d_skip, norm_w):
    Bsz, S, _ = z.shape
    xbc = jax.nn.silu(causal_depthwise_conv(xbc, conv_w, conv_b))
    xs, bm, cm = jnp.split(xbc, [SSM_D_INNER, SSM_D_INNER + SSM_GROUPS * SSM_STATE], axis=-1)
    rep = SSM_HEADS // SSM_GROUPS
    xs = xs.reshape(Bsz, S, SSM_HEADS, SSM_HEAD_DIM)
    bm = jnp.repeat(bm.reshape(Bsz, S, SSM_GROUPS, SSM_STATE), rep, axis=2)
    cm = jnp.repeat(cm.reshape(Bsz, S, SSM_GROUPS, SSM_STATE), rep, axis=2)
    dt = jax.nn.softplus((dt_raw + dt_bias).astype(jnp.float32))
    a = -jnp.exp(a_log.astype(jnp.float32))
    y = ssd_chunked_scan(xs, dt, a, bm, cm) + d_skip[:, None].astype(xs.dtype) * xs
    return gated_group_rmsnorm(y.reshape(Bsz, S, SSM_D_INNER), z, norm_w)


def compress_blocks(kv, pos_emb, w1, b1, w2, b2):
    Bsz, S, G, dh = kv.shape
    n_c = (S - CMP_BLOCK) // CMP_STRIDE + 1
    idx = jnp.arange(n_c)[:, None] * CMP_STRIDE + jnp.arange(CMP_BLOCK)[None, :]
    blocks = kv[:, idx] + pos_emb[:, None, :].astype(kv.dtype)
    blocks = blocks.transpose(0, 1, 3, 2, 4).reshape(Bsz, n_c, G, CMP_BLOCK * dh)
    h = jax.nn.gelu(blocks @ w1 + b1)
    return h @ w2 + b2


def nsa_mixer(q, k_cmp, v_cmp, k_slc, v_slc, k_win, v_win, gate_logits,
              cmp_pos_emb, cmp_w1, cmp_b1, cmp_w2, cmp_b2):
    Bsz, S, _ = q.shape
    G, R, dh = NSA_KV_HEADS, NSA_Q_PER_KV, NSA_HEAD_DIM
    scale = dh ** -0.5
    pos = jnp.arange(S, dtype=jnp.int32)
    cos, sin = rope_tables(pos)
    q = partial_rope(q.reshape(Bsz, S, NSA_HEADS, dh), cos, sin)
    kv_shape = (Bsz, S, G, dh)
    k_slc = partial_rope(k_slc.reshape(kv_shape), cos, sin)
    k_win = partial_rope(k_win.reshape(kv_shape), cos, sin)
    v_slc = v_slc.reshape(kv_shape)
    v_win = v_win.reshape(kv_shape)
    qg = q.reshape(Bsz, S, G, R, dh)

    kc = compress_blocks(k_cmp.reshape(kv_shape), cmp_pos_emb[0], cmp_w1[0], cmp_b1[0], cmp_w2[0], cmp_b2[0])
    vc = compress_blocks(v_cmp.reshape(kv_shape), cmp_pos_emb[1], cmp_w1[1], cmp_b1[1], cmp_w2[1], cmp_b2[1])
    n_c = kc.shape[1]
    cmp_end = jnp.arange(n_c, dtype=jnp.int32) * CMP_STRIDE + CMP_BLOCK - 1
    cc, sc = rope_tables(cmp_end)
    kc = partial_rope(kc, cc, sc)
    s_cmp = jnp.einsum("bsgrd,bcgd->bsgrc", qg, kc) * scale
    mask_cmp = cmp_end[None, :] <= pos[:, None]
    p_cmp = masked_softmax(s_cmp, mask_cmp[:, None, None, :])
    o_cmp = jnp.einsum("bsgrc,bcgd->bsgrd", p_cmp.astype(vc.dtype), vc)

    n_sel = S // SEL_BLOCK
    cmp_tok = jnp.arange(n_c)[:, None] * CMP_STRIDE + jnp.arange(CMP_BLOCK)[None, :]
    overlap = jax.nn.one_hot(cmp_tok // SEL_BLOCK, n_sel, dtype=jnp.float32).sum(axis=1)
    imp = jnp.einsum("bsgrc,cj->bsgj", p_cmp, overlap)
    cur = pos // SEL_BLOCK
    j = jnp.arange(n_sel)
    forced = (j[None, :] == 0) | (j[None, :] == cur[:, None]) | (j[None, :] == cur[:, None] - 1)
    valid = j[None, :] <= cur[:, None]
    score = jnp.where(forced[:, None, :], SEL_FORCE_SCORE,
                      jnp.where(valid[:, None, :], imp, -SEL_FORCE_SCORE))
    k_sel = min(N_SELECT, n_sel)
    _, sel_idx = lax.top_k(score, k_sel)

    nqb = S // Q_BLOCK
    span = WINDOW + Q_BLOCK
    pad = ((0, 0), (WINDOW, 0), (0, 0), (0, 0))
    k_win_pad = jnp.pad(k_win, pad)
    v_win_pad = jnp.pad(v_win, pad)
    b_ix = jnp.arange(Bsz)[:, None, None, None]
    g_ix = jnp.arange(G)[None, None, :, None]
    sel_off = jnp.arange(SEL_BLOCK, dtype=jnp.int32)

    def block_fn(args):
        qb, q_blk, idx_blk = args
        t = qb * Q_BLOCK + jnp.arange(Q_BLOCK, dtype=jnp.int32)
        tok = (idx_blk[..., None] * SEL_BLOCK + sel_off).reshape(Bsz, Q_BLOCK, G, k_sel * SEL_BLOCK)
        ks = k_slc[b_ix, tok, g_ix]
        vs = v_slc[b_ix, tok, g_ix]
        s = jnp.einsum("bqgrd,bqgnd->bqgrn", q_blk, ks) * scale
        m = (tok <= t[None, :, None, None])[:, :, :, None, :]
        p = masked_softmax(s, m).astype(vs.dtype)
        o_s = jnp.einsum("bqgrn,bqgnd->bqgrd", p, vs)
        kw = lax.dynamic_slice_in_dim(k_win_pad, qb * Q_BLOCK, span, axis=1)
        vw = lax.dynamic_slice_in_dim(v_win_pad, qb * Q_BLOCK, span, axis=1)
        kpos = qb * Q_BLOCK - WINDOW + jnp.arange(span, dtype=jnp.int32)
        dist = t[:, None] - kpos[None, :]
        mw = (dist >= 0) & (dist < WINDOW) & (kpos[None, :] >= 0)
        s = jnp.einsum("bqgrd,bkgd->bqgrk", q_blk, kw) * scale
        p = masked_softmax(s, mw[:, None, None, :]).astype(vw.dtype)
        o_w = jnp.einsum("bqgrk,bkgd->bqgrd", p, vw)
        return o_s, o_w

    q_blocks = qg.reshape(Bsz, nqb, Q_BLOCK, G, R, dh).transpose(1, 0, 2, 3, 4, 5)
    idx_blocks = sel_idx.reshape(Bsz, nqb, Q_BLOCK, G, k_sel).transpose(1, 0, 2, 3, 4)
    o_s, o_w = lax.map(block_fn, (jnp.arange(nqb, dtype=jnp.int32), q_blocks, idx_blocks))
    o_s = o_s.transpose(1, 0, 2, 3, 4, 5).reshape(Bsz, S, G, R, dh)
    o_w = o_w.transpose(1, 0, 2, 3, 4, 5).reshape(Bsz, S, G, R, dh)

    g = jax.nn.sigmoid(gate_logits.reshape(Bsz, S, G, R, 3))
    o = g[..., 0:1] * o_cmp + g[..., 1:2] * o_s + g[..., 2:3] * o_w
    return o.reshape(Bsz, S, NSA_D)


def moe_ffn(h, w_router, b_router, w_gate_up, b_gate_up, w_down, b_down):
    Bsz, S, D = h.shape
    T = Bsz * S
    xt = h.reshape(T, D)
    logits = xt @ w_router + b_router
    top_v, top_i = lax.top_k(logits, TOP_K)
    gates = jax.nn.softmax(top_v.astype(jnp.float32), axis=-1)
    n_rows = T * TOP_K
    e_flat = top_i.reshape(-1)
    tok_flat = jnp.arange(n_rows, dtype=jnp.int32) // TOP_K
    g_flat = gates.reshape(-1)
    order = jnp.argsort(e_flat)
    e_sorted = e_flat[order]
    counts = jnp.bincount(e_flat, length=N_EXPERTS)
    starts = jnp.cumsum(counts) - counts
    padded = ((counts + MOE_BLOCK - 1) // MOE_BLOCK) * MOE_BLOCK
    pad_end = jnp.cumsum(padded)
    pad_start = pad_end - padded
    dest = pad_start[e_sorted] + (jnp.arange(n_rows, dtype=jnp.int32) - starts[e_sorted])
    n_pad = n_rows + N_EXPERTS * MOE_BLOCK
    n_blocks = n_pad // MOE_BLOCK
    row_tok = jnp.full((n_pad,), T, dtype=jnp.int32).at[dest].set(tok_flat[order])
    row_gate = jnp.zeros((n_pad,), jnp.float32).at[dest].set(g_flat[order])
    blk_expert = jnp.minimum(
        jnp.searchsorted(pad_end, jnp.arange(n_blocks, dtype=jnp.int32) * MOE_BLOCK, side="right"),
        N_EXPERTS - 1)
    x_pad = jnp.concatenate([xt, jnp.zeros((1, D), xt.dtype)], axis=0)
    xb = x_pad[row_tok].reshape(n_blocks, MOE_BLOCK, D)

    def expert_block(args):
        xblk, e = args
        gu = xblk @ w_gate_up[e] + b_gate_up[e]
        gate = jnp.minimum(gu[:, :D_FF], SWIGLU_LIMIT)
        up = jnp.clip(gu[:, D_FF:], -SWIGLU_LIMIT, SWIGLU_LIMIT)
        glu = gate * jax.nn.sigmoid(SWIGLU_ALPHA * gate)
        return ((up + 1.0) * glu) @ w_down[e] + b_down[e]

    yb = lax.map(expert_block, (xb, blk_expert))
    y_rows = yb.reshape(n_pad, D) * row_gate[:, None].astype(yb.dtype)
    out = jax.ops.segment_sum(y_rows, row_tok, num_segments=T + 1)[:T]
    return out.reshape(Bsz, S, D).astype(h.dtype)


def hybrid_layer(x, attn_norm_w, w_in, conv_w, conv_b, dt_bias, a_log, d_skip, ssm_norm_w,
                 cmp_pos_emb, cmp_w1, cmp_b1, cmp_w2, cmp_b2, w_out, moe_norm_w, w_router,
                 b_router, w_gate_up, b_gate_up, w_down, b_down):
    h = rmsnorm(x, attn_norm_w)
    proj = h @ w_in
    widths = [SSM_D_INNER, SSM_CONV_DIM, SSM_HEADS, NSA_D] + [KV_D] * 6 + [3 * NSA_HEADS]
    cuts = [int(c) for c in np.cumsum(widths)[:-1]]
    (z, xbc, dt_raw, q, k_cmp, v_cmp, k_slc, v_slc, k_win, v_win, gate_logits) = jnp.split(proj, cuts, axis=-1)
    y_ssd = ssd_mixer(z, xbc, dt_raw, conv_w, conv_b, dt_bias, a_log, d_skip, ssm_norm_w)
    y_nsa = nsa_mixer(q, k_cmp, v_cmp, k_slc, v_slc, k_win, v_win, gate_logits,
                      cmp_pos_emb, cmp_w1, cmp_b1, cmp_w2, cmp_b2)
    x = x + jnp.concatenate([y_ssd, y_nsa], axis=-1) @ w_out
    x = x + moe_ffn(rmsnorm(x, moe_norm_w), w_router, b_router, w_gate_up, b_gate_up, w_down, b_down)
    return x


def setup_inputs(seed: int = 0) -> dict:
    key = jax.random.key(seed)
    ks = jax.random.split(key, 32)
    L = DEPTH
    f32 = jnp.float32

    def nrm(k, shape, scale):
        return jax.random.normal(k, shape, f32) * scale

    dt0 = jnp.exp(jax.random.uniform(ks[6], (L, SSM_HEADS), f32) * (math.log(0.1) - math.log(0.001)) + math.log(0.001))
    return {
        "x": nrm(ks[0], (BATCH, SEQ, D_MODEL), 1.0),
        "attn_norm_w": 1.0 + nrm(ks[1], (L, D_MODEL), 0.02),
        "w_in": nrm(ks[2], (L, D_MODEL, IN_PROJ_D), D_MODEL ** -0.5),
        "conv_w": nrm(ks[3], (L, SSM_CONV, SSM_CONV_DIM), SSM_CONV ** -0.5),
        "conv_b": nrm(ks[4], (L, SSM_CONV_DIM), 0.02),
        "dt_bias": dt0 + jnp.log(-jnp.expm1(-dt0)),
        "a_log": jnp.log(jax.random.uniform(ks[7], (L, SSM_HEADS), f32, 1.0, 16.0)),
        "d_skip": 1.0 + nrm(ks[8], (L, SSM_HEADS), 0.1),
        "ssm_norm_w": 1.0 + nrm(ks[9], (L, SSM_D_INNER), 0.02),
        "cmp_pos_emb": nrm(ks[10], (L, 2, CMP_BLOCK, NSA_HEAD_DIM), 0.1),
        "cmp_w1": nrm(ks[11], (L, 2, CMP_BLOCK * NSA_HEAD_DIM, CMP_HIDDEN), (CMP_BLOCK * NSA_HEAD_DIM) ** -0.5),
        "cmp_b1": nrm(ks[12], (L, 2, CMP_HIDDEN), 0.02),
        "cmp_w2": nrm(ks[13], (L, 2, CMP_HIDDEN, NSA_HEAD_DIM), CMP_HIDDEN ** -0.5),
        "cmp_b2": nrm(ks[14], (L, 2, NSA_HEAD_DIM), 0.02),
        "w_out": nrm(ks[15], (L, MIX_D, D_MODEL), MIX_D ** -0.5),
        "moe_norm_w": 1.0 + nrm(ks[16], (L, D_MODEL), 0.02),
        "w_router": nrm(ks[17], (L, D_MODEL, N_EXPERTS), D_MODEL ** -0.5),
        "b_router": nrm(ks[18], (L, N_EXPERTS), 0.01),
        "w_gate_up": nrm(ks[19], (L, N_EXPERTS, D_MODEL, 2 * D_FF), D_MODEL ** -0.5),
        "b_gate_up": nrm(ks[20], (L, N_EXPERTS, 2 * D_FF), 0.01),
        "w_down": nrm(ks[21], (L, N_EXPERTS, D_FF, D_MODEL), D_FF ** -0.5),
        "b_down": nrm(ks[22], (L, N_EXPERTS, D_MODEL), 0.01),
        "final_norm_w": 1.0 + nrm(ks[23], (D_MODEL,), 0.02),
    }


def reference(x, attn_norm_w, w_in, conv_w, conv_b, dt_bias, a_log, d_skip, ssm_norm_w,
              cmp_pos_emb, cmp_w1, cmp_b1, cmp_w2, cmp_b2, w_out, moe_norm_w, w_router,
              b_router, w_gate_up, b_gate_up, w_down, b_down, final_norm_w):
    for i in range(DEPTH):
        x = hybrid_layer(x, attn_norm_w[i], w_in[i], conv_w[i], conv_b[i], dt_bias[i], a_log[i],
                         d_skip[i], ssm_norm_w[i], cmp_pos_emb[i], cmp_w1[i], cmp_b1[i], cmp_w2[i],
                         cmp_b2[i], w_out[i], moe_norm_w[i], w_router[i], b_router[i],
                         w_gate_up[i], b_gate_up[i], w_down[i], b_down[i])
    return rmsnorm(x, final_norm_w)
```

```python
import functools
import math

import jax
import jax.numpy as jnp
import numpy as np
from jax import lax
from jax.experimental import pallas as pl
from jax.experimental.pallas import tpu as pltpu

F32 = jnp.float32
BF16 = jnp.bfloat16
I32 = jnp.int32
U32 = jnp.uint32

NORM_EPS = 1e-5
SSM_D_INNER = 1024
SSM_HEAD_DIM = 64
SSM_HEADS = 16
SSM_GROUPS = 4
SSM_STATE = 128
SSM_CONV = 4
SSM_CHUNK = 128
SSM_CONV_DIM = 2048
NSA_HEADS = 16
NSA_KV_HEADS = 4
NSA_HEAD_DIM = 64
NSA_D = 1024
KV_D = 256
CMP_BLOCK = 32
CMP_STRIDE = 16
CMP_HIDDEN = 512
SEL_BLOCK = 64
N_SELECT = 16
WINDOW = 512
ROPE_THETA = 500000.0
ROPE_DIM = 16
SEL_FORCE_SCORE = 1.0e4
N_EXPERTS = 32
TOP_K = 4
D_FF = 2048
SWIGLU_LIMIT = 7.0
SWIGLU_ALPHA = 1.702

LANES = 128
SUBLANES = 8
VMEM_LIMIT = 56 * 1024 * 1024

NEG = -1.0e30

COL_Q = 0
COL_Z = 1024
COL_XBC = 2048
COL_KV = 4096
COL_DT = 5632
COL_GATE = 5760
PROJ_W = 6144

MOE_TM = 256
Q_TILE = 128
SLC_TILE = 512


def _cparams(sem, vmem=VMEM_LIMIT):
    return pltpu.CompilerParams(dimension_semantics=sem, vmem_limit_bytes=vmem)


def _dot(a, b):
    return jnp.dot(a, b, preferred_element_type=F32)


def _dot_nt(a, b):
    return lax.dot_general(a, b, (((1,), (1,)), ((), ())), preferred_element_type=F32)


def _split3(b):
    b1 = b.astype(BF16)
    r1 = b - b1.astype(F32)
    b2 = r1.astype(BF16)
    r2 = r1 - b2.astype(F32)
    return b1, b2, r2.astype(BF16)


def _dot_exact_lhs(a_bf16, b):
    b1, b2, b3 = _split3(b)
    return _dot(a_bf16, b1) + _dot(a_bf16, b2) + _dot(a_bf16, b3)


def _dot_exact_rhs(a, b_bf16):
    a1, a2, a3 = _split3(a)
    return _dot(a1, b_bf16) + _dot(a2, b_bf16) + _dot(a3, b_bf16)


def _sigmoid(x):
    return 1.0 / (1.0 + jnp.exp(-x))


def _silu(x):
    return x * _sigmoid(x)


def _softplus(x):
    return jnp.maximum(x, 0.0) + jnp.log1p(jnp.exp(-jnp.abs(x)))


def _gelu_tanh(x):
    c = math.sqrt(2.0 / math.pi)
    return 0.5 * x * (1.0 + jnp.tanh(c * (x + 0.044715 * (x * x * x))))


def _rms_rows(x, w):
    ms = jnp.mean(x * x, axis=-1, keepdims=True)
    return x * lax.rsqrt(ms + NORM_EPS) * w


def _in_proj_kernel(x_ref, nw_ref, w_ref, o_ref, hn_ref, *, tm):
    @pl.when(pl.program_id(1) == 0)
    def _():
        def body(c, carry):
            r = pl.multiple_of(c * 128, 128)
            hn_ref[pl.ds(r, 128), :] = _rms_rows(x_ref[pl.ds(r, 128), :], nw_ref[...]).astype(BF16)
            return carry
        lax.fori_loop(0, tm // 128, body, 0)

    o_ref[...] = _dot(hn_ref[...], w_ref[...])


def _in_proj(x2d, nw, wp):
    T, D = x2d.shape
    NP = wp.shape[1]
    tm = min(1024, T)
    tn = 1024
    return pl.pallas_call(
        functools.partial(_in_proj_kernel, tm=tm),
        out_shape=jax.ShapeDtypeStruct((T, NP), F32),
        grid=(T // tm, NP // tn),
        in_specs=[
            pl.BlockSpec((tm, D), lambda i, j: (i, 0)),
            pl.BlockSpec((1, D), lambda i, j: (0, 0)),
            pl.BlockSpec((D, tn), lambda i, j: (0, j)),
        ],
        out_specs=pl.BlockSpec((tm, tn), lambda i, j: (i, j)),
        scratch_shapes=[pltpu.VMEM((tm, D), BF16)],
        compiler_params=_cparams(("parallel", "arbitrary")),
        name="in_proj",
    )(x2d, nw, wp)


def _rope128(x, cos, sina, sinb):
    return x * cos + pltpu.roll(x, 8, 1) * sina + pltpu.roll(x, 120, 1) * sinb


def _nsa_prep_kernel(q_ref, ks_ref, kw_ref, gl_ref, cos_ref, sina_ref, sinb_ref,
                     qo_ref, kso_ref, vso_ref, kwo_ref, vwo_ref, sg_ref):
    cos = cos_ref[...]
    sina = sina_ref[...]
    sinb = sinb_ref[...]
    scale = NSA_HEAD_DIM ** -0.5
    for c in range(NSA_D // LANES):
        xq = q_ref[:, c * LANES:(c + 1) * LANES]
        qo_ref[:, c * LANES:(c + 1) * LANES] = (_rope128(xq, cos, sina, sinb) * scale).astype(BF16)
    for src, ko, vo in ((ks_ref, kso_ref, vso_ref), (kw_ref, kwo_ref, vwo_ref)):
        for c in range(KV_D // LANES):
            kr = _rope128(src[:, c * LANES:(c + 1) * LANES], cos, sina, sinb)
            vv = src[:, KV_D + c * LANES:KV_D + (c + 1) * LANES]
            for half in range(2):
                g = 2 * c + half
                ko[g] = kr[:, half * 64:(half + 1) * 64].astype(BF16)
                vo[g] = vv[:, half * 64:(half + 1) * 64].astype(BF16)
    sg_ref[...] = _sigmoid(gl_ref[...])


def _nsa_prep(proj, B, S, cos, sina, sinb):
    ts = min(512, S)
    nst = S // ts
    G = NSA_KV_HEADS
    kv_spec = pl.BlockSpec((None, G, ts, 64), lambda b, s: (b, 0, s, 0))
    kv_shape = jax.ShapeDtypeStruct((B, G, S, 64), BF16)
    return pl.pallas_call(
        _nsa_prep_kernel,
        out_shape=(
            jax.ShapeDtypeStruct((B, S, NSA_D), BF16),
            kv_shape, kv_shape, kv_shape, kv_shape,
            jax.ShapeDtypeStruct((B, S, LANES), F32),
        ),
        grid=(B, nst),
        in_specs=[
            pl.BlockSpec((ts, NSA_D), lambda b, s: (b * nst + s, COL_Q // NSA_D)),
            pl.BlockSpec((ts, 512), lambda b, s: (b * nst + s, (COL_KV + 512) // 512)),
            pl.BlockSpec((ts, 512), lambda b, s: (b * nst + s, (COL_KV + 1024) // 512)),
            pl.BlockSpec((ts, LANES), lambda b, s: (b * nst + s, COL_GATE // LANES)),
            pl.BlockSpec((ts, LANES), lambda b, s: (s, 0)),
            pl.BlockSpec((ts, LANES), lambda b, s: (s, 0)),
            pl.BlockSpec((ts, LANES), lambda b, s: (s, 0)),
        ],
        out_specs=(
            pl.BlockSpec((None, ts, NSA_D), lambda b, s: (b, s, 0)),
            kv_spec, kv_spec, kv_spec, kv_spec,
            pl.BlockSpec((None, ts, LANES), lambda b, s: (b, s, 0)),
        ),
        compiler_params=_cparams(("parallel", "parallel")),
        name="nsa_prep",
    )(proj, proj, proj, proj, cos, sina, sinb)


def _ssd_kernel(z_ref, xbc_ref, dt_ref, cw_ref, cb_ref, dtb_ref, alog_ref, dsk_ref, nw_ref,
                y_ref, buf, st):
    L = SSM_CHUNK
    c = pl.program_id(1)

    @pl.when(c == 0)
    def _():
        buf[0:8, :] = jnp.zeros((8, SSM_CONV_DIM), F32)
        st[...] = jnp.zeros(st.shape, F32)

    buf[8:8 + L, :] = xbc_ref[...]
    acc = jnp.broadcast_to(cb_ref[...], (L, SSM_CONV_DIM))
    for k in range(SSM_CONV):
        acc = acc + cw_ref[k:k + 1, :] * buf[5 + k:5 + k + L, :]
    xc = _silu(acc)
    buf[0:8, :] = xbc_ref[L - 8:L, :]

    lane = lax.broadcasted_iota(I32, (L, LANES), 1)
    row = lax.broadcasted_iota(I32, (L, LANES), 0)
    lo = lane < 64
    dtv = jnp.where(lane < SSM_HEADS, _softplus(dt_ref[...] + dtb_ref[...]), 0.0)
    a = -jnp.exp(alog_ref[...])
    tri = jnp.where(row >= lane, 1.0, 0.0).astype(BF16)
    acs = _dot_exact_lhs(tri, dtv * a)
    acs_t = acs.T
    causal = row >= lane

    ys = []
    for g in range(SSM_GROUPS):
        bg = xc[:, SSM_D_INNER + g * SSM_STATE:SSM_D_INNER + (g + 1) * SSM_STATE]
        cg = xc[:, SSM_D_INNER + (SSM_GROUPS + g) * SSM_STATE:SSM_D_INNER + (SSM_GROUPS + g + 1) * SSM_STATE]
        bgt = bg.T.astype(BF16)
        cgb = cg.astype(BF16)
        gmat = _dot(cgb, bgt)
        for p in (2 * g, 2 * g + 1):
            h0, h1 = 2 * p, 2 * p + 1
            xs_pair = xc[:, p * LANES:(p + 1) * LANES]
            col0 = acs[:, h0:h0 + 1]
            col1 = acs[:, h1:h1 + 1]
            colp = jnp.where(lo, col0, col1)
            last = jnp.where(lo[0:1, :], acs[L - 1:L, h0:h0 + 1], acs[L - 1:L, h1:h1 + 1])
            x = xs_pair * jnp.where(lo, dtv[:, h0:h0 + 1], dtv[:, h1:h1 + 1])
            m0 = (gmat * jnp.where(causal, jnp.exp(col0 - acs_t[h0:h0 + 1, :]), 0.0)).astype(BF16)
            m1 = (gmat * jnp.where(causal, jnp.exp(col1 - acs_t[h1:h1 + 1, :]), 0.0)).astype(BF16)
            y_diag = _dot(m0, jnp.where(lo, x, 0.0).astype(BF16)) + _dot(m1, jnp.where(lo, 0.0, x).astype(BF16))
            s_prev = st[p]
            y_off = _dot(cgb, s_prev.astype(BF16)) * jnp.exp(colp)
            w = (x * jnp.exp(last - colp)).astype(BF16)
            st[p] = jnp.exp(last) * s_prev + _dot(bgt, w)
            ys.append(y_diag + y_off + dsk_ref[:, p * LANES:(p + 1) * LANES] * xs_pair)
    y = jnp.concatenate(ys, axis=1)
    gte = y * _silu(z_ref[...])
    gw = SSM_D_INNER // SSM_GROUPS
    outs = []
    for k in range(SSM_GROUPS):
        gk = gte[:, k * gw:(k + 1) * gw]
        ms = jnp.mean(gk * gk, axis=-1, keepdims=True)
        outs.append(gk * lax.rsqrt(ms + NORM_EPS))
    y_ref[...] = (jnp.concatenate(outs, axis=1) * nw_ref[...]).astype(BF16)


def _ssd(proj, B, S, conv_w, conv_b, dtb, alog, dskip, nw):
    L = SSM_CHUNK
    nc = S // L
    small = lambda shp: pl.BlockSpec(shp, lambda b, c: (0, 0))
    return pl.pallas_call(
        _ssd_kernel,
        out_shape=jax.ShapeDtypeStruct((B * S, SSM_D_INNER), BF16),
        grid=(B, nc),
        in_specs=[
            pl.BlockSpec((L, SSM_D_INNER), lambda b, c: (b * nc + c, COL_Z // SSM_D_INNER)),
            pl.BlockSpec((L, SSM_CONV_DIM), lambda b, c: (b * nc + c, COL_XBC // SSM_CONV_DIM)),
            pl.BlockSpec((L, LANES), lambda b, c: (b * nc + c, COL_DT // LANES)),
            small((SSM_CONV, SSM_CONV_DIM)),
            small((1, SSM_CONV_DIM)),
            small((1, LANES)),
            small((1, LANES)),
            small((1, SSM_D_INNER)),
            small((1, SSM_D_INNER)),
        ],
        out_specs=pl.BlockSpec((L, SSM_D_INNER), lambda b, c: (b * nc + c, 0)),
        scratch_shapes=[
            pltpu.VMEM((L + 8, SSM_CONV_DIM), F32),
            pltpu.VMEM((SSM_HEADS // 2, SSM_STATE, LANES), F32),
        ],
        compiler_params=_cparams(("parallel", "arbitrary")),
        name="ssd",
    )(proj, proj, proj, conv_w, conv_b, dtb, alog, dskip, nw)


def _compress_kernel(xa_ref, xb_ref, pos_ref, w1_ref, b1_ref, w2_ref, b2_ref, cos_ref, sin_ref, rot_ref,
                     o_ref, w1b, feat, *, ncp):
    kv = pl.program_id(1)
    half = CMP_STRIDE
    hw = half * 64
    w1b[...] = w1_ref[...].astype(BF16)
    w2 = w2_ref[...].astype(BF16)
    G = NSA_KV_HEADS
    is_k = kv == 0
    for g in range(G):
        x_ref = (xa_ref, xb_ref)[g // 2]
        for l in range(half):
            xg = x_ref[pl.ds(l, ncp, stride=half), :][:, (g % 2) * 64:(g % 2 + 1) * 64]
            feat[:, l * 64:(l + 1) * 64] = xg + pos_ref[l:l + 1, :]
            feat[:, hw + l * 64:hw + (l + 1) * 64] = xg + pos_ref[half + l:half + l + 1, :]
        first = _dot(feat[:, 0:hw].astype(BF16), w1b[0:hw, :])
        second = _dot(feat[:, hw:2 * hw].astype(BF16), w1b[hw:2 * hw, :])
        h = first + pltpu.roll(second, ncp - 1, 0) + b1_ref[...]
        h = _gelu_tanh(h)
        o = _dot(h.astype(BF16), w2) + b2_ref[...]
        roped = o * cos_ref[...] + _dot_exact_rhs(o, rot_ref[...]) * sin_ref[...]
        o_ref[g] = jnp.where(is_k, roped, o)


def _compress(proj, B, S, pos, w1, b1, w2, b2, cosc, sinc, rot):
    ncp = S // CMP_STRIDE
    G = NSA_KV_HEADS
    kvblk = COL_KV // KV_D
    return pl.pallas_call(
        functools.partial(_compress_kernel, ncp=ncp),
        out_shape=jax.ShapeDtypeStruct((B, 2, G, ncp, 64), F32),
        grid=(B, 2),
        in_specs=[
            pl.BlockSpec((S, LANES), lambda b, k: (b, 2 * (kvblk + k))),
            pl.BlockSpec((S, LANES), lambda b, k: (b, 2 * (kvblk + k) + 1)),
            pl.BlockSpec((None, CMP_BLOCK, 64), lambda b, k: (k, 0, 0)),
            pl.BlockSpec((None, CMP_BLOCK * 64, CMP_HIDDEN), lambda b, k: (k, 0, 0)),
            pl.BlockSpec((None, 1, CMP_HIDDEN), lambda b, k: (k, 0, 0)),
            pl.BlockSpec((None, CMP_HIDDEN, 64), lambda b, k: (k, 0, 0)),
            pl.BlockSpec((None, 1, 64), lambda b, k: (k, 0, 0)),
            pl.BlockSpec((ncp, 64), lambda b, k: (0, 0)),
            pl.BlockSpec((ncp, 64), lambda b, k: (0, 0)),
            pl.BlockSpec((64, 64), lambda b, k: (0, 0)),
        ],
        out_specs=pl.BlockSpec((None, None, G, ncp, 64), lambda b, k: (b, k, 0, 0, 0)),
        scratch_shapes=[pltpu.VMEM((CMP_BLOCK * 64, CMP_HIDDEN), BF16),
                        pltpu.VMEM((ncp, CMP_BLOCK * 64), F32)],
        compiler_params=_cparams(("parallel", "parallel")),
        name="compress",
    )(proj, proj, pos, w1, b1, w2, b2, cosc, sinc, rot)


def _nsa_attn_kernel(q_ref, kc_ref, vc_ref, ks_ref, vs_ref, kw_ref, vw_ref, sg_ref, ovl_ref,
                     o_ref, *, ncp, n_c, n_sel, k_sel):
    g = pl.program_id(1)
    qi = pl.program_id(2)
    tq = Q_TILE
    R = NSA_HEADS // NSA_KV_HEADS
    t0 = qi * tq
    q = q_ref[...]
    q4 = jnp.concatenate([q[:, r * 64:(r + 1) * 64] for r in range(R)], axis=0)
    rep = lambda a: jnp.concatenate([a] * R, axis=0)
    tpos = t0 + lax.broadcasted_iota(I32, (tq, 1), 0)

    kc = kc_ref[...].astype(BF16)
    vc = vc_ref[...].astype(BF16)
    cidx = lax.broadcasted_iota(I32, (tq, ncp), 1)
    maskc = rep(jnp.where((cidx * CMP_STRIDE + (CMP_BLOCK - 1) <= tpos) & (cidx < n_c), 1.0, 0.0)) > 0.5
    s = _dot_nt(q4, kc)
    m = jnp.max(jnp.where(maskc, s, NEG), axis=-1, keepdims=True)
    e = jnp.where(maskc, jnp.exp(s - m), 0.0)
    p = e / jnp.maximum(jnp.sum(e, axis=-1, keepdims=True), 1e-30)
    o_cmp = _dot(p.astype(BF16), vc)
    psum = p[0:tq]
    for r in range(1, R):
        psum = psum + p[r * tq:(r + 1) * tq]

    p_hi = psum.astype(BF16)
    p_lo = (psum - p_hi.astype(F32)).astype(BF16)
    imp = _dot(p_hi, ovl_ref[...]) + _dot(p_lo, ovl_ref[...])
    jl = lax.broadcasted_iota(I32, (tq, LANES), 1)
    cur = lax.shift_right_logical(tpos, 6)
    forced = (jl == 0) | (jl == cur) | (jl == cur - 1)
    score = jnp.where(forced, SEL_FORCE_SCORE, jnp.where(jl <= cur, imp, -SEL_FORCE_SCORE))
    score = jnp.where(jl < n_sel, score, -3.0 * SEL_FORCE_SCORE)
    st = score.T
    nsp = max(n_sel, SUBLANES)
    sel_rows = []
    for v in range(nsp // SUBLANES):
        gv = st[v * SUBLANES:(v + 1) * SUBLANES, :]
        jrow = v * SUBLANES + lax.broadcasted_iota(I32, (SUBLANES, tq), 0)
        cnt = jnp.zeros((SUBLANES, tq), F32)
        for i in range(n_sel):
            ri = st[i:i + 1, :]
            if v * SUBLANES + SUBLANES - 1 < i:
                beats = ri > gv
            elif v * SUBLANES > i:
                beats = ri >= gv
            else:
                beats = ((ri >= gv) & (jrow > i)) | ((ri > gv) & (jrow < i))
            cnt = cnt + jnp.where(beats, 1.0, 0.0)
        sel_rows.append(jnp.where((cnt < float(k_sel)) & (jrow < n_sel), 1.0, 0.0))
    sel_rows.append(jnp.zeros((LANES - nsp, tq), F32))
    sel = jnp.concatenate(sel_rows, axis=0).T.astype(BF16)

    tk = min(SLC_TILE, ks_ref.shape[0])
    blk_r = lax.broadcasted_iota(I32, (LANES, tk), 0)
    blk_c = lax.shift_right_logical(lax.broadcasted_iota(I32, (LANES, tk), 1), 6)
    kcol = lax.broadcasted_iota(I32, (tq, tk), 1)

    def slc_body(j, carry):
        m_old, l_old, acc = carry
        k0 = pl.multiple_of(j * tk, tk)
        kt = ks_ref[pl.ds(k0, tk), :]
        vt = vs_ref[pl.ds(k0, tk), :]
        expand = jnp.where(blk_r == blk_c + j * (tk // SEL_BLOCK), 1.0, 0.0).astype(BF16)
        bias = jnp.where((_dot(sel, expand) > 0.5) & (kcol + k0 <= tpos), 0.0, NEG)
        s = _dot_nt(q4, kt) + rep(bias)
        m_new = jnp.maximum(m_old, jnp.max(s, axis=-1, keepdims=True))
        p = jnp.exp(s - m_new)
        alpha = jnp.exp(m_old - m_new)
        l_new = alpha * l_old + jnp.sum(p, axis=-1, keepdims=True)
        return m_new, l_new, alpha * acc + _dot(p.astype(BF16), vt)

    init = (jnp.full((R * tq, 1), NEG, F32), jnp.zeros((R * tq, 1), F32), jnp.zeros((R * tq, 64), F32))
    n_slc = lax.shift_right_logical(t0 + (tq + tk - 1), int(math.log2(tk)))
    _, l_slc, acc_slc = lax.fori_loop(0, n_slc, slc_body, init)
    o_slc = acc_slc / jnp.maximum(l_slc, 1e-30)

    span = min(WINDOW + tq, kw_ref.shape[0])
    w0 = pl.multiple_of(jnp.maximum(t0 + tq - span, 0), tq)
    kt = kw_ref[pl.ds(w0, span), :]
    vt = vw_ref[pl.ds(w0, span), :]
    dist = tpos - (w0 + lax.broadcasted_iota(I32, (tq, span), 1))
    wbias = jnp.where((dist >= 0) & (dist < WINDOW), 0.0, NEG)
    s = _dot_nt(q4, kt) + rep(wbias)
    p = jnp.exp(s - jnp.max(s, axis=-1, keepdims=True))
    o_win = _dot(p.astype(BF16), vt) / jnp.sum(p, axis=-1, keepdims=True)

    sg = sg_ref[...]
    outs = []
    for r in range(R):
        base = 3 * (g * R + r)
        gts = [jnp.sum(jnp.where(jl == base + k, sg, 0.0), axis=-1, keepdims=True) for k in range(3)]
        rows = slice(r * tq, (r + 1) * tq)
        outs.append(gts[0] * o_cmp[rows] + gts[1] * o_slc[rows] + gts[2] * o_win[rows])
    o_ref[...] = jnp.concatenate(outs, axis=1).astype(BF16)


def _nsa_attn(q_rot, kvc, ks, vs, kw, vw, sg, ovl, B, S):
    G = NSA_KV_HEADS
    nq = S // Q_TILE
    ncp = S // CMP_STRIDE
    n_c = (S - CMP_BLOCK) // CMP_STRIDE + 1
    n_sel = S // SEL_BLOCK
    k_sel = min(N_SELECT, n_sel)
    full = pl.BlockSpec((None, None, S, 64), lambda b, g, i: (b, g, 0, 0))
    return pl.pallas_call(
        functools.partial(_nsa_attn_kernel, ncp=ncp, n_c=n_c, n_sel=n_sel, k_sel=k_sel),
        out_shape=jax.ShapeDtypeStruct((B, S, NSA_D), BF16),
        grid=(B, G, nq),
        in_specs=[
            pl.BlockSpec((None, Q_TILE, KV_D), lambda b, g, i: (b, i, g)),
            pl.BlockSpec((None, None, None, ncp, 64), lambda b, g, i: (b, 0, g, 0, 0)),
            pl.BlockSpec((None, None, None, ncp, 64), lambda b, g, i: (b, 1, g, 0, 0)),
            full, full, full, full,
            pl.BlockSpec((None, Q_TILE, LANES), lambda b, g, i: (b, i, 0)),
            pl.BlockSpec((ncp, LANES), lambda b, g, i: (0, 0)),
        ],
        out_specs=pl.BlockSpec((None, Q_TILE, KV_D), lambda b, g, i: (b, i, g)),
        compiler_params=_cparams(("parallel", "parallel", "arbitrary")),
        name="nsa_attn",
    )(q_rot, kvc, kvc, ks, vs, kw, vw, sg, ovl)


def _out_proj_kernel(ys_ref, yn_ref, x_ref, w_ref, nw_ref, wr_ref, br_ref, x1_ref, h2_ref, lg_ref):
    half = ys_ref.shape[1]
    y = _dot(ys_ref[...], w_ref[0:half, :]) + _dot(yn_ref[...], w_ref[half:2 * half, :])
    x1 = x_ref[...] + y
    x1_ref[...] = x1
    hn = _rms_rows(x1, nw_ref[...])
    h2_ref[...] = hn
    h1, h2, _ = _split3(hn)
    w1, w2, _ = _split3(wr_ref[...])
    lg_ref[...] = _dot(h1, w1) + _dot(h1, w2) + _dot(h2, w1) + br_ref[...]


def _out_proj(y_ssd, y_nsa, x2d, w_out_b, nw, wr, br):
    T, D = x2d.shape
    tm = min(512, T)
    half = y_ssd.shape[1]
    return pl.pallas_call(
        _out_proj_kernel,
        out_shape=(
            jax.ShapeDtypeStruct((T, D), F32),
            jax.ShapeDtypeStruct((T, D), F32),
            jax.ShapeDtypeStruct((T, LANES), F32),
        ),
        grid=(T // tm,),
        in_specs=[
            pl.BlockSpec((tm, half), lambda i: (i, 0)),
            pl.BlockSpec((tm, half), lambda i: (i, 0)),
            pl.BlockSpec((tm, D), lambda i: (i, 0)),
            pl.BlockSpec((D, D), lambda i: (0, 0)),
            pl.BlockSpec((1, D), lambda i: (0, 0)),
            pl.BlockSpec((D, LANES), lambda i: (0, 0)),
            pl.BlockSpec((1, LANES), lambda i: (0, 0)),
        ],
        out_specs=(
            pl.BlockSpec((tm, D), lambda i: (i, 0)),
            pl.BlockSpec((tm, D), lambda i: (i, 0)),
            pl.BlockSpec((tm, LANES), lambda i: (i, 0)),
        ),
        compiler_params=_cparams(("parallel",)),
        name="out_proj",
    )(y_ssd, y_nsa, x2d, w_out_b, nw, wr, br)


def _route_kernel(lg_ref, idx_ref, rank_ref, gate_ref, cnt_ref, carry, *, tr):
    i = pl.program_id(0)

    @pl.when(i == 0)
    def _():
        carry[...] = jnp.zeros(carry.shape, F32)

    lane = lax.broadcasted_iota(I32, (tr, LANES), 1)
    lanef = lane.astype(F32)
    l = jnp.where(lane < N_EXPERTS, lg_ref[...], NEG)
    ohs, vals, idxs = [], [], []
    for _ in range(TOP_K):
        m = jnp.max(l, axis=-1, keepdims=True)
        idx = jnp.min(jnp.where(l == m, lanef, float(LANES)), axis=-1, keepdims=True)
        oh = lanef == idx
        l = jnp.where(oh, 2.0 * NEG, l)
        ohs.append(oh)
        vals.append(m)
        idxs.append(idx)
    es = [jnp.exp(v - vals[0]) for v in vals]
    den = es[0] + es[1] + es[2] + es[3]
    oh_sum = jnp.zeros((tr, LANES), F32)
    for oh in ohs:
        oh_sum = oh_sum + jnp.where(oh, 1.0, 0.0)
    r_i = lax.broadcasted_iota(I32, (tr, tr), 0)
    c_i = lax.broadcasted_iota(I32, (tr, tr), 1)
    strict = jnp.where(r_i > c_i, 1.0, 0.0).astype(BF16)
    base = carry[0:1, :] + _dot(strict, oh_sum.astype(BF16))
    idx_o = jnp.zeros((tr, LANES), F32)
    rank_o = jnp.zeros((tr, LANES), F32)
    gate_o = jnp.zeros((tr, LANES), F32)
    for k in range(TOP_K):
        rk = jnp.sum(jnp.where(ohs[k], base, 0.0), axis=-1, keepdims=True)
        idx_o = jnp.where(lane == k, idxs[k], idx_o)
        rank_o = jnp.where(lane == k, rk, rank_o)
        gate_o = jnp.where(lane == k, es[k] / den, gate_o)
    idx_ref[...] = idx_o.astype(I32)
    rank_ref[...] = rank_o.astype(I32)
    gate_ref[...] = gate_o
    carry[...] = carry[...] + jnp.sum(oh_sum, axis=0, keepdims=True)
    cnt_ref[...] = carry[...]


def _route(logits):
    T = logits.shape[0]
    tr = min(512, T)
    tok = pl.BlockSpec((tr, LANES), lambda i: (i, 0))
    return pl.pallas_call(
        functools.partial(_route_kernel, tr=tr),
        out_shape=(
            jax.ShapeDtypeStruct((T, LANES), I32),
            jax.ShapeDtypeStruct((T, LANES), I32),
            jax.ShapeDtypeStruct((T, LANES), F32),
            jax.ShapeDtypeStruct((SUBLANES, LANES), F32),
        ),
        grid=(T // tr,),
        in_specs=[tok],
        out_specs=(tok, tok, tok, pl.BlockSpec((SUBLANES, LANES), lambda i: (0, 0))),
        scratch_shapes=[pltpu.VMEM((SUBLANES, LANES), F32)],
        compiler_params=_cparams(("arbitrary",)),
        name="route",
    )(logits)


def _route_fin_kernel(cnt_ref, idx_ref, rank_ref, dest_ref, be_ref, *, tr, nbp, tm_shift):
    lane = lax.broadcasted_iota(I32, (SUBLANES, LANES), 1)
    cnt = cnt_ref[...].astype(I32)
    tm = 1 << tm_shift
    nblk = jnp.where(lane < N_EXPERTS, lax.shift_right_logical(cnt + (tm - 1), tm_shift), 0)
    r_i = lax.broadcasted_iota(I32, (LANES, LANES), 0)
    c_i = lax.broadcasted_iota(I32, (LANES, LANES), 1)
    upper = jnp.where(r_i <= c_i, 1.0, 0.0).astype(BF16)
    end_blk = _dot(nblk.astype(F32).astype(BF16), upper)
    start_row = (end_blk - nblk.astype(F32)) * float(tm)
    idx = idx_ref[...].astype(F32)
    lane_t = lax.broadcasted_iota(I32, (tr, LANES), 1)
    lane_f = lane_t.astype(F32)
    dest = jnp.zeros((tr, LANES), F32)
    for k in range(TOP_K):
        e_k = jnp.sum(jnp.where(lane_t == k, idx, 0.0), axis=-1, keepdims=True)
        s_k = jnp.sum(jnp.where(lane_f == e_k, start_row[0:1, :], 0.0), axis=-1, keepdims=True)
        dest = jnp.where(lane_t == k, s_k, dest)
    dest_ref[...] = dest.astype(I32) + jnp.where(lane_t < TOP_K, rank_ref[...], 0)
    blk = lax.broadcasted_iota(I32, (nbp, LANES), 0).astype(F32)
    lane_b = lax.broadcasted_iota(I32, (nbp, LANES), 1)
    lane_bf = lane_b.astype(F32)
    rsum = lambda v: jnp.sum(v, axis=-1, keepdims=True)
    end_row = end_blk[0:1, :]
    nblk_row = nblk.astype(F32)[0:1, :]
    is_exp = lane_b < N_EXPERTS
    nonempty = is_exp & (nblk_row > 0.0)
    be = jnp.minimum(rsum(jnp.where(is_exp & (end_row <= blk), 1.0, 0.0)), float(N_EXPERTS - 1))
    n_used = rsum(jnp.where(lane_b == N_EXPERTS - 1, end_row, 0.0))
    start_of = rsum(jnp.where(lane_bf == be, end_row - nblk_row, 0.0))
    first = jnp.where((start_of == blk[:, 0:1]) & (blk[:, 0:1] < n_used), 1.0, 0.0)
    run = rsum(jnp.where(nonempty & (lane_bf < be), 1.0, 0.0))
    none = float(LANES)
    nxt = jnp.min(jnp.where(nonempty & (lane_bf > be), lane_bf, none), axis=-1, keepdims=True)
    first_e = jnp.min(jnp.where(nonempty, lane_bf, none), axis=-1, keepdims=True)
    last = jnp.where(nxt == none, 1.0, 0.0)
    nxt = jnp.where(nxt == none, first_e, nxt)
    n_runs = rsum(jnp.where(nonempty, 1.0, 0.0))
    tab = jnp.zeros((nbp, LANES), F32)
    for c, v in enumerate((be, n_used, first, run, nxt, last, n_runs)):
        tab = jnp.where(lane_b == c, v, tab)
    be_ref[...] = tab.astype(I32)


def _route_fin(cnt, idx, rank, n_blocks):
    T = idx.shape[0]
    tr = min(512, T)
    nbp = ((n_blocks + SUBLANES - 1) // SUBLANES) * SUBLANES
    tok = pl.BlockSpec((tr, LANES), lambda i: (i, 0))
    return pl.pallas_call(
        functools.partial(_route_fin_kernel, tr=tr, nbp=nbp, tm_shift=int(math.log2(MOE_TM))),
        out_shape=(
            jax.ShapeDtypeStruct((T, LANES), I32),
            jax.ShapeDtypeStruct((nbp, LANES), I32),
        ),
        grid=(T // tr,),
        in_specs=[pl.BlockSpec((SUBLANES, LANES), lambda i: (0, 0)), tok, tok],
        out_specs=(tok, pl.BlockSpec((nbp, LANES), lambda i: (0, 0))),
        compiler_params=_cparams(("arbitrary",)),
        name="route_fin",
    )(cnt, idx, rank)


def _row_copy(src_ref, src_row, dst_ref, dst_row, sem):
    return pltpu.make_async_copy(src_ref.at[pl.ds(src_row, 1)], dst_ref.at[pl.ds(dst_row, 1)], sem)


def _dispatch_kernel(dest_ref, h2_ref, xs_in, xs_out, sem, *, tm):
    del xs_in
    base = pl.program_id(0) * (tm * TOP_K)

    def issue(r, c):
        for k in range(TOP_K):
            _row_copy(h2_ref, r, xs_out, dest_ref[base + r * TOP_K + k], sem).start(priority=k % 2)
        return c

    lax.fori_loop(0, tm, issue, 0)

    def drain(r, c):
        for k in range(TOP_K):
            _row_copy(h2_ref, r, xs_out, dest_ref[base + r * TOP_K + k], sem).wait()
        return c

    lax.fori_loop(0, tm, drain, 0)


def _dispatch(dest_flat, h2p, n_pad):
    T, W = h2p.shape
    tm = min(256, T)
    xs0 = jnp.zeros((n_pad, W), F32)
    return pl.pallas_call(
        functools.partial(_dispatch_kernel, tm=tm),
        out_shape=jax.ShapeDtypeStruct((n_pad, W), F32),
        grid_spec=pltpu.PrefetchScalarGridSpec(
            num_scalar_prefetch=1,
            grid=(T // tm,),
            in_specs=[
                pl.BlockSpec((tm, W), lambda i, d: (i, 0)),
                pl.BlockSpec(memory_space=pl.ANY),
            ],
            out_specs=pl.BlockSpec(memory_space=pl.ANY),
            scratch_shapes=[pltpu.SemaphoreType.DMA(())],
        ),
        input_output_aliases={2: 0},
        compiler_params=_cparams(("arbitrary",)),
        name="dispatch",
    )(dest_flat, h2p, xs0)


def _cast_rows(src, dst, rows, chunk=256):
    def body(c, carry):
        r = pl.multiple_of(c * chunk, chunk)
        dst[pl.ds(r, chunk), :] = src[pl.ds(r, chunk), :].astype(BF16)
        return carry
    lax.fori_loop(0, rows // chunk, body, 0)


def _gate_up_kernel(be_ref, nu_ref, first_ref, run_ref, nxt_ref, last_ref, nr_ref,
                    x_ref, bg_ref, bu_ref, w_hbm, o_ref, wbuf, wgb, wub, sem, *, tf, nf):
    f = pl.program_id(0)
    i = pl.program_id(1)
    used = i < nu_ref[0]
    D = wgb.shape[0]

    def w_copy(e, ff, slot, part):
        col = pl.multiple_of(part * D_FF + ff * tf, tf)
        return pltpu.make_async_copy(w_hbm.at[e, :, pl.ds(col, tf)], wbuf.at[slot, part], sem.at[slot])

    @pl.when(used & (first_ref[i] == 1))
    def _():
        run = f * nr_ref[0] + run_ref[i]
        slot = run & 1
        e = be_ref[i]

        @pl.when(run == 0)
        def _():
            w_copy(e, f, slot, 0).start()
            w_copy(e, f, slot, 1).start()

        w_copy(e, f, slot, 0).wait()
        w_copy(e, f, slot, 1).wait()
        last = last_ref[i]

        @pl.when((last == 0) | (f < nf - 1))
        def _():
            w_copy(nxt_ref[i], f + last, 1 - slot, 0).start()
            w_copy(nxt_ref[i], f + last, 1 - slot, 1).start()

        _cast_rows(wbuf.at[slot, 0], wgb, D)
        _cast_rows(wbuf.at[slot, 1], wub, D)

    @pl.when(used)
    def _():
        x = x_ref[...].astype(BF16)
        gate = jnp.minimum(_dot(x, wgb[...]) + bg_ref[...], SWIGLU_LIMIT)
        up = jnp.clip(_dot(x, wub[...]) + bu_ref[...], -SWIGLU_LIMIT, SWIGLU_LIMIT)
        glu = gate * _sigmoid(SWIGLU_ALPHA * gate)
        o_ref[...] = ((up + 1.0) * glu).astype(BF16)

    @pl.when(jnp.logical_not(used))
    def _():
        o_ref[...] = jnp.zeros(o_ref.shape, BF16)


def _gate_up(sched, xs, w_gate_up, b_gate_up3):
    n_pad, D = xs.shape
    tm = MOE_TM
    tf = 1024
    nf = D_FF // tf
    nb = n_pad // tm
    eff = lambda i, nu: jnp.minimum(i, nu[0] - 1)
    return pl.pallas_call(
        functools.partial(_gate_up_kernel, tf=tf, nf=nf),
        out_shape=jax.ShapeDtypeStruct((n_pad, D_FF), BF16),
        grid_spec=pltpu.PrefetchScalarGridSpec(
            num_scalar_prefetch=len(sched),
            grid=(nf, nb),
            in_specs=[
                pl.BlockSpec((tm, D), lambda f, i, be, nu, *_: (eff(i, nu), 0)),
                pl.BlockSpec((None, 1, tf), lambda f, i, be, nu, *_: (be[eff(i, nu)], 0, f)),
                pl.BlockSpec((None, 1, tf), lambda f, i, be, nu, *_: (be[eff(i, nu)], 0, nf + f)),
                pl.BlockSpec(memory_space=pl.ANY),
            ],
            out_specs=pl.BlockSpec((tm, tf), lambda f, i, *_: (i, f)),
            scratch_shapes=[
                pltpu.VMEM((2, 2, D, tf), F32),
                pltpu.VMEM((D, tf), BF16),
                pltpu.VMEM((D, tf), BF16),
                pltpu.SemaphoreType.DMA((2,)),
            ],
        ),
        compiler_params=_cparams(("arbitrary", "arbitrary")),
        name="gate_up",
    )(*sched, xs, b_gate_up3, b_gate_up3, w_gate_up)


def _down_kernel(be_ref, nu_ref, first_ref, run_ref, nxt_ref, last_ref, nr_ref,
                 h_ref, b_ref, w_hbm, o_ref, wbuf, wb, sem):
    i = pl.program_id(0)
    used = i < nu_ref[0]

    def w_copy(e, slot):
        return pltpu.make_async_copy(w_hbm.at[e], wbuf.at[slot], sem.at[slot])

    @pl.when(used & (first_ref[i] == 1))
    def _():
        run = run_ref[i]
        slot = run & 1
        e = be_ref[i]

        @pl.when(run == 0)
        def _():
            w_copy(e, slot).start()

        w_copy(e, slot).wait()

        @pl.when(last_ref[i] == 0)
        def _():
            w_copy(nxt_ref[i], 1 - slot).start()

        _cast_rows(wbuf.at[slot], wb, wb.shape[0])

    @pl.when(used)
    def _():
        o_ref[...] = _dot(h_ref[...], wb[...]) + b_ref[...]

    @pl.when(jnp.logical_not(used))
    def _():
        o_ref[...] = jnp.zeros(o_ref.shape, F32)


def _down(sched, h, w_down, b_down3):
    n_pad, F = h.shape
    D = w_down.shape[2]
    tm = MOE_TM
    nb = n_pad // tm
    eff = lambda i, nu: jnp.minimum(i, nu[0] - 1)
    return pl.pallas_call(
        _down_kernel,
        out_shape=jax.ShapeDtypeStruct((n_pad, D), F32),
        grid_spec=pltpu.PrefetchScalarGridSpec(
            num_scalar_prefetch=len(sched),
            grid=(nb,),
            in_specs=[
                pl.BlockSpec((tm, F), lambda i, be, nu, *_: (eff(i, nu), 0)),
                pl.BlockSpec((None, 1, D), lambda i, be, nu, *_: (be[eff(i, nu)], 0, 0)),
                pl.BlockSpec(memory_space=pl.ANY),
            ],
            out_specs=pl.BlockSpec((tm, D), lambda i, *_: (i, 0)),
            scratch_shapes=[
                pltpu.VMEM((2, F, D), F32),
                pltpu.VMEM((F, D), BF16),
                pltpu.SemaphoreType.DMA((2,)),
            ],
        ),
        compiler_params=_cparams(("arbitrary",)),
        name="down",
    )(*sched, h, b_down3, w_down)


def _combine_kernel(dest_ref, g_ref, x1_ref, fw_ref, y_hbm, o_ref, buf, sem, *, tm):
    base = pl.program_id(0) * (tm * TOP_K)

    def issue(r, c):
        for k in range(TOP_K):
            _row_copy(y_hbm, dest_ref[base + r * TOP_K + k], buf.at[k], r, sem).start(priority=k % 2)
        return c

    lax.fori_loop(0, tm, issue, 0)

    def drain(r, c):
        for k in range(TOP_K):
            _row_copy(y_hbm, dest_ref[base + r * TOP_K + k], buf.at[k], r, sem).wait()
        return c

    lax.fori_loop(0, tm, drain, 0)

    acc = x1_ref[...]
    gts = g_ref[...]
    for k in range(TOP_K):
        acc = acc + gts[:, k:k + 1] * buf[k]
    o_ref[...] = _rms_rows(acc, fw_ref[...])


def _combine(dest_flat, gates, x1, fw, y):
    T, D = x1.shape
    tm = min(256, T)
    return pl.pallas_call(
        functools.partial(_combine_kernel, tm=tm),
        out_shape=jax.ShapeDtypeStruct((T, D), F32),
        grid_spec=pltpu.PrefetchScalarGridSpec(
            num_scalar_prefetch=1,
            grid=(T // tm,),
            in_specs=[
                pl.BlockSpec((tm, LANES), lambda i, d: (i, 0)),
                pl.BlockSpec((tm, D), lambda i, d: (i, 0)),
                pl.BlockSpec((1, D), lambda i, d: (0, 0)),
                pl.BlockSpec(memory_space=pl.ANY),
            ],
            out_specs=pl.BlockSpec((tm, D), lambda i, d: (i, 0)),
            scratch_shapes=[pltpu.VMEM((TOP_K, tm, D), F32), pltpu.SemaphoreType.DMA(())],
        ),
        compiler_params=_cparams(("arbitrary",)),
        name="combine",
    )(dest_flat, gates, x1, fw, y)


def _rope_angles(pos):
    inv = ROPE_THETA ** (-jnp.arange(0, ROPE_DIM, 2, dtype=F32) / ROPE_DIM)
    return pos.astype(F32)[:, None] * inv[None, :]


def _rope_tables128(S):
    ang = _rope_angles(jnp.arange(S, dtype=I32))
    c, s = jnp.cos(ang), jnp.sin(ang)
    one = jnp.ones((S, 64 - ROPE_DIM), F32)
    zero = jnp.zeros((S, 64 - ROPE_DIM), F32)
    z8 = jnp.zeros((S, 8), F32)
    cos64 = jnp.concatenate([c, c, one], axis=1)
    sina64 = jnp.concatenate([z8, s, zero], axis=1)
    sinb64 = jnp.concatenate([-s, z8, zero], axis=1)
    t2 = lambda t: jnp.concatenate([t, t], axis=1)
    return t2(cos64), t2(sina64), t2(sinb64)


def _rope_tables_cmp(ncp):
    pos = jnp.arange(ncp, dtype=I32) * CMP_STRIDE + CMP_BLOCK - 1
    ang = _rope_angles(pos)
    c, s = jnp.cos(ang), jnp.sin(ang)
    cos64 = jnp.concatenate([c, c, jnp.ones((ncp, 64 - ROPE_DIM), F32)], axis=1)
    sin64 = jnp.concatenate([s, s, jnp.zeros((ncp, 64 - ROPE_DIM), F32)], axis=1)
    rot = np.zeros((64, 64), np.float32)
    for d in range(8):
        rot[d + 8, d] = -1.0
        rot[d, d + 8] = 1.0
    return cos64, sin64, jnp.asarray(rot, BF16)


def _overlap_matrix(S):
    ncp = S // CMP_STRIDE
    n_c = (S - CMP_BLOCK) // CMP_STRIDE + 1
    ovl = np.zeros((ncp, LANES), np.float32)
    for c in range(n_c):
        for l in range(CMP_BLOCK):
            ovl[c, (c * CMP_STRIDE + l) // SEL_BLOCK] += 1.0
    return jnp.asarray(ovl, BF16)


def _pack_w_in(w_in):
    D = w_in.shape[0]
    widths = [SSM_D_INNER, SSM_CONV_DIM, SSM_HEADS, NSA_D] + [KV_D] * 6 + [3 * NSA_HEADS]
    cuts = np.cumsum([0] + widths)
    seg = lambda k: w_in[:, cuts[k]:cuts[k + 1]]
    padw = lambda a, w: jnp.pad(a, ((0, 0), (0, w - a.shape[1])))
    parts = [seg(3), seg(0), seg(1)] + [seg(k) for k in range(4, 10)] + [padw(seg(2), LANES), padw(seg(10), LANES)]
    packed = jnp.concatenate(parts, axis=1)
    return padw(packed, PROJ_W).astype(BF16)


def _sched_columns(tab, n_blocks):
    per_tile = lambda c: tab[:n_blocks, c]
    once = lambda c: tab[0:1, c]
    return (per_tile(0), once(1), per_tile(2), per_tile(3), per_tile(4), per_tile(5), once(6))


def kernel(x, attn_norm_w, w_in, conv_w, conv_b, dt_bias, a_log, d_skip, ssm_norm_w, cmp_pos_emb, cmp_w1,
           cmp_b1, cmp_w2, cmp_b2, w_out, moe_norm_w, w_router, b_router, w_gate_up, b_gate_up, w_down,
           b_down, final_norm_w):
    B, S, D = x.shape
    T = B * S
    x2d = x.reshape(T, D)
    row = lambda v: v.reshape(1, -1)
    padl = lambda v: jnp.pad(v.reshape(1, -1), ((0, 0), (0, LANES - v.size)))

    proj = _in_proj(x2d, row(attn_norm_w[0]), _pack_w_in(w_in[0]))
    y_ssd = _ssd(proj, B, S, conv_w[0], row(conv_b[0]), padl(dt_bias[0]), padl(a_log[0]),
                 row(jnp.repeat(d_skip[0], SSM_HEAD_DIM)), row(ssm_norm_w[0]))
    cos, sina, sinb = _rope_tables128(S)
    q_rot, ks, vs, kw, vw, sg = _nsa_prep(proj, B, S, cos, sina, sinb)
    cosc, sinc, rot = _rope_tables_cmp(S // CMP_STRIDE)
    kvc = _compress(proj, B, S, cmp_pos_emb[0], cmp_w1[0], cmp_b1[0][:, None, :], cmp_w2[0],
                    cmp_b2[0][:, None, :], cosc, sinc, rot)
    y_nsa = _nsa_attn(q_rot, kvc, ks, vs, kw, vw, sg, _overlap_matrix(S), B, S)

    wr = jnp.pad(w_router[0], ((0, 0), (0, LANES - N_EXPERTS)))
    x1, h2, logits = _out_proj(y_ssd, y_nsa.reshape(T, NSA_D), x2d, w_out[0].astype(BF16),
                                row(moe_norm_w[0]), wr, padl(b_router[0]))

    n_pad = T * TOP_K + N_EXPERTS * MOE_TM
    n_blocks = n_pad // MOE_TM
    idx, rank, gates, cnt = _route(logits)
    dest, be_tab = _route_fin(cnt, idx, rank, n_blocks)
    dest_flat = dest[:, :TOP_K].reshape(-1)
    sched = _sched_columns(be_tab, n_blocks)
    xs = _dispatch(dest_flat, h2, n_pad)
    hmid = _gate_up(sched, xs, w_gate_up[0], b_gate_up[0][:, None, :])
    y = _down(sched, hmid, w_down[0], b_down[0][:, None, :])
    out = _combine(dest_flat, gates, x1, row(final_norm_w), y)
    return out.reshape(B, S, D)
```

```python
import functools
import math

import jax
import jax.numpy as jnp
import numpy as np
from jax import lax
from jax.experimental import pallas as pl
from jax.experimental.pallas import tpu as pltpu

F32 = jnp.float32
BF16 = jnp.bfloat16
I32 = jnp.int32
U32 = jnp.uint32

NORM_EPS = 1e-5
SSM_D_INNER = 1024
SSM_HEAD_DIM = 64
SSM_HEADS = 16
SSM_GROUPS = 4
SSM_STATE = 128
SSM_CONV = 4
SSM_CHUNK = 128
SSM_CONV_DIM = 2048
NSA_HEADS = 16
NSA_KV_HEADS = 4
NSA_HEAD_DIM = 64
NSA_D = 1024
KV_D = 256
CMP_BLOCK = 32
CMP_STRIDE = 16
CMP_HIDDEN = 512
SEL_BLOCK = 64
N_SELECT = 16
WINDOW = 512
ROPE_THETA = 500000.0
ROPE_DIM = 16
SEL_FORCE_SCORE = 1.0e4
N_EXPERTS = 32
TOP_K = 4
D_FF = 2048
SWIGLU_LIMIT = 7.0
SWIGLU_ALPHA = 1.702

LANES = 128
SUBLANES = 8
VMEM_LIMIT = 56 * 1024 * 1024

NEG = -1.0e30

COL_Q = 0
COL_Z = 1024
COL_XBC = 2048
COL_KV = 4096
COL_DT = 5632
COL_GATE = 5760
PROJ_W = 6144

MOE_TM = 256
Q_TILE = 128
SLC_TILE = 512
KE_W = LANES + 64
V_ROWS = 80


def _cparams(sem, vmem=VMEM_LIMIT):
    return pltpu.CompilerParams(dimension_semantics=sem, vmem_limit_bytes=vmem)


def _dot(a, b):
    return jnp.dot(a, b, preferred_element_type=F32)


def _dot_nt(a, b):
    return lax.dot_general(a, b, (((1,), (1,)), ((), ())), preferred_element_type=F32)


def _split3(b):
    b1 = b.astype(BF16)
    r1 = b - b1.astype(F32)
    b2 = r1.astype(BF16)
    r2 = r1 - b2.astype(F32)
    return b1, b2, r2.astype(BF16)


def _dot_exact_lhs(a_bf16, b):
    b1, b2, b3 = _split3(b)
    return _dot(a_bf16, b1) + _dot(a_bf16, b2) + _dot(a_bf16, b3)


def _dot_exact_rhs(a, b_bf16):
    a1, a2, a3 = _split3(a)
    return _dot(a1, b_bf16) + _dot(a2, b_bf16) + _dot(a3, b_bf16)


def _sigmoid(x):
    return 1.0 / (1.0 + jnp.exp(-x))


def _silu(x):
    return x * _sigmoid(x)


def _softplus(x):
    return jnp.maximum(x, 0.0) + jnp.log1p(jnp.exp(-jnp.abs(x)))


def _gelu_tanh(x):
    c = math.sqrt(2.0 / math.pi)
    return 0.5 * x * (1.0 + jnp.tanh(c * (x + 0.044715 * (x * x * x))))


def _rms_rows(x, w):
    ms = jnp.mean(x * x, axis=-1, keepdims=True)
    return x * lax.rsqrt(ms + NORM_EPS) * w


def _in_proj_kernel(x_ref, nw_ref, w_ref, o_ref, hn_ref, *, tm):
    @pl.when(pl.program_id(1) == 0)
    def _():
        def body(c, carry):
            r = pl.multiple_of(c * 128, 128)
            hn_ref[pl.ds(r, 128), :] = _rms_rows(x_ref[pl.ds(r, 128), :], nw_ref[...]).astype(BF16)
            return carry
        lax.fori_loop(0, tm // 128, body, 0)

    o_ref[...] = _dot(hn_ref[...], w_ref[...])


def _in_proj(x2d, nw, wp):
    T, D = x2d.shape
    NP = wp.shape[1]
    tm = min(1024, T)
    tn = 1024
    return pl.pallas_call(
        functools.partial(_in_proj_kernel, tm=tm),
        out_shape=jax.ShapeDtypeStruct((T, NP), F32),
        grid=(T // tm, NP // tn),
        in_specs=[
            pl.BlockSpec((tm, D), lambda i, j: (i, 0)),
            pl.BlockSpec((1, D), lambda i, j: (0, 0)),
            pl.BlockSpec((D, tn), lambda i, j: (0, j)),
        ],
        out_specs=pl.BlockSpec((tm, tn), lambda i, j: (i, j)),
        scratch_shapes=[pltpu.VMEM((tm, D), BF16)],
        compiler_params=_cparams(("parallel", "arbitrary")),
        name="in_proj",
    )(x2d, nw, wp)


def _rope128(x, cos, sina, sinb):
    return x * cos + pltpu.roll(x, 8, 1) * sina + pltpu.roll(x, 120, 1) * sinb


def _nsa_prep_kernel(q_ref, ks_ref, kw_ref, gl_ref, cos_ref, sina_ref, sinb_ref,
                     qo_ref, kso_ref, vso_ref, kwo_ref, vwo_ref, sg_ref):
    cos = cos_ref[...]
    sina = sina_ref[...]
    sinb = sinb_ref[...]
    scale = NSA_HEAD_DIM ** -0.5
    for c in range(NSA_D // LANES):
        xq = q_ref[:, c * LANES:(c + 1) * LANES]
        qo_ref[:, c * LANES:(c + 1) * LANES] = (_rope128(xq, cos, sina, sinb) * scale).astype(BF16)
    ts = q_ref.shape[0]
    key = pl.program_id(1) * ts + lax.broadcasted_iota(I32, (ts, LANES), 0)
    blk1h = jnp.where(lax.shift_right_logical(key, 6) == lax.broadcasted_iota(I32, (ts, LANES), 1), 1.0, 0.0)
    ones = jnp.ones((V_ROWS - 64, LANES), BF16)
    for src, ko, vo, ext in ((ks_ref, kso_ref, vso_ref, True), (kw_ref, kwo_ref, vwo_ref, False)):
        for c in range(KV_D // LANES):
            kr = _rope128(src[:, c * LANES:(c + 1) * LANES], cos, sina, sinb)
            vt = src[:, KV_D + c * LANES:KV_D + (c + 1) * LANES].T
            for half in range(2):
                g = 2 * c + half
                kg = kr[:, half * 64:(half + 1) * 64].astype(BF16)
                if ext:
                    ko[g, :, 0:LANES] = blk1h.astype(BF16)
                    ko[g, :, LANES:LANES + 64] = kg
                else:
                    ko[g] = kg
                for j in range(ts // LANES):
                    vo[g, j, 0:64, :] = vt[half * 64:(half + 1) * 64, j * LANES:(j + 1) * LANES].astype(BF16)
                    vo[g, j, 64:V_ROWS, :] = ones
    sg_ref[...] = _sigmoid(gl_ref[...])


def _nsa_prep(proj, B, S, cos, sina, sinb):
    ts = min(512, S)
    nst = S // ts
    G = NSA_KV_HEADS
    kv_spec = pl.BlockSpec((None, G, ts, 64), lambda b, s: (b, 0, s, 0))
    kv_shape = jax.ShapeDtypeStruct((B, G, S, 64), BF16)
    ke_spec = pl.BlockSpec((None, G, ts, KE_W), lambda b, s: (b, 0, s, 0))
    ke_shape = jax.ShapeDtypeStruct((B, G, S, KE_W), BF16)
    vt_spec = pl.BlockSpec((None, G, ts // LANES, V_ROWS, LANES), lambda b, s: (b, 0, s, 0, 0))
    vt_shape = jax.ShapeDtypeStruct((B, G, S // LANES, V_ROWS, LANES), BF16)
    return pl.pallas_call(
        _nsa_prep_kernel,
        out_shape=(
            jax.ShapeDtypeStruct((B, S, NSA_D), BF16),
            ke_shape, vt_shape, kv_shape, vt_shape,
            jax.ShapeDtypeStruct((B, S, LANES), F32),
        ),
        grid=(B, nst),
        in_specs=[
            pl.BlockSpec((ts, NSA_D), lambda b, s: (b * nst + s, COL_Q // NSA_D)),
            pl.BlockSpec((ts, 512), lambda b, s: (b * nst + s, (COL_KV + 512) // 512)),
            pl.BlockSpec((ts, 512), lambda b, s: (b * nst + s, (COL_KV + 1024) // 512)),
            pl.BlockSpec((ts, LANES), lambda b, s: (b * nst + s, COL_GATE // LANES)),
            pl.BlockSpec((ts, LANES), lambda b, s: (s, 0)),
            pl.BlockSpec((ts, LANES), lambda b, s: (s, 0)),
            pl.BlockSpec((ts, LANES), lambda b, s: (s, 0)),
        ],
        out_specs=(
            pl.BlockSpec((None, ts, NSA_D), lambda b, s: (b, s, 0)),
            ke_spec, vt_spec, kv_spec, vt_spec,
            pl.BlockSpec((None, ts, LANES), lambda b, s: (b, s, 0)),
        ),
        compiler_params=_cparams(("parallel", "parallel")),
        name="nsa_prep",
    )(proj, proj, proj, proj, cos, sina, sinb)


def _ssd_kernel(z_ref, xbc_ref, dt_ref, cw_ref, cb_ref, dtb_ref, alog_ref, dsk_ref, nw_ref,
                y_ref, buf, st):
    L = SSM_CHUNK
    c = pl.program_id(1)

    @pl.when(c == 0)
    def _():
        buf[0:8, :] = jnp.zeros((8, SSM_CONV_DIM), F32)
        st[...] = jnp.zeros(st.shape, F32)

    buf[8:8 + L, :] = xbc_ref[...]
    acc = jnp.broadcast_to(cb_ref[...], (L, SSM_CONV_DIM))
    for k in range(SSM_CONV):
        acc = acc + cw_ref[k:k + 1, :] * buf[5 + k:5 + k + L, :]
    xc = _silu(acc)
    buf[0:8, :] = xbc_ref[L - 8:L, :]

    lane = lax.broadcasted_iota(I32, (L, LANES), 1)
    row = lax.broadcasted_iota(I32, (L, LANES), 0)
    lo = lane < 64
    dtv = jnp.where(lane < SSM_HEADS, _softplus(dt_ref[...] + dtb_ref[...]), 0.0)
    a = -jnp.exp(alog_ref[...])
    tri = jnp.where(row >= lane, 1.0, 0.0).astype(BF16)
    acs = _dot_exact_lhs(tri, dtv * a)
    acs_t = acs.T
    causal = row >= lane

    ys = []
    for g in range(SSM_GROUPS):
        bg = xc[:, SSM_D_INNER + g * SSM_STATE:SSM_D_INNER + (g + 1) * SSM_STATE]
        cg = xc[:, SSM_D_INNER + (SSM_GROUPS + g) * SSM_STATE:SSM_D_INNER + (SSM_GROUPS + g + 1) * SSM_STATE]
        bgt = bg.T.astype(BF16)
        cgb = cg.astype(BF16)
        gmat = _dot(cgb, bgt)
        for p in (2 * g, 2 * g + 1):
            h0, h1 = 2 * p, 2 * p + 1
            xs_pair = xc[:, p * LANES:(p + 1) * LANES]
            col0 = acs[:, h0:h0 + 1]
            col1 = acs[:, h1:h1 + 1]
            colp = jnp.where(lo, col0, col1)
            last = jnp.where(lo[0:1, :], acs[L - 1:L, h0:h0 + 1], acs[L - 1:L, h1:h1 + 1])
            x = xs_pair * jnp.where(lo, dtv[:, h0:h0 + 1], dtv[:, h1:h1 + 1])
            m0 = (gmat * jnp.where(causal, jnp.exp(col0 - acs_t[h0:h0 + 1, :]), 0.0)).astype(BF16)
            m1 = (gmat * jnp.where(causal, jnp.exp(col1 - acs_t[h1:h1 + 1, :]), 0.0)).astype(BF16)
            y_diag = _dot(m0, jnp.where(lo, x, 0.0).astype(BF16)) + _dot(m1, jnp.where(lo, 0.0, x).astype(BF16))
            s_prev = st[p]
            y_off = _dot(cgb, s_prev.astype(BF16)) * jnp.exp(colp)
            w = (x * jnp.exp(last - colp)).astype(BF16)
            st[p] = jnp.exp(last) * s_prev + _dot(bgt, w)
            ys.append(y_diag + y_off + dsk_ref[:, p * LANES:(p + 1) * LANES] * xs_pair)
    y = jnp.concatenate(ys, axis=1)
    gte = y * _silu(z_ref[...])
    gw = SSM_D_INNER // SSM_GROUPS
    outs = []
    for k in range(SSM_GROUPS):
        gk = gte[:, k * gw:(k + 1) * gw]
        ms = jnp.mean(gk * gk, axis=-1, keepdims=True)
        outs.append(gk * lax.rsqrt(ms + NORM_EPS))
    y_ref[...] = (jnp.concatenate(outs, axis=1) * nw_ref[...]).astype(BF16)


def _ssd(proj, B, S, conv_w, conv_b, dtb, alog, dskip, nw):
    L = SSM_CHUNK
    nc = S // L
    small = lambda shp: pl.BlockSpec(shp, lambda b, c: (0, 0))
    return pl.pallas_call(
        _ssd_kernel,
        out_shape=jax.ShapeDtypeStruct((B * S, SSM_D_INNER), BF16),
        grid=(B, nc),
        in_specs=[
            pl.BlockSpec((L, SSM_D_INNER), lambda b, c: (b * nc + c, COL_Z // SSM_D_INNER)),
            pl.BlockSpec((L, SSM_CONV_DIM), lambda b, c: (b * nc + c, COL_XBC // SSM_CONV_DIM)),
            pl.BlockSpec((L, LANES), lambda b, c: (b * nc + c, COL_DT // LANES)),
            small((SSM_CONV, SSM_CONV_DIM)),
            small((1, SSM_CONV_DIM)),
            small((1, LANES)),
            small((1, LANES)),
            small((1, SSM_D_INNER)),
            small((1, SSM_D_INNER)),
        ],
        out_specs=pl.BlockSpec((L, SSM_D_INNER), lambda b, c: (b * nc + c, 0)),
        scratch_shapes=[
            pltpu.VMEM((L + 8, SSM_CONV_DIM), F32),
            pltpu.VMEM((SSM_HEADS // 2, SSM_STATE, LANES), F32),
        ],
        compiler_params=_cparams(("parallel", "arbitrary")),
        name="ssd",
    )(proj, proj, proj, conv_w, conv_b, dtb, alog, dskip, nw)


def _compress_kernel(xa_ref, xb_ref, pos_ref, w1_ref, b1_ref, w2_ref, b2_ref, cos_ref, sin_ref, rot_ref,
                     o_ref, ot_ref, w1b, feat, *, ncp):
    kv = pl.program_id(1)
    outs = []
    half = CMP_STRIDE
    hw = half * 64
    w1b[...] = w1_ref[...].astype(BF16)
    w2 = w2_ref[...].astype(BF16)
    G = NSA_KV_HEADS
    is_k = kv == 0
    for g in range(G):
        x_ref = (xa_ref, xb_ref)[g // 2]
        for l in range(half):
            xg = x_ref[pl.ds(l, ncp, stride=half), :][:, (g % 2) * 64:(g % 2 + 1) * 64]
            feat[:, l * 64:(l + 1) * 64] = xg + pos_ref[l:l + 1, :]
            feat[:, hw + l * 64:hw + (l + 1) * 64] = xg + pos_ref[half + l:half + l + 1, :]
        first = _dot(feat[:, 0:hw].astype(BF16), w1b[0:hw, :])
        second = _dot(feat[:, hw:2 * hw].astype(BF16), w1b[hw:2 * hw, :])
        h = first + pltpu.roll(second, ncp - 1, 0) + b1_ref[...]
        h = _gelu_tanh(h)
        o = _dot(h.astype(BF16), w2) + b2_ref[...]
        roped = o * cos_ref[...] + _dot_exact_rhs(o, rot_ref[...]) * sin_ref[...]
        outs.append(jnp.where(is_k, roped, o))
        o_ref[g] = outs[g]
    for c in range(G // 2):
        t = jnp.concatenate([outs[2 * c], outs[2 * c + 1]], axis=1).T
        ot_ref[2 * c] = t[0:64, :]
        ot_ref[2 * c + 1] = t[64:128, :]


def _compress(proj, B, S, pos, w1, b1, w2, b2, cosc, sinc, rot):
    ncp = S // CMP_STRIDE
    G = NSA_KV_HEADS
    kvblk = COL_KV // KV_D
    return pl.pallas_call(
        functools.partial(_compress_kernel, ncp=ncp),
        out_shape=(jax.ShapeDtypeStruct((B, 2, G, ncp, 64), F32),
                   jax.ShapeDtypeStruct((B, 2, G, 64, ncp), F32)),
        grid=(B, 2),
        in_specs=[
            pl.BlockSpec((S, LANES), lambda b, k: (b, 2 * (kvblk + k))),
            pl.BlockSpec((S, LANES), lambda b, k: (b, 2 * (kvblk + k) + 1)),
            pl.BlockSpec((None, CMP_BLOCK, 64), lambda b, k: (k, 0, 0)),
            pl.BlockSpec((None, CMP_BLOCK * 64, CMP_HIDDEN), lambda b, k: (k, 0, 0)),
            pl.BlockSpec((None, 1, CMP_HIDDEN), lambda b, k: (k, 0, 0)),
            pl.BlockSpec((None, CMP_HIDDEN, 64), lambda b, k: (k, 0, 0)),
            pl.BlockSpec((None, 1, 64), lambda b, k: (k, 0, 0)),
            pl.BlockSpec((ncp, 64), lambda b, k: (0, 0)),
            pl.BlockSpec((ncp, 64), lambda b, k: (0, 0)),
            pl.BlockSpec((64, 64), lambda b, k: (0, 0)),
        ],
        out_specs=(pl.BlockSpec((None, None, G, ncp, 64), lambda b, k: (b, k, 0, 0, 0)),
                   pl.BlockSpec((None, None, G, 64, ncp), lambda b, k: (b, k, 0, 0, 0))),
        scratch_shapes=[pltpu.VMEM((CMP_BLOCK * 64, CMP_HIDDEN), BF16),
                        pltpu.VMEM((ncp, CMP_BLOCK * 64), F32)],
        compiler_params=_cparams(("parallel", "parallel")),
        name="compress",
    )(proj, proj, pos, w1, b1, w2, b2, cosc, sinc, rot)


def _nsa_attn_kernel(q_ref, kc_ref, vct_ref, ks_ref, vst_ref, kw_ref, vwt_ref, sg_ref, ovlt_ref,
                     o_ref, *, ncp, n_c, n_sel, k_sel):
    g = pl.program_id(1)
    qi = pl.program_id(2)
    tq = Q_TILE
    R = NSA_HEADS // NSA_KV_HEADS
    t0 = qi * tq
    qt = q_ref[...].astype(F32).T
    q4t = jnp.concatenate([qt[r * 64:(r + 1) * 64, :] for r in range(R)], axis=1).astype(BF16)
    rep = lambda a: jnp.concatenate([a] * R, axis=1)
    tpos = t0 + lax.broadcasted_iota(I32, (1, tq), 1)
    cmax = lambda a: jnp.max(a, axis=0, keepdims=True)
    csum = lambda a: jnp.sum(a, axis=0, keepdims=True)
    jl = lax.broadcasted_iota(I32, (LANES, tq), 0)

    def online(carry, s, vt):
        m_old, acc = carry
        m_new = jnp.maximum(m_old, cmax(s))
        alpha = jnp.exp(m_old - m_new)
        return m_new, alpha * acc + _dot(vt, jnp.exp(s - m_new).astype(BF16))

    sgt = sg_ref[...].T
    gts = [[csum(jnp.where(jl == 3 * (g * R + r) + k, sgt, 0.0)) for k in range(3)] for r in range(R)]

    span = min(WINDOW + tq, kw_ref.shape[0])
    w0 = pl.multiple_of(jnp.maximum(t0 + tq - span, 0), tq)
    jw = lax.shift_right_logical(w0, int(math.log2(tq)))
    vt = jnp.concatenate([vwt_ref[jw + c] for c in range(span // LANES)], axis=1)
    dist = tpos - (w0 + lax.broadcasted_iota(I32, (span, tq), 0))
    s = _dot(kw_ref[pl.ds(w0, span), :], q4t) + rep(jnp.where((dist >= 0) & (dist < WINDOW), 0.0, NEG))
    acc_win = _dot(vt, jnp.exp(s - cmax(s)).astype(BF16))
    o_win = acc_win[0:64] / acc_win[64:65]

    r_i = lax.broadcasted_iota(I32, (tq, tq), 0)
    c_i = lax.broadcasted_iota(I32, (tq, tq), 1)
    kd = ks_ref[pl.ds(pl.multiple_of(t0, tq), tq), :][:, LANES:KE_W]
    s = _dot(kd, q4t) + rep(jnp.where(r_i <= c_i, 0.0, NEG))
    m_diag, acc_diag = online((jnp.full((1, R * tq), NEG, F32), jnp.zeros((V_ROWS, R * tq), F32)), s, vst_ref[qi])

    cidx = lax.broadcasted_iota(I32, (ncp, tq), 0)
    maskc = rep(jnp.where((cidx * CMP_STRIDE + (CMP_BLOCK - 1) <= tpos) & (cidx < n_c), 1.0, 0.0)) > 0.5
    s = _dot(kc_ref[...].astype(BF16), q4t)
    e = jnp.where(maskc, jnp.exp(s - cmax(jnp.where(maskc, s, NEG))), 0.0)
    p = e / jnp.maximum(csum(e), 1e-30)
    o_cmp = _dot(vct_ref[...].astype(BF16), p.astype(BF16))
    psum = p[:, 0:tq]
    for r in range(1, R):
        psum = psum + p[:, r * tq:(r + 1) * tq]

    p_hi = psum.astype(BF16)
    p_lo = (psum - p_hi.astype(F32)).astype(BF16)
    imp = _dot(ovlt_ref[...], p_hi) + _dot(ovlt_ref[...], p_lo)
    cur = lax.shift_right_logical(tpos, 6)
    forced = (jl == 0) | (jl == cur) | (jl == cur - 1)
    st = jnp.where(forced, SEL_FORCE_SCORE, jnp.where(jl <= cur, imp, -SEL_FORCE_SCORE))
    nsp = max(n_sel, SUBLANES)
    sel_rows = []
    for v in range(nsp // SUBLANES):
        gv = st[v * SUBLANES:(v + 1) * SUBLANES, :]
        jrow = v * SUBLANES + lax.broadcasted_iota(I32, (SUBLANES, tq), 0)
        cnt = jnp.zeros((SUBLANES, tq), F32)
        for i in range(n_sel):
            ri = st[i:i + 1, :]
            if v * SUBLANES + SUBLANES - 1 < i:
                beats = ri > gv
            elif v * SUBLANES > i:
                beats = ri >= gv
            else:
                beats = ((ri >= gv) & (jrow > i)) | ((ri > gv) & (jrow < i))
            cnt = cnt + jnp.where(beats, 1.0, 0.0)
        sel_rows.append(jnp.where((cnt < float(k_sel)) & (jrow < n_sel), 1.0, 0.0))
    sel_rows.append(jnp.zeros((LANES - nsp, tq), F32))
    sel = jnp.concatenate(sel_rows, axis=0)

    tk = min(SLC_TILE, ks_ref.shape[0])
    nsub = tk // LANES
    before = jl < lax.shift_right_logical(t0, 6)
    qext = jnp.concatenate([rep(jnp.where(before & (sel > 0.5), 0.0, NEG)).astype(BF16), q4t], axis=0)
    n_tiles = ks_ref.shape[0] // tk

    def scores(j):
        jj = jnp.minimum(j, n_tiles - 1)
        return _dot(ks_ref[pl.ds(pl.multiple_of(jj * tk, tk), tk), :], qext)

    def slc_body(j, carry):
        m_old, acc, s = carry
        s_next = scores(j + 1)
        vt = jnp.concatenate([vst_ref[j * nsub + c] for c in range(nsub)], axis=1)
        m_new, acc = online((m_old, acc), s, vt)
        return m_new, acc, s_next

    n_main = lax.shift_right_logical(t0 + (tk - 1), int(math.log2(tk)))
    _, acc_slc, _ = lax.fori_loop(0, n_main, slc_body, (m_diag, acc_diag, scores(0)))
    o_slc = acc_slc[0:64] / acc_slc[64:65]

    outs = []
    for r in range(R):
        cols = slice(r * tq, (r + 1) * tq)
        outs.append(gts[r][0] * o_cmp[:, cols] + gts[r][1] * o_slc[:, cols] + gts[r][2] * o_win[:, cols])
    for c in range(R // 2):
        o_ref[:, c * LANES:(c + 1) * LANES] = jnp.concatenate(outs[2 * c:2 * c + 2], axis=0).T.astype(BF16)


def _nsa_attn(q_rot, kvc, kvct, kse, vst, kw, vwt, sg, ovlt, B, S):
    G = NSA_KV_HEADS
    nq = S // Q_TILE
    ncp = S // CMP_STRIDE
    n_c = (S - CMP_BLOCK) // CMP_STRIDE + 1
    n_sel = S // SEL_BLOCK
    k_sel = min(N_SELECT, n_sel)
    keys = pl.BlockSpec((None, None, S, 64), lambda b, g, i: (b, g, 0, 0))
    keys_ext = pl.BlockSpec((None, None, S, KE_W), lambda b, g, i: (b, g, 0, 0))
    vals = pl.BlockSpec((None, None, S // LANES, V_ROWS, LANES), lambda b, g, i: (b, g, 0, 0, 0))
    return pl.pallas_call(
        functools.partial(_nsa_attn_kernel, ncp=ncp, n_c=n_c, n_sel=n_sel, k_sel=k_sel),
        out_shape=jax.ShapeDtypeStruct((B, S, NSA_D), BF16),
        grid=(B, G, nq),
        in_specs=[
            pl.BlockSpec((None, Q_TILE, KV_D), lambda b, g, i: (b, i, g)),
            pl.BlockSpec((None, None, None, ncp, 64), lambda b, g, i: (b, 0, g, 0, 0)),
            pl.BlockSpec((None, None, None, 64, ncp), lambda b, g, i: (b, 1, g, 0, 0)),
            keys_ext, vals, keys, vals,
            pl.BlockSpec((None, Q_TILE, LANES), lambda b, g, i: (b, i, 0)),
            pl.BlockSpec(ovlt.shape, lambda b, g, i: (0, 0)),
        ],
        out_specs=pl.BlockSpec((None, Q_TILE, KV_D), lambda b, g, i: (b, i, g)),
        compiler_params=_cparams(("parallel", "parallel", "arbitrary")),
        name="nsa_attn",
    )(q_rot, kvc, kvct, kse, vst, kw, vwt, sg, ovlt)


def _out_proj_kernel(ys_ref, yn_ref, x_ref, w_ref, nw_ref, wr_ref, br_ref, x1_ref, h2_ref, lg_ref):
    half = ys_ref.shape[1]
    y = _dot(ys_ref[...], w_ref[0:half, :]) + _dot(yn_ref[...], w_ref[half:2 * half, :])
    x1 = x_ref[...] + y
    x1_ref[...] = x1
    hn = _rms_rows(x1, nw_ref[...])
    h2_ref[...] = hn
    h1, h2, _ = _split3(hn)
    w1, w2, _ = _split3(wr_ref[...])
    lg_ref[...] = _dot(h1, w1) + _dot(h1, w2) + _dot(h2, w1) + br_ref[...]


def _out_proj(y_ssd, y_nsa, x2d, w_out_b, nw, wr, br):
    T, D = x2d.shape
    tm = min(512, T)
    half = y_ssd.shape[1]
    return pl.pallas_call(
        _out_proj_kernel,
        out_shape=(
            jax.ShapeDtypeStruct((T, D), F32),
            jax.ShapeDtypeStruct((T, D), F32),
            jax.ShapeDtypeStruct((T, LANES), F32),
        ),
        grid=(T // tm,),
        in_specs=[
            pl.BlockSpec((tm, half), lambda i: (i, 0)),
            pl.BlockSpec((tm, half), lambda i: (i, 0)),
            pl.BlockSpec((tm, D), lambda i: (i, 0)),
            pl.BlockSpec((D, D), lambda i: (0, 0)),
            pl.BlockSpec((1, D), lambda i: (0, 0)),
            pl.BlockSpec((D, LANES), lambda i: (0, 0)),
            pl.BlockSpec((1, LANES), lambda i: (0, 0)),
        ],
        out_specs=(
            pl.BlockSpec((tm, D), lambda i: (i, 0)),
            pl.BlockSpec((tm, D), lambda i: (i, 0)),
            pl.BlockSpec((tm, LANES), lambda i: (i, 0)),
        ),
        compiler_params=_cparams(("parallel",)),
        name="out_proj",
    )(y_ssd, y_nsa, x2d, w_out_b, nw, wr, br)


def _route_kernel(lg_ref, idx_ref, rank_ref, gate_ref, cnt_ref, carry, *, tr):
    i = pl.program_id(0)

    @pl.when(i == 0)
    def _():
        carry[...] = jnp.zeros(carry.shape, F32)

    lane = lax.broadcasted_iota(I32, (tr, LANES), 1)
    lanef = lane.astype(F32)
    l = jnp.where(lane < N_EXPERTS, lg_ref[...], NEG)
    ohs, vals, idxs = [], [], []
    for _ in range(TOP_K):
        m = jnp.max(l, axis=-1, keepdims=True)
        idx = jnp.min(jnp.where(l == m, lanef, float(LANES)), axis=-1, keepdims=True)
        oh = lanef == idx
        l = jnp.where(oh, 2.0 * NEG, l)
        ohs.append(oh)
        vals.append(m)
        idxs.append(idx)
    es = [jnp.exp(v - vals[0]) for v in vals]
    den = es[0] + es[1] + es[2] + es[3]
    oh_sum = jnp.zeros((tr, LANES), F32)
    for oh in ohs:
        oh_sum = oh_sum + jnp.where(oh, 1.0, 0.0)
    r_i = lax.broadcasted_iota(I32, (tr, tr), 0)
    c_i = lax.broadcasted_iota(I32, (tr, tr), 1)
    strict = jnp.where(r_i > c_i, 1.0, 0.0).astype(BF16)
    base = carry[0:1, :] + _dot(strict, oh_sum.astype(BF16))
    idx_o = jnp.zeros((tr, LANES), F32)
    rank_o = jnp.zeros((tr, LANES), F32)
    gate_o = jnp.zeros((tr, LANES), F32)
    for k in range(TOP_K):
        rk = jnp.sum(jnp.where(ohs[k], base, 0.0), axis=-1, keepdims=True)
        idx_o = jnp.where(lane == k, idxs[k], idx_o)
        rank_o = jnp.where(lane == k, rk, rank_o)
        gate_o = jnp.where(lane == k, es[k] / den, gate_o)
    idx_ref[...] = idx_o.astype(I32)
    rank_ref[...] = rank_o.astype(I32)
    gate_ref[...] = gate_o
    carry[...] = carry[...] + jnp.sum(oh_sum, axis=0, keepdims=True)
    cnt_ref[...] = carry[...]


def _route(logits):
    T = logits.shape[0]
    tr = min(512, T)
    tok = pl.BlockSpec((tr, LANES), lambda i: (i, 0))
    return pl.pallas_call(
        functools.partial(_route_kernel, tr=tr),
        out_shape=(
            jax.ShapeDtypeStruct((T, LANES), I32),
            jax.ShapeDtypeStruct((T, LANES), I32),
            jax.ShapeDtypeStruct((T, LANES), F32),
            jax.ShapeDtypeStruct((SUBLANES, LANES), F32),
        ),
        grid=(T // tr,),
        in_specs=[tok],
        out_specs=(tok, tok, tok, pl.BlockSpec((SUBLANES, LANES), lambda i: (0, 0))),
        scratch_shapes=[pltpu.VMEM((SUBLANES, LANES), F32)],
        compiler_params=_cparams(("arbitrary",)),
        name="route",
    )(logits)


def _route_fin_kernel(cnt_ref, idx_ref, rank_ref, dest_ref, be_ref, pads_ref, *, tr, nbp, tm_shift):
    lane = lax.broadcasted_iota(I32, (SUBLANES, LANES), 1)
    cnt = cnt_ref[...].astype(I32)
    tm = 1 << tm_shift
    nblk = jnp.where(lane < N_EXPERTS, lax.shift_right_logical(cnt + (tm - 1), tm_shift), 0)
    r_i = lax.broadcasted_iota(I32, (LANES, LANES), 0)
    c_i = lax.broadcasted_iota(I32, (LANES, LANES), 1)
    upper = jnp.where(r_i <= c_i, 1.0, 0.0).astype(BF16)
    end_blk = _dot(nblk.astype(F32).astype(BF16), upper)
    start_row = (end_blk - nblk.astype(F32)) * float(tm)
    idx = idx_ref[...].astype(F32)
    lane_t = lax.broadcasted_iota(I32, (tr, LANES), 1)
    lane_f = lane_t.astype(F32)
    dest = jnp.zeros((tr, LANES), F32)
    for k in range(TOP_K):
        e_k = jnp.sum(jnp.where(lane_t == k, idx, 0.0), axis=-1, keepdims=True)
        s_k = jnp.sum(jnp.where(lane_f == e_k, start_row[0:1, :], 0.0), axis=-1, keepdims=True)
        dest = jnp.where(lane_t == k, s_k, dest)
    dest_ref[...] = dest.astype(I32) + jnp.where(lane_t < TOP_K, rank_ref[...], 0)
    blk = lax.broadcasted_iota(I32, (nbp, LANES), 0).astype(F32)
    lane_b = lax.broadcasted_iota(I32, (nbp, LANES), 1)
    lane_bf = lane_b.astype(F32)
    rsum = lambda v: jnp.sum(v, axis=-1, keepdims=True)
    end_row = end_blk[0:1, :]
    nblk_row = nblk.astype(F32)[0:1, :]
    is_exp = lane_b < N_EXPERTS
    nonempty = is_exp & (nblk_row > 0.0)
    be = jnp.minimum(rsum(jnp.where(is_exp & (end_row <= blk), 1.0, 0.0)), float(N_EXPERTS - 1))
    n_used = rsum(jnp.where(lane_b == N_EXPERTS - 1, end_row, 0.0))
    start_of = rsum(jnp.where(lane_bf == be, end_row - nblk_row, 0.0))
    first = jnp.where((start_of == blk[:, 0:1]) & (blk[:, 0:1] < n_used), 1.0, 0.0)
    run = rsum(jnp.where(nonempty & (lane_bf < be), 1.0, 0.0))
    none = float(LANES)
    nxt = jnp.min(jnp.where(nonempty & (lane_bf > be), lane_bf, none), axis=-1, keepdims=True)
    first_e = jnp.min(jnp.where(nonempty, lane_bf, none), axis=-1, keepdims=True)
    last = jnp.where(nxt == none, 1.0, 0.0)
    nxt = jnp.where(nxt == none, first_e, nxt)
    n_runs = rsum(jnp.where(nonempty, 1.0, 0.0))
    tab = jnp.zeros((nbp, LANES), F32)
    for c, v in enumerate((be, n_used, first, run, nxt, last, n_runs)):
        tab = jnp.where(lane_b == c, v, tab)
    be_ref[...] = tab.astype(I32)
    sub = lax.broadcasted_iota(I32, (SUBLANES, LANES), 0)
    pad_start = start_row.astype(I32) + cnt
    pad_len = jnp.where(lane < N_EXPERTS, nblk * tm - cnt, 0)
    pads_ref[...] = jnp.where(sub == 0, pad_start, jnp.where(sub == 1, pad_len, 0))


def _route_fin(cnt, idx, rank, n_blocks):
    T = idx.shape[0]
    tr = min(512, T)
    nbp = ((n_blocks + SUBLANES - 1) // SUBLANES) * SUBLANES
    tok = pl.BlockSpec((tr, LANES), lambda i: (i, 0))
    return pl.pallas_call(
        functools.partial(_route_fin_kernel, tr=tr, nbp=nbp, tm_shift=int(math.log2(MOE_TM))),
        out_shape=(
            jax.ShapeDtypeStruct((T, LANES), I32),
            jax.ShapeDtypeStruct((nbp, LANES), I32),
            jax.ShapeDtypeStruct((SUBLANES, LANES), I32),
        ),
        grid=(T // tr,),
        in_specs=[pl.BlockSpec((SUBLANES, LANES), lambda i: (0, 0)), tok, tok],
        out_specs=(tok, pl.BlockSpec((nbp, LANES), lambda i: (0, 0)),
                   pl.BlockSpec((SUBLANES, LANES), lambda i: (0, 0))),
        compiler_params=_cparams(("arbitrary",)),
        name="route_fin",
    )(cnt, idx, rank)


def _row_copy(src_ref, src_row, dst_ref, dst_row, sem):
    return pltpu.make_async_copy(src_ref.at[pl.ds(src_row, 1)], dst_ref.at[pl.ds(dst_row, 1)], sem)


def _dispatch_kernel(dest_ref, pstart_ref, plen_ref, nu_ref, h2_ref, xs_out, zeros, sem, *, tm, nb):
    base = pl.program_id(0) * (tm * TOP_K)

    @pl.when(pl.program_id(0) == 0)
    def _():
        zeros[...] = jnp.zeros(zeros.shape, F32)

        def pad_rows(start):
            def per_expert(e, c):
                def one(r, c2):
                    cp = _row_copy(zeros, 0, xs_out, pstart_ref[e] + r, sem)
                    cp.start() if start else cp.wait()
                    return c2
                lax.fori_loop(0, plen_ref[e], one, 0)
                return c
            lax.fori_loop(0, N_EXPERTS, per_expert, 0)

            def tail(t, c):
                row0 = pl.multiple_of((nu_ref[0] + t) * MOE_TM, MOE_TM)
                cp = pltpu.make_async_copy(zeros, xs_out.at[pl.ds(row0, MOE_TM)], sem)
                cp.start() if start else cp.wait()
                return c
            lax.fori_loop(0, nb - nu_ref[0], tail, 0)

        pad_rows(True)
        pad_rows(False)

    def issue(r, c):
        for k in range(TOP_K):
            _row_copy(h2_ref, r, xs_out, dest_ref[base + r * TOP_K + k], sem).start(priority=k % 2)
        return c

    lax.fori_loop(0, tm, issue, 0, unroll=4)

    def drain(r, c):
        for k in range(TOP_K):
            _row_copy(h2_ref, r, xs_out, dest_ref[base + r * TOP_K + k], sem).wait()
        return c

    lax.fori_loop(0, tm, drain, 0, unroll=4)


def _dispatch(dest_flat, pads, nu, h2, n_pad):
    T, W = h2.shape
    tm = min(256, T)
    return pl.pallas_call(
        functools.partial(_dispatch_kernel, tm=tm, nb=n_pad // MOE_TM),
        out_shape=jax.ShapeDtypeStruct((n_pad, W), F32),
        grid_spec=pltpu.PrefetchScalarGridSpec(
            num_scalar_prefetch=4,
            grid=(T // tm,),
            in_specs=[pl.BlockSpec((tm, W), lambda i, *_: (i, 0))],
            out_specs=pl.BlockSpec(memory_space=pl.ANY),
            scratch_shapes=[pltpu.VMEM((MOE_TM, W), F32), pltpu.SemaphoreType.DMA(())],
        ),
        compiler_params=_cparams(("arbitrary",)),
        name="dispatch",
    )(dest_flat, pads[0, :N_EXPERTS], pads[1, :N_EXPERTS], nu, h2)


def _cast_rows(src, dst, rows, chunk=256):
    def body(c, carry):
        r = pl.multiple_of(c * chunk, chunk)
        dst[pl.ds(r, chunk), :] = src[pl.ds(r, chunk), :].astype(BF16)
        return carry
    lax.fori_loop(0, rows // chunk, body, 0)


def _gate_up_kernel(be_ref, nu_ref, first_ref, run_ref, nxt_ref, last_ref, nr_ref,
                    x_ref, bg_ref, bu_ref, w_hbm, o_ref, wbuf, wgb, wub, sem, *, tf, nf):
    f = pl.program_id(0)
    i = pl.program_id(1)
    used = i < nu_ref[0]
    D = wgb.shape[0]

    def w_copy(e, ff, slot, part):
        col = pl.multiple_of(part * D_FF + ff * tf, tf)
        return pltpu.make_async_copy(w_hbm.at[e, :, pl.ds(col, tf)], wbuf.at[slot, part], sem.at[slot])

    @pl.when(used & (first_ref[i] == 1))
    def _():
        run = f * nr_ref[0] + run_ref[i]
        slot = run & 1
        e = be_ref[i]

        @pl.when(run == 0)
        def _():
            w_copy(e, f, slot, 0).start()
            w_copy(e, f, slot, 1).start()

        w_copy(e, f, slot, 0).wait()
        w_copy(e, f, slot, 1).wait()
        last = last_ref[i]

        @pl.when((last == 0) | (f < nf - 1))
        def _():
            w_copy(nxt_ref[i], f + last, 1 - slot, 0).start()
            w_copy(nxt_ref[i], f + last, 1 - slot, 1).start()

        _cast_rows(wbuf.at[slot, 0], wgb, D)
        _cast_rows(wbuf.at[slot, 1], wub, D)

    @pl.when(used)
    def _():
        x = x_ref[...].astype(BF16)
        gate = jnp.minimum(_dot(x, wgb[...]) + bg_ref[...], SWIGLU_LIMIT)
        up = jnp.clip(_dot(x, wub[...]) + bu_ref[...], -SWIGLU_LIMIT, SWIGLU_LIMIT)
        glu = gate * _sigmoid(SWIGLU_ALPHA * gate)
        o_ref[...] = ((up + 1.0) * glu).astype(BF16)

    @pl.when(jnp.logical_not(used))
    def _():
        o_ref[...] = jnp.zeros(o_ref.shape, BF16)


def _gate_up(sched, xs, w_gate_up, b_gate_up3):
    n_pad, D = xs.shape
    tm = MOE_TM
    tf = 1024
    nf = D_FF // tf
    nb = n_pad // tm
    eff = lambda i, nu: jnp.minimum(i, nu[0] - 1)
    return pl.pallas_call(
        functools.partial(_gate_up_kernel, tf=tf, nf=nf),
        out_shape=jax.ShapeDtypeStruct((n_pad, D_FF), BF16),
        grid_spec=pltpu.PrefetchScalarGridSpec(
            num_scalar_prefetch=len(sched),
            grid=(nf, nb),
            in_specs=[
                pl.BlockSpec((tm, D), lambda f, i, be, nu, *_: (eff(i, nu), 0)),
                pl.BlockSpec((None, 1, tf), lambda f, i, be, nu, *_: (be[eff(i, nu)], 0, f)),
                pl.BlockSpec((None, 1, tf), lambda f, i, be, nu, *_: (be[eff(i, nu)], 0, nf + f)),
                pl.BlockSpec(memory_space=pl.ANY),
            ],
            out_specs=pl.BlockSpec((tm, tf), lambda f, i, *_: (i, f)),
            scratch_shapes=[
                pltpu.VMEM((2, 2, D, tf), F32),
                pltpu.VMEM((D, tf), BF16),
                pltpu.VMEM((D, tf), BF16),
                pltpu.SemaphoreType.DMA((2,)),
            ],
        ),
        compiler_params=_cparams(("arbitrary", "arbitrary")),
        name="gate_up",
    )(*sched, xs, b_gate_up3, b_gate_up3, w_gate_up)


def _down_kernel(be_ref, nu_ref, first_ref, run_ref, nxt_ref, last_ref, nr_ref,
                 h_ref, b_ref, w_hbm, o_ref, wbuf, wb, sem):
    i = pl.program_id(0)
    used = i < nu_ref[0]

    def w_copy(e, slot):
        return pltpu.make_async_copy(w_hbm.at[e], wbuf.at[slot], sem.at[slot])

    @pl.when(used & (first_ref[i] == 1))
    def _():
        run = run_ref[i]
        slot = run & 1
        e = be_ref[i]

        @pl.when(run == 0)
        def _():
            w_copy(e, slot).start()

        w_copy(e, slot).wait()

        @pl.when(last_ref[i] == 0)
        def _():
            w_copy(nxt_ref[i], 1 - slot).start()

        _cast_rows(wbuf.at[slot], wb, wb.shape[0])

    @pl.when(used)
    def _():
        o_ref[...] = _dot(h_ref[...], wb[...]) + b_ref[...]

    @pl.when(jnp.logical_not(used))
    def _():
        o_ref[...] = jnp.zeros(o_ref.shape, F32)


def _down(sched, h, w_down, b_down3):
    n_pad, F = h.shape
    D = w_down.shape[2]
    tm = MOE_TM
    nb = n_pad // tm
    eff = lambda i, nu: jnp.minimum(i, nu[0] - 1)
    return pl.pallas_call(
        _down_kernel,
        out_shape=jax.ShapeDtypeStruct((n_pad, D), F32),
        grid_spec=pltpu.PrefetchScalarGridSpec(
            num_scalar_prefetch=len(sched),
            grid=(nb,),
            in_specs=[
                pl.BlockSpec((tm, F), lambda i, be, nu, *_: (eff(i, nu), 0)),
                pl.BlockSpec((None, 1, D), lambda i, be, nu, *_: (be[eff(i, nu)], 0, 0)),
                pl.BlockSpec(memory_space=pl.ANY),
            ],
            out_specs=pl.BlockSpec((tm, D), lambda i, *_: (i, 0)),
            scratch_shapes=[
                pltpu.VMEM((2, F, D), F32),
                pltpu.VMEM((F, D), BF16),
                pltpu.SemaphoreType.DMA((2,)),
            ],
        ),
        compiler_params=_cparams(("arbitrary",)),
        name="down",
    )(*sched, h, b_down3, w_down)


def _combine_kernel(dest_ref, g_ref, x1_ref, fw_ref, y_hbm, o_ref, buf, sem, *, tm):
    base = pl.program_id(0) * (tm * TOP_K)

    def issue(r, c):
        for k in range(TOP_K):
            _row_copy(y_hbm, dest_ref[base + r * TOP_K + k], buf.at[k], r, sem).start(priority=k % 2)
        return c

    lax.fori_loop(0, tm, issue, 0, unroll=4)

    def drain(r, c):
        for k in range(TOP_K):
            _row_copy(y_hbm, dest_ref[base + r * TOP_K + k], buf.at[k], r, sem).wait()
        return c

    lax.fori_loop(0, tm, drain, 0, unroll=4)

    acc = x1_ref[...]
    gts = g_ref[...]
    for k in range(TOP_K):
        acc = acc + gts[:, k:k + 1] * buf[k]
    o_ref[...] = _rms_rows(acc, fw_ref[...])


def _combine(dest_flat, gates, x1, fw, y):
    T, D = x1.shape
    tm = min(256, T)
    return pl.pallas_call(
        functools.partial(_combine_kernel, tm=tm),
        out_shape=jax.ShapeDtypeStruct((T, D), F32),
        grid_spec=pltpu.PrefetchScalarGridSpec(
            num_scalar_prefetch=1,
            grid=(T // tm,),
            in_specs=[
                pl.BlockSpec((tm, LANES), lambda i, d: (i, 0)),
                pl.BlockSpec((tm, D), lambda i, d: (i, 0)),
                pl.BlockSpec((1, D), lambda i, d: (0, 0)),
                pl.BlockSpec(memory_space=pl.ANY),
            ],
            out_specs=pl.BlockSpec((tm, D), lambda i, d: (i, 0)),
            scratch_shapes=[pltpu.VMEM((TOP_K, tm, D), F32), pltpu.SemaphoreType.DMA(())],
        ),
        compiler_params=_cparams(("arbitrary",)),
        name="combine",
    )(dest_flat, gates, x1, fw, y)


def _rope_angles(pos):
    inv = ROPE_THETA ** (-jnp.arange(0, ROPE_DIM, 2, dtype=F32) / ROPE_DIM)
    return pos.astype(F32)[:, None] * inv[None, :]


def _rope_tables128(S):
    ang = _rope_angles(jnp.arange(S, dtype=I32))
    c, s = jnp.cos(ang), jnp.sin(ang)
    one = jnp.ones((S, 64 - ROPE_DIM), F32)
    zero = jnp.zeros((S, 64 - ROPE_DIM), F32)
    z8 = jnp.zeros((S, 8), F32)
    cos64 = jnp.concatenate([c, c, one], axis=1)
    sina64 = jnp.concatenate([z8, s, zero], axis=1)
    sinb64 = jnp.concatenate([-s, z8, zero], axis=1)
    t2 = lambda t: jnp.concatenate([t, t], axis=1)
    return t2(cos64), t2(sina64), t2(sinb64)


def _rope_tables_cmp(ncp):
    pos = jnp.arange(ncp, dtype=I32) * CMP_STRIDE + CMP_BLOCK - 1
    ang = _rope_angles(pos)
    c, s = jnp.cos(ang), jnp.sin(ang)
    cos64 = jnp.concatenate([c, c, jnp.ones((ncp, 64 - ROPE_DIM), F32)], axis=1)
    sin64 = jnp.concatenate([s, s, jnp.zeros((ncp, 64 - ROPE_DIM), F32)], axis=1)
    rot = np.zeros((64, 64), np.float32)
    for d in range(8):
        rot[d + 8, d] = -1.0
        rot[d, d + 8] = 1.0
    return cos64, sin64, jnp.asarray(rot, BF16)


def _overlap_matrix_t(S):
    ncp = S // CMP_STRIDE
    n_c = (S - CMP_BLOCK) // CMP_STRIDE + 1
    ovl = np.zeros((LANES, ncp), np.float32)
    for c in range(n_c):
        for l in range(CMP_BLOCK):
            ovl[(c * CMP_STRIDE + l) // SEL_BLOCK, c] += 1.0
    return jnp.asarray(ovl, BF16)


def _pack_w_in(w_in):
    D = w_in.shape[0]
    widths = [SSM_D_INNER, SSM_CONV_DIM, SSM_HEADS, NSA_D] + [KV_D] * 6 + [3 * NSA_HEADS]
    cuts = np.cumsum([0] + widths)
    seg = lambda k: w_in[:, cuts[k]:cuts[k + 1]]
    padw = lambda a, w: jnp.pad(a, ((0, 0), (0, w - a.shape[1])))
    parts = [seg(3), seg(0), seg(1)] + [seg(k) for k in range(4, 10)] + [padw(seg(2), LANES), padw(seg(10), LANES)]
    packed = jnp.concatenate(parts, axis=1)
    return padw(packed, PROJ_W).astype(BF16)


def _sched_columns(tab, n_blocks):
    per_tile = lambda c: tab[:n_blocks, c]
    once = lambda c: tab[0:1, c]
    return (per_tile(0), once(1), per_tile(2), per_tile(3), per_tile(4), per_tile(5), once(6))


def kernel(x, attn_norm_w, w_in, conv_w, conv_b, dt_bias, a_log, d_skip, ssm_norm_w, cmp_pos_emb, cmp_w1,
           cmp_b1, cmp_w2, cmp_b2, w_out, moe_norm_w, w_router, b_router, w_gate_up, b_gate_up, w_down,
           b_down, final_norm_w):
    B, S, D = x.shape
    T = B * S
    x2d = x.reshape(T, D)
    row = lambda v: v.reshape(1, -1)
    padl = lambda v: jnp.pad(v.reshape(1, -1), ((0, 0), (0, LANES - v.size)))

    proj = _in_proj(x2d, row(attn_norm_w[0]), _pack_w_in(w_in[0]))
    y_ssd = _ssd(proj, B, S, conv_w[0], row(conv_b[0]), padl(dt_bias[0]), padl(a_log[0]),
                 row(jnp.repeat(d_skip[0], SSM_HEAD_DIM)), row(ssm_norm_w[0]))
    cos, sina, sinb = _rope_tables128(S)
    q_rot, kse, vst, kw, vwt, sg = _nsa_prep(proj, B, S, cos, sina, sinb)
    cosc, sinc, rot = _rope_tables_cmp(S // CMP_STRIDE)
    kvc, kvct = _compress(proj, B, S, cmp_pos_emb[0], cmp_w1[0], cmp_b1[0][:, None, :], cmp_w2[0],
                          cmp_b2[0][:, None, :], cosc, sinc, rot)
    y_nsa = _nsa_attn(q_rot, kvc, kvct, kse, vst, kw, vwt, sg, _overlap_matrix_t(S), B, S)

    wr = jnp.pad(w_router[0], ((0, 0), (0, LANES - N_EXPERTS)))
    x1, h2, logits = _out_proj(y_ssd, y_nsa.reshape(T, NSA_D), x2d, w_out[0].astype(BF16),
                                row(moe_norm_w[0]), wr, padl(b_router[0]))

    n_pad = T * TOP_K + N_EXPERTS * MOE_TM
    n_blocks = n_pad // MOE_TM
    idx, rank, gates, cnt = _route(logits)
    dest, be_tab, pads = _route_fin(cnt, idx, rank, n_blocks)
    dest_flat = dest[:, :TOP_K].reshape(-1)
    sched = _sched_columns(be_tab, n_blocks)
    xs = _dispatch(dest_flat, pads, sched[1], h2, n_pad)
    hmid = _gate_up(sched, xs, w_gate_up[0], b_gate_up[0][:, None, :])
    y = _down(sched, hmid, w_down[0], b_down[0][:, None, :])
    out = _combine(dest_flat, gates, x1, row(final_norm_w), y)
    return out.reshape(B, S, D)
```

```python
import functools
import math

import jax
import jax.numpy as jnp
import numpy as np
from jax import lax
from jax.experimental import pallas as pl
from jax.experimental.pallas import tpu as pltpu

F32 = jnp.float32
BF16 = jnp.bfloat16
I32 = jnp.int32
U32 = jnp.uint32

NORM_EPS = 1e-5
SSM_D_INNER = 1024
SSM_HEAD_DIM = 64
SSM_HEADS = 16
SSM_GROUPS = 4
SSM_STATE = 128
SSM_CONV = 4
SSM_CHUNK = 128
SSM_CONV_DIM = 2048
NSA_HEADS = 16
NSA_KV_HEADS = 4
NSA_HEAD_DIM = 64
NSA_D = 1024
KV_D = 256
CMP_BLOCK = 32
CMP_STRIDE = 16
CMP_HIDDEN = 512
SEL_BLOCK = 64
N_SELECT = 16
WINDOW = 512
ROPE_THETA = 500000.0
ROPE_DIM = 16
SEL_FORCE_SCORE = 1.0e4
N_EXPERTS = 32
TOP_K = 4
D_FF = 2048
SWIGLU_LIMIT = 7.0
SWIGLU_ALPHA = 1.702

LANES = 128
SUBLANES = 8
VMEM_LIMIT = 56 * 1024 * 1024

NEG = -1.0e30

COL_Q = 0
COL_Z = 1024
COL_XBC = 2048
COL_KV = 4096
COL_DT = 5632
COL_GATE = 5760
PROJ_W = 6144

MOE_TM = 256
Q_TILE = 256
SLC_TILE = 512
KE_W = LANES + 64
V_ROWS = 80


def _cparams(sem, vmem=VMEM_LIMIT):
    return pltpu.CompilerParams(dimension_semantics=sem, vmem_limit_bytes=vmem)


def _dot(a, b):
    return jnp.dot(a, b, preferred_element_type=F32)


def _dot_nt(a, b):
    return lax.dot_general(a, b, (((1,), (1,)), ((), ())), preferred_element_type=F32)


def _split3(b):
    b1 = b.astype(BF16)
    r1 = b - b1.astype(F32)
    b2 = r1.astype(BF16)
    r2 = r1 - b2.astype(F32)
    return b1, b2, r2.astype(BF16)


def _dot_exact_lhs(a_bf16, b):
    b1, b2, b3 = _split3(b)
    return _dot(a_bf16, b1) + _dot(a_bf16, b2) + _dot(a_bf16, b3)


def _dot_exact_rhs(a, b_bf16):
    a1, a2, a3 = _split3(a)
    return _dot(a1, b_bf16) + _dot(a2, b_bf16) + _dot(a3, b_bf16)


def _sigmoid(x):
    return 1.0 / (1.0 + jnp.exp(-x))


def _silu(x):
    return x * _sigmoid(x)


def _softplus(x):
    return jnp.maximum(x, 0.0) + jnp.log(1.0 + jnp.exp(-jnp.abs(x)))


def _gelu_tanh(x):
    c = math.sqrt(2.0 / math.pi)
    return 0.5 * x * (1.0 + jnp.tanh(c * (x + 0.044715 * (x * x * x))))


def _rms_rows(x, w):
    ms = jnp.mean(x * x, axis=-1, keepdims=True)
    return x * lax.rsqrt(ms + NORM_EPS) * w


def _in_proj_kernel(x_ref, nw_ref, w_ref, o_ref, hn_ref, *, tm):
    @pl.when(pl.program_id(1) == 0)
    def _():
        def body(c, carry):
            r = pl.multiple_of(c * 128, 128)
            hn_ref[pl.ds(r, 128), :] = _rms_rows(x_ref[pl.ds(r, 128), :], nw_ref[...]).astype(BF16)
            return carry
        lax.fori_loop(0, tm // 128, body, 0)

    o_ref[...] = _dot(hn_ref[...], w_ref[...])


def _in_proj(x2d, nw, wp):
    T, D = x2d.shape
    NP = wp.shape[1]
    tm = min(1024, T)
    tn = 1024
    return pl.pallas_call(
        functools.partial(_in_proj_kernel, tm=tm),
        out_shape=jax.ShapeDtypeStruct((T, NP), F32),
        grid=(T // tm, NP // tn),
        in_specs=[
            pl.BlockSpec((tm, D), lambda i, j: (i, 0)),
            pl.BlockSpec((1, D), lambda i, j: (0, 0)),
            pl.BlockSpec((D, tn), lambda i, j: (0, j)),
        ],
        out_specs=pl.BlockSpec((tm, tn), lambda i, j: (i, j)),
        scratch_shapes=[pltpu.VMEM((tm, D), BF16)],
        compiler_params=_cparams(("parallel", "arbitrary")),
        name="in_proj",
    )(x2d, nw, wp)


def _rope128(x, cos, sina, sinb):
    return x * cos + pltpu.roll(x, 8, 1) * sina + pltpu.roll(x, 120, 1) * sinb


def _nsa_prep_kernel(q_ref, ks_ref, kw_ref, gl_ref, cos_ref, sina_ref, sinb_ref,
                     qo_ref, kso_ref, vso_ref, kwo_ref, vwo_ref, sg_ref):
    cos = cos_ref[...]
    sina = sina_ref[...]
    sinb = sinb_ref[...]
    scale = NSA_HEAD_DIM ** -0.5
    for c in range(NSA_D // LANES):
        xq = q_ref[:, c * LANES:(c + 1) * LANES]
        qo_ref[:, c * LANES:(c + 1) * LANES] = (_rope128(xq, cos, sina, sinb) * scale).astype(BF16)
    ts = q_ref.shape[0]
    key = pl.program_id(1) * ts + lax.broadcasted_iota(I32, (ts, LANES), 0)
    blk1h = jnp.where(lax.shift_right_logical(key, 6) == lax.broadcasted_iota(I32, (ts, LANES), 1), 1.0, 0.0)
    ones = jnp.ones((V_ROWS - 64, LANES), BF16)
    for src, ko, vo, ext in ((ks_ref, kso_ref, vso_ref, True), (kw_ref, kwo_ref, vwo_ref, False)):
        for c in range(KV_D // LANES):
            kr = _rope128(src[:, c * LANES:(c + 1) * LANES], cos, sina, sinb)
            vt = src[:, KV_D + c * LANES:KV_D + (c + 1) * LANES].T
            for half in range(2):
                g = 2 * c + half
                kg = kr[:, half * 64:(half + 1) * 64].astype(BF16)
                if ext:
                    ko[g, :, 0:LANES] = blk1h.astype(BF16)
                    ko[g, :, LANES:LANES + 64] = kg
                else:
                    ko[g] = kg
                for j in range(ts // LANES):
                    vo[g, j, 0:64, :] = vt[half * 64:(half + 1) * 64, j * LANES:(j + 1) * LANES].astype(BF16)
                    vo[g, j, 64:V_ROWS, :] = ones
    sg_ref[...] = _sigmoid(gl_ref[...])


def _nsa_prep(proj, B, S, cos, sina, sinb):
    ts = min(512, S)
    nst = S // ts
    G = NSA_KV_HEADS
    kv_spec = pl.BlockSpec((None, G, ts, 64), lambda b, s: (b, 0, s, 0))
    kv_shape = jax.ShapeDtypeStruct((B, G, S, 64), BF16)
    ke_spec = pl.BlockSpec((None, G, ts, KE_W), lambda b, s: (b, 0, s, 0))
    ke_shape = jax.ShapeDtypeStruct((B, G, S, KE_W), BF16)
    vt_spec = pl.BlockSpec((None, G, ts // LANES, V_ROWS, LANES), lambda b, s: (b, 0, s, 0, 0))
    vt_shape = jax.ShapeDtypeStruct((B, G, S // LANES, V_ROWS, LANES), BF16)
    return pl.pallas_call(
        _nsa_prep_kernel,
        out_shape=(
            jax.ShapeDtypeStruct((B, S, NSA_D), BF16),
            ke_shape, vt_shape, kv_shape, vt_shape,
            jax.ShapeDtypeStruct((B, S, LANES), F32),
        ),
        grid=(B, nst),
        in_specs=[
            pl.BlockSpec((ts, NSA_D), lambda b, s: (b * nst + s, COL_Q // NSA_D)),
            pl.BlockSpec((ts, 512), lambda b, s: (b * nst + s, (COL_KV + 512) // 512)),
            pl.BlockSpec((ts, 512), lambda b, s: (b * nst + s, (COL_KV + 1024) // 512)),
            pl.BlockSpec((ts, LANES), lambda b, s: (b * nst + s, COL_GATE // LANES)),
            pl.BlockSpec((ts, LANES), lambda b, s: (s, 0)),
            pl.BlockSpec((ts, LANES), lambda b, s: (s, 0)),
            pl.BlockSpec((ts, LANES), lambda b, s: (s, 0)),
        ],
        out_specs=(
            pl.BlockSpec((None, ts, NSA_D), lambda b, s: (b, s, 0)),
            ke_spec, vt_spec, kv_spec, vt_spec,
            pl.BlockSpec((None, ts, LANES), lambda b, s: (b, s, 0)),
        ),
        compiler_params=_cparams(("parallel", "parallel")),
        name="nsa_prep",
    )(proj, proj, proj, proj, cos, sina, sinb)


def _ssd_kernel(z_ref, xbc_ref, dt_ref, cw_ref, cb_ref, dtb_ref, alog_ref, dsk_ref, nw_ref,
                y_ref, buf, st):
    L = SSM_CHUNK
    c = pl.program_id(1)

    @pl.when(c == 0)
    def _():
        buf[0:8, :] = jnp.zeros((8, SSM_CONV_DIM), F32)
        st[...] = jnp.zeros(st.shape, F32)

    buf[8:8 + L, :] = xbc_ref[...]
    acc = jnp.broadcast_to(cb_ref[...], (L, SSM_CONV_DIM))
    for k in range(SSM_CONV):
        acc = acc + cw_ref[k:k + 1, :] * buf[5 + k:5 + k + L, :]
    xc = _silu(acc)
    buf[0:8, :] = xbc_ref[L - 8:L, :]

    lane = lax.broadcasted_iota(I32, (L, LANES), 1)
    row = lax.broadcasted_iota(I32, (L, LANES), 0)
    lo = lane < 64
    dtv = jnp.where(lane < SSM_HEADS, _softplus(dt_ref[...] + dtb_ref[...]), 0.0)
    a = -jnp.exp(alog_ref[...])
    tri = jnp.where(row >= lane, 1.0, 0.0).astype(BF16)
    acs = _dot_exact_lhs(tri, dtv * a)
    acs_t = acs.T
    causal = row >= lane

    ys = []
    for g in range(SSM_GROUPS):
        bg = xc[:, SSM_D_INNER + g * SSM_STATE:SSM_D_INNER + (g + 1) * SSM_STATE]
        cg = xc[:, SSM_D_INNER + (SSM_GROUPS + g) * SSM_STATE:SSM_D_INNER + (SSM_GROUPS + g + 1) * SSM_STATE]
        bgt = bg.T.astype(BF16)
        cgb = cg.astype(BF16)
        gmat = _dot(cgb, bgt)
        for p in (2 * g, 2 * g + 1):
            h0, h1 = 2 * p, 2 * p + 1
            xs_pair = xc[:, p * LANES:(p + 1) * LANES]
            col0 = acs[:, h0:h0 + 1]
            col1 = acs[:, h1:h1 + 1]
            colp = jnp.where(lo, col0, col1)
            last = jnp.where(lo[0:1, :], acs[L - 1:L, h0:h0 + 1], acs[L - 1:L, h1:h1 + 1])
            x = xs_pair * jnp.where(lo, dtv[:, h0:h0 + 1], dtv[:, h1:h1 + 1])
            m0 = (gmat * jnp.where(causal, jnp.exp(col0 - acs_t[h0:h0 + 1, :]), 0.0)).astype(BF16)
            m1 = (gmat * jnp.where(causal, jnp.exp(col1 - acs_t[h1:h1 + 1, :]), 0.0)).astype(BF16)
            y_diag = _dot(m0, jnp.where(lo, x, 0.0).astype(BF16)) + _dot(m1, jnp.where(lo, 0.0, x).astype(BF16))
            s_prev = st[p]
            y_off = _dot(cgb, s_prev.astype(BF16)) * jnp.exp(colp)
            w = (x * jnp.exp(last - colp)).astype(BF16)
            st[p] = jnp.exp(last) * s_prev + _dot(bgt, w)
            ys.append(y_diag + y_off + dsk_ref[:, p * LANES:(p + 1) * LANES] * xs_pair)
    y = jnp.concatenate(ys, axis=1)
    gte = y * _silu(z_ref[...])
    gw = SSM_D_INNER // SSM_GROUPS
    outs = []
    for k in range(SSM_GROUPS):
        gk = gte[:, k * gw:(k + 1) * gw]
        ms = jnp.mean(gk * gk, axis=-1, keepdims=True)
        outs.append(gk * lax.rsqrt(ms + NORM_EPS))
    y_ref[...] = (jnp.concatenate(outs, axis=1) * nw_ref[...]).astype(BF16)


def _ssd(proj, B, S, conv_w, conv_b, dtb, alog, dskip, nw):
    L = SSM_CHUNK
    nc = S // L
    small = lambda shp: pl.BlockSpec(shp, lambda b, c: (0, 0))
    return pl.pallas_call(
        _ssd_kernel,
        out_shape=jax.ShapeDtypeStruct((B * S, SSM_D_INNER), BF16),
        grid=(B, nc),
        in_specs=[
            pl.BlockSpec((L, SSM_D_INNER), lambda b, c: (b * nc + c, COL_Z // SSM_D_INNER)),
            pl.BlockSpec((L, SSM_CONV_DIM), lambda b, c: (b * nc + c, COL_XBC // SSM_CONV_DIM)),
            pl.BlockSpec((L, LANES), lambda b, c: (b * nc + c, COL_DT // LANES)),
            small((SSM_CONV, SSM_CONV_DIM)),
            small((1, SSM_CONV_DIM)),
            small((1, LANES)),
            small((1, LANES)),
            small((1, SSM_D_INNER)),
            small((1, SSM_D_INNER)),
        ],
        out_specs=pl.BlockSpec((L, SSM_D_INNER), lambda b, c: (b * nc + c, 0)),
        scratch_shapes=[
            pltpu.VMEM((L + 8, SSM_CONV_DIM), F32),
            pltpu.VMEM((SSM_HEADS // 2, SSM_STATE, LANES), F32),
        ],
        compiler_params=_cparams(("parallel", "arbitrary")),
        name="ssd",
    )(proj, proj, proj, conv_w, conv_b, dtb, alog, dskip, nw)


def _compress_kernel(xa_ref, xb_ref, pos_ref, w1_ref, b1_ref, w2_ref, b2_ref, cos_ref, sin_ref, rot_ref,
                     o_ref, ot_ref, w1b, feat, *, ncp):
    kv = pl.program_id(1)
    outs = []
    half = CMP_STRIDE
    hw = half * 64
    w1b[...] = w1_ref[...].astype(BF16)
    w2 = w2_ref[...].astype(BF16)
    G = NSA_KV_HEADS
    is_k = kv == 0
    for g in range(G):
        x_ref = (xa_ref, xb_ref)[g // 2]
        for l in range(half):
            xg = x_ref[pl.ds(l, ncp, stride=half), :][:, (g % 2) * 64:(g % 2 + 1) * 64]
            feat[:, l * 64:(l + 1) * 64] = xg + pos_ref[l:l + 1, :]
            feat[:, hw + l * 64:hw + (l + 1) * 64] = xg + pos_ref[half + l:half + l + 1, :]
        first = _dot(feat[:, 0:hw].astype(BF16), w1b[0:hw, :])
        second = _dot(feat[:, hw:2 * hw].astype(BF16), w1b[hw:2 * hw, :])
        h = first + pltpu.roll(second, ncp - 1, 0) + b1_ref[...]
        h = _gelu_tanh(h)
        o = _dot(h.astype(BF16), w2) + b2_ref[...]
        roped = o * cos_ref[...] + _dot_exact_rhs(o, rot_ref[...]) * sin_ref[...]
        outs.append(jnp.where(is_k, roped, o))
        o_ref[g] = outs[g]
    for c in range(G // 2):
        t = jnp.concatenate([outs[2 * c], outs[2 * c + 1]], axis=1).T
        ot_ref[2 * c] = t[0:64, :]
        ot_ref[2 * c + 1] = t[64:128, :]


def _compress(proj, B, S, pos, w1, b1, w2, b2, cosc, sinc, rot):
    ncp = S // CMP_STRIDE
    G = NSA_KV_HEADS
    kvblk = COL_KV // KV_D
    return pl.pallas_call(
        functools.partial(_compress_kernel, ncp=ncp),
        out_shape=(jax.ShapeDtypeStruct((B, 2, G, ncp, 64), F32),
                   jax.ShapeDtypeStruct((B, 2, G, 64, ncp), F32)),
        grid=(B, 2),
        in_specs=[
            pl.BlockSpec((S, LANES), lambda b, k: (b, 2 * (kvblk + k))),
            pl.BlockSpec((S, LANES), lambda b, k: (b, 2 * (kvblk + k) + 1)),
            pl.BlockSpec((None, CMP_BLOCK, 64), lambda b, k: (k, 0, 0)),
            pl.BlockSpec((None, CMP_BLOCK * 64, CMP_HIDDEN), lambda b, k: (k, 0, 0)),
            pl.BlockSpec((None, 1, CMP_HIDDEN), lambda b, k: (k, 0, 0)),
            pl.BlockSpec((None, CMP_HIDDEN, 64), lambda b, k: (k, 0, 0)),
            pl.BlockSpec((None, 1, 64), lambda b, k: (k, 0, 0)),
            pl.BlockSpec((ncp, 64), lambda b, k: (0, 0)),
            pl.BlockSpec((ncp, 64), lambda b, k: (0, 0)),
            pl.BlockSpec((64, 64), lambda b, k: (0, 0)),
        ],
        out_specs=(pl.BlockSpec((None, None, G, ncp, 64), lambda b, k: (b, k, 0, 0, 0)),
                   pl.BlockSpec((None, None, G, 64, ncp), lambda b, k: (b, k, 0, 0, 0))),
        scratch_shapes=[pltpu.VMEM((CMP_BLOCK * 64, CMP_HIDDEN), BF16),
                        pltpu.VMEM((ncp, CMP_BLOCK * 64), F32)],
        compiler_params=_cparams(("parallel", "parallel")),
        name="compress",
    )(proj, proj, pos, w1, b1, w2, b2, cosc, sinc, rot)


def _nsa_attn_kernel(q_ref, kc_ref, vct_ref, ks_ref, vst_ref, kw_ref, vwt_ref, sg_ref, ovlt_ref,
                     o_ref, *, ncp, n_c, n_sel, k_sel):
    g = pl.program_id(1)
    qi = pl.program_id(2)
    tq = Q_TILE
    R = NSA_HEADS // NSA_KV_HEADS
    t0 = qi * tq
    qt = q_ref[...].astype(F32).T
    q4t = jnp.concatenate([qt[r * 64:(r + 1) * 64, :] for r in range(R)], axis=1).astype(BF16)
    rep = lambda a: jnp.concatenate([a] * R, axis=1)
    tpos = t0 + lax.broadcasted_iota(I32, (1, tq), 1)
    cmax = lambda a: jnp.max(a, axis=0, keepdims=True)
    csum = lambda a: jnp.sum(a, axis=0, keepdims=True)
    jl = lax.broadcasted_iota(I32, (LANES, tq), 0)

    def online(carry, s, vt):
        m_old, acc = carry
        m_new = jnp.maximum(m_old, cmax(s))
        alpha = jnp.exp(m_old - m_new)
        return m_new, alpha * acc + _dot(vt, jnp.exp(s - m_new).astype(BF16))

    sgt = sg_ref[...].T
    gts = [[csum(jnp.where(jl == 3 * (g * R + r) + k, sgt, 0.0)) for k in range(3)] for r in range(R)]

    span = min(WINDOW + tq, kw_ref.shape[0])
    w0 = pl.multiple_of(jnp.maximum(t0 + tq - span, 0), tq)
    jw = lax.shift_right_logical(w0, int(math.log2(LANES)))
    vt = jnp.concatenate([vwt_ref[jw + c] for c in range(span // LANES)], axis=1)
    dist = tpos - (w0 + lax.broadcasted_iota(I32, (span, tq), 0))
    s = _dot(kw_ref[pl.ds(w0, span), :], q4t) + rep(jnp.where((dist >= 0) & (dist < WINDOW), 0.0, NEG))
    acc_win = _dot(vt, jnp.exp(s - cmax(s)).astype(BF16))
    o_win = acc_win[0:64] / acc_win[64:65]

    cidx = lax.broadcasted_iota(I32, (ncp, tq), 0)
    maskc = rep(jnp.where((cidx * CMP_STRIDE + (CMP_BLOCK - 1) <= tpos) & (cidx < n_c), 1.0, 0.0)) > 0.5
    s = _dot(kc_ref[...].astype(BF16), q4t)
    e = jnp.where(maskc, jnp.exp(s - cmax(jnp.where(maskc, s, NEG))), 0.0)
    p = e / jnp.maximum(csum(e), 1e-30)
    o_cmp = _dot(vct_ref[...].astype(BF16), p.astype(BF16))
    psum = p[:, 0:tq]
    for r in range(1, R):
        psum = psum + p[:, r * tq:(r + 1) * tq]

    p_hi = psum.astype(BF16)
    p_lo = (psum - p_hi.astype(F32)).astype(BF16)
    imp = _dot(ovlt_ref[...], p_hi) + _dot(ovlt_ref[...], p_lo)
    cur = lax.shift_right_logical(tpos, 6)
    forced = (jl == 0) | (jl == cur) | (jl == cur - 1)
    st = jnp.where(forced, SEL_FORCE_SCORE, jnp.where(jl <= cur, imp, -SEL_FORCE_SCORE))
    nsp = max(n_sel, SUBLANES)
    sel_rows = []
    for v in range(nsp // SUBLANES):
        gv = st[v * SUBLANES:(v + 1) * SUBLANES, :]
        jrow = v * SUBLANES + lax.broadcasted_iota(I32, (SUBLANES, tq), 0)
        cnt = jnp.zeros((SUBLANES, tq), F32)
        for i in range(n_sel):
            ri = st[i:i + 1, :]
            if v * SUBLANES + SUBLANES - 1 < i:
                beats = ri > gv
            elif v * SUBLANES > i:
                beats = ri >= gv
            else:
                beats = ((ri >= gv) & (jrow > i)) | ((ri > gv) & (jrow < i))
            cnt = cnt + jnp.where(beats, 1.0, 0.0)
        sel_rows.append(jnp.where((cnt < float(k_sel)) & (jrow < n_sel), 1.0, 0.0))
    sel_rows.append(jnp.zeros((LANES - nsp, tq), F32))
    sel = jnp.concatenate(sel_rows, axis=0)

    tk = min(SLC_TILE, ks_ref.shape[0])
    nsub = tk // LANES
    picked = sel > 0.5
    before = jl < lax.shift_right_logical(t0, 6)
    qext_diag = jnp.concatenate([rep(jnp.where(picked, 0.0, NEG)).astype(BF16), q4t], axis=0)
    qext = jnp.concatenate([rep(jnp.where(before & picked, 0.0, NEG)).astype(BF16), q4t], axis=0)
    r_i = lax.broadcasted_iota(I32, (tq, tq), 0)
    c_i = lax.broadcasted_iota(I32, (tq, tq), 1)
    s = _dot(ks_ref[pl.ds(pl.multiple_of(t0, tq), tq), :], qext_diag) + rep(jnp.where(r_i <= c_i, 0.0, NEG))
    nd = tq // LANES
    vd = jnp.concatenate([vst_ref[qi * nd + c] for c in range(nd)], axis=1)
    m_diag, acc_diag = online((jnp.full((1, R * tq), NEG, F32), jnp.zeros((V_ROWS, R * tq), F32)), s, vd)
    n_tiles = ks_ref.shape[0] // tk

    def scores(j):
        jj = jnp.minimum(j, n_tiles - 1)
        return _dot(ks_ref[pl.ds(pl.multiple_of(jj * tk, tk), tk), :], qext)

    def slc_body(j, carry):
        vt = jnp.concatenate([vst_ref[j * nsub + c] for c in range(nsub)], axis=1)
        return online(carry, scores(j), vt)

    n_main = lax.shift_right_logical(t0 + (tk - 1), int(math.log2(tk)))
    _, acc_slc = lax.fori_loop(0, n_main, slc_body, (m_diag, acc_diag))
    o_slc = acc_slc[0:64] / acc_slc[64:65]

    outs = []
    for r in range(R):
        cols = slice(r * tq, (r + 1) * tq)
        outs.append(gts[r][0] * o_cmp[:, cols] + gts[r][1] * o_slc[:, cols] + gts[r][2] * o_win[:, cols])
    for c in range(R // 2):
        o_ref[:, c * LANES:(c + 1) * LANES] = jnp.concatenate(outs[2 * c:2 * c + 2], axis=0).T.astype(BF16)


def _nsa_attn(q_rot, kvc, kvct, kse, vst, kw, vwt, sg, ovlt, B, S):
    G = NSA_KV_HEADS
    nq = S // Q_TILE
    ncp = S // CMP_STRIDE
    n_c = (S - CMP_BLOCK) // CMP_STRIDE + 1
    n_sel = S // SEL_BLOCK
    k_sel = min(N_SELECT, n_sel)
    keys = pl.BlockSpec((None, None, S, 64), lambda b, g, i: (b, g, 0, 0))
    keys_ext = pl.BlockSpec((None, None, S, KE_W), lambda b, g, i: (b, g, 0, 0))
    vals = pl.BlockSpec((None, None, S // LANES, V_ROWS, LANES), lambda b, g, i: (b, g, 0, 0, 0))
    return pl.pallas_call(
        functools.partial(_nsa_attn_kernel, ncp=ncp, n_c=n_c, n_sel=n_sel, k_sel=k_sel),
        out_shape=jax.ShapeDtypeStruct((B, S, NSA_D), BF16),
        grid=(B, G, nq),
        in_specs=[
            pl.BlockSpec((None, Q_TILE, KV_D), lambda b, g, i: (b, i, g)),
            pl.BlockSpec((None, None, None, ncp, 64), lambda b, g, i: (b, 0, g, 0, 0)),
            pl.BlockSpec((None, None, None, 64, ncp), lambda b, g, i: (b, 1, g, 0, 0)),
            keys_ext, vals, keys, vals,
            pl.BlockSpec((None, Q_TILE, LANES), lambda b, g, i: (b, i, 0)),
            pl.BlockSpec(ovlt.shape, lambda b, g, i: (0, 0)),
        ],
        out_specs=pl.BlockSpec((None, Q_TILE, KV_D), lambda b, g, i: (b, i, g)),
        compiler_params=_cparams(("parallel", "parallel", "arbitrary")),
        name="nsa_attn",
    )(q_rot, kvc, kvct, kse, vst, kw, vwt, sg, ovlt)


def _out_proj_kernel(ys_ref, yn_ref, x_ref, w_ref, nw_ref, wr_ref, br_ref, x1_ref, h2_ref, lg_ref):
    half = ys_ref.shape[1]
    y = _dot(ys_ref[...], w_ref[0:half, :]) + _dot(yn_ref[...], w_ref[half:2 * half, :])
    x1 = x_ref[...] + y
    x1_ref[...] = x1
    hn = _rms_rows(x1, nw_ref[...])
    h2_ref[...] = hn
    h1, h2, _ = _split3(hn)
    w1, w2, _ = _split3(wr_ref[...])
    lg_ref[...] = _dot(h1, w1) + _dot(h1, w2) + _dot(h2, w1) + br_ref[...]


def _out_proj(y_ssd, y_nsa, x2d, w_out_b, nw, wr, br):
    T, D = x2d.shape
    tm = min(512, T)
    half = y_ssd.shape[1]
    return pl.pallas_call(
        _out_proj_kernel,
        out_shape=(
            jax.ShapeDtypeStruct((T, D), F32),
            jax.ShapeDtypeStruct((T, D), F32),
            jax.ShapeDtypeStruct((T, LANES), F32),
        ),
        grid=(T // tm,),
        in_specs=[
            pl.BlockSpec((tm, half), lambda i: (i, 0)),
            pl.BlockSpec((tm, half), lambda i: (i, 0)),
            pl.BlockSpec((tm, D), lambda i: (i, 0)),
            pl.BlockSpec((D, D), lambda i: (0, 0)),
            pl.BlockSpec((1, D), lambda i: (0, 0)),
            pl.BlockSpec((D, LANES), lambda i: (0, 0)),
            pl.BlockSpec((1, LANES), lambda i: (0, 0)),
        ],
        out_specs=(
            pl.BlockSpec((tm, D), lambda i: (i, 0)),
            pl.BlockSpec((tm, D), lambda i: (i, 0)),
            pl.BlockSpec((tm, LANES), lambda i: (i, 0)),
        ),
        compiler_params=_cparams(("parallel",)),
        name="out_proj",
    )(y_ssd, y_nsa, x2d, w_out_b, nw, wr, br)


def _route_kernel(lg_ref, idx_ref, rank_ref, gate_ref, cnt_ref, carry, *, tr):
    i = pl.program_id(0)

    @pl.when(i == 0)
    def _():
        carry[...] = jnp.zeros(carry.shape, F32)

    lane = lax.broadcasted_iota(I32, (tr, LANES), 1)
    lanef = lane.astype(F32)
    l = jnp.where(lane < N_EXPERTS, lg_ref[...], NEG)
    ohs, vals, idxs = [], [], []
    for _ in range(TOP_K):
        m = jnp.max(l, axis=-1, keepdims=True)
        idx = jnp.min(jnp.where(l == m, lanef, float(LANES)), axis=-1, keepdims=True)
        oh = lanef == idx
        l = jnp.where(oh, 2.0 * NEG, l)
        ohs.append(oh)
        vals.append(m)
        idxs.append(idx)
    es = [jnp.exp(v - vals[0]) for v in vals]
    den = es[0] + es[1] + es[2] + es[3]
    oh_sum = jnp.zeros((tr, LANES), F32)
    for oh in ohs:
        oh_sum = oh_sum + jnp.where(oh, 1.0, 0.0)
    r_i = lax.broadcasted_iota(I32, (tr, tr), 0)
    c_i = lax.broadcasted_iota(I32, (tr, tr), 1)
    strict = jnp.where(r_i > c_i, 1.0, 0.0).astype(BF16)
    base = carry[0:1, :] + _dot(strict, oh_sum.astype(BF16))
    idx_o = jnp.zeros((tr, LANES), F32)
    rank_o = jnp.zeros((tr, LANES), F32)
    gate_o = jnp.zeros((tr, LANES), F32)
    for k in range(TOP_K):
        rk = jnp.sum(jnp.where(ohs[k], base, 0.0), axis=-1, keepdims=True)
        idx_o = jnp.where(lane == k, idxs[k], idx_o)
        rank_o = jnp.where(lane == k, rk, rank_o)
        gate_o = jnp.where(lane == k, es[k] / den, gate_o)
    idx_ref[...] = idx_o.astype(I32)
    rank_ref[...] = rank_o.astype(I32)
    gate_ref[...] = gate_o
    carry[...] = carry[...] + jnp.sum(oh_sum, axis=0, keepdims=True)
    cnt_ref[...] = carry[...]


def _route(logits):
    T = logits.shape[0]
    tr = min(512, T)
    tok = pl.BlockSpec((tr, LANES), lambda i: (i, 0))
    return pl.pallas_call(
        functools.partial(_route_kernel, tr=tr),
        out_shape=(
            jax.ShapeDtypeStruct((T, LANES), I32),
            jax.ShapeDtypeStruct((T, LANES), I32),
            jax.ShapeDtypeStruct((T, LANES), F32),
            jax.ShapeDtypeStruct((SUBLANES, LANES), F32),
        ),
        grid=(T // tr,),
        in_specs=[tok],
        out_specs=(tok, tok, tok, pl.BlockSpec((SUBLANES, LANES), lambda i: (0, 0))),
        scratch_shapes=[pltpu.VMEM((SUBLANES, LANES), F32)],
        compiler_params=_cparams(("arbitrary",)),
        name="route",
    )(logits)


def _route_fin_kernel(cnt_ref, idx_ref, rank_ref, dest_ref, be_ref, pads_ref, *, tr, nbp, tm_shift):
    lane = lax.broadcasted_iota(I32, (SUBLANES, LANES), 1)
    cnt = cnt_ref[...].astype(I32)
    tm = 1 << tm_shift
    nblk = jnp.where(lane < N_EXPERTS, lax.shift_right_logical(cnt + (tm - 1), tm_shift), 0)
    r_i = lax.broadcasted_iota(I32, (LANES, LANES), 0)
    c_i = lax.broadcasted_iota(I32, (LANES, LANES), 1)
    upper = jnp.where(r_i <= c_i, 1.0, 0.0).astype(BF16)
    end_blk = _dot(nblk.astype(F32).astype(BF16), upper)
    start_row = (end_blk - nblk.astype(F32)) * float(tm)
    idx = idx_ref[...].astype(F32)
    lane_t = lax.broadcasted_iota(I32, (tr, LANES), 1)
    lane_f = lane_t.astype(F32)
    dest = jnp.zeros((tr, LANES), F32)
    for k in range(TOP_K):
        e_k = jnp.sum(jnp.where(lane_t == k, idx, 0.0), axis=-1, keepdims=True)
        s_k = jnp.sum(jnp.where(lane_f == e_k, start_row[0:1, :], 0.0), axis=-1, keepdims=True)
        dest = jnp.where(lane_t == k, s_k, dest)
    dest_ref[...] = dest.astype(I32) + jnp.where(lane_t < TOP_K, rank_ref[...], 0)
    blk = lax.broadcasted_iota(I32, (nbp, LANES), 0).astype(F32)
    lane_b = lax.broadcasted_iota(I32, (nbp, LANES), 1)
    lane_bf = lane_b.astype(F32)
    rsum = lambda v: jnp.sum(v, axis=-1, keepdims=True)
    end_row = end_blk[0:1, :]
    nblk_row = nblk.astype(F32)[0:1, :]
    is_exp = lane_b < N_EXPERTS
    nonempty = is_exp & (nblk_row > 0.0)
    be = jnp.minimum(rsum(jnp.where(is_exp & (end_row <= blk), 1.0, 0.0)), float(N_EXPERTS - 1))
    n_used = rsum(jnp.where(lane_b == N_EXPERTS - 1, end_row, 0.0))
    start_of = rsum(jnp.where(lane_bf == be, end_row - nblk_row, 0.0))
    first = jnp.where((start_of == blk[:, 0:1]) & (blk[:, 0:1] < n_used), 1.0, 0.0)
    run = rsum(jnp.where(nonempty & (lane_bf < be), 1.0, 0.0))
    none = float(LANES)
    nxt = jnp.min(jnp.where(nonempty & (lane_bf > be), lane_bf, none), axis=-1, keepdims=True)
    first_e = jnp.min(jnp.where(nonempty, lane_bf, none), axis=-1, keepdims=True)
    last = jnp.where(nxt == none, 1.0, 0.0)
    nxt = jnp.where(nxt == none, first_e, nxt)
    n_runs = rsum(jnp.where(nonempty, 1.0, 0.0))
    tab = jnp.zeros((nbp, LANES), F32)
    for c, v in enumerate((be, n_used, first, run, nxt, last, n_runs)):
        tab = jnp.where(lane_b == c, v, tab)
    be_ref[...] = tab.astype(I32)
    sub = lax.broadcasted_iota(I32, (SUBLANES, LANES), 0)
    pad_start = start_row.astype(I32) + cnt
    pad_len = jnp.where(lane < N_EXPERTS, nblk * tm - cnt, 0)
    pads_ref[...] = jnp.where(sub == 0, pad_start, jnp.where(sub == 1, pad_len, 0))


def _route_fin(cnt, idx, rank, n_blocks):
    T = idx.shape[0]
    tr = min(512, T)
    nbp = ((n_blocks + SUBLANES - 1) // SUBLANES) * SUBLANES
    tok = pl.BlockSpec((tr, LANES), lambda i: (i, 0))
    return pl.pallas_call(
        functools.partial(_route_fin_kernel, tr=tr, nbp=nbp, tm_shift=int(math.log2(MOE_TM))),
        out_shape=(
            jax.ShapeDtypeStruct((T, LANES), I32),
            jax.ShapeDtypeStruct((nbp, LANES), I32),
            jax.ShapeDtypeStruct((SUBLANES, LANES), I32),
        ),
        grid=(T // tr,),
        in_specs=[pl.BlockSpec((SUBLANES, LANES), lambda i: (0, 0)), tok, tok],
        out_specs=(tok, pl.BlockSpec((nbp, LANES), lambda i: (0, 0)),
                   pl.BlockSpec((SUBLANES, LANES), lambda i: (0, 0))),
        compiler_params=_cparams(("arbitrary",)),
        name="route_fin",
    )(cnt, idx, rank)


def _row_copy(src_ref, src_row, dst_ref, dst_row, sem):
    return pltpu.make_async_copy(src_ref.at[pl.ds(src_row, 1)], dst_ref.at[pl.ds(dst_row, 1)], sem)


def _dispatch_kernel(dest_ref, pstart_ref, plen_ref, nu_ref, h2_ref, xs_out, zeros, sem, *, tm, nb):
    base = pl.program_id(0) * (tm * TOP_K)

    @pl.when(pl.program_id(0) == 0)
    def _():
        zeros[...] = jnp.zeros(zeros.shape, F32)

        def pad_rows(start):
            def per_expert(e, c):
                def one(r, c2):
                    cp = _row_copy(zeros, 0, xs_out, pstart_ref[e] + r, sem)
                    cp.start() if start else cp.wait()
                    return c2
                lax.fori_loop(0, plen_ref[e], one, 0)
                return c
            lax.fori_loop(0, N_EXPERTS, per_expert, 0)

            def tail(t, c):
                row0 = pl.multiple_of((nu_ref[0] + t) * MOE_TM, MOE_TM)
                cp = pltpu.make_async_copy(zeros, xs_out.at[pl.ds(row0, MOE_TM)], sem)
                cp.start() if start else cp.wait()
                return c
            lax.fori_loop(0, nb - nu_ref[0], tail, 0)

        pad_rows(True)
        pad_rows(False)

    def issue(r, c):
        for k in range(TOP_K):
            _row_copy(h2_ref, r, xs_out, dest_ref[base + r * TOP_K + k], sem).start(priority=k % 2)
        return c

    lax.fori_loop(0, tm, issue, 0, unroll=4)

    def drain(r, c):
        for k in range(TOP_K):
            _row_copy(h2_ref, r, xs_out, dest_ref[base + r * TOP_K + k], sem).wait()
        return c

    lax.fori_loop(0, tm, drain, 0, unroll=4)


def _dispatch(dest_flat, pads, nu, h2, n_pad):
    T, W = h2.shape
    tm = min(256, T)
    return pl.pallas_call(
        functools.partial(_dispatch_kernel, tm=tm, nb=n_pad // MOE_TM),
        out_shape=jax.ShapeDtypeStruct((n_pad, W), F32),
        grid_spec=pltpu.PrefetchScalarGridSpec(
            num_scalar_prefetch=4,
            grid=(T // tm,),
            in_specs=[pl.BlockSpec((tm, W), lambda i, *_: (i, 0))],
            out_specs=pl.BlockSpec(memory_space=pl.ANY),
            scratch_shapes=[pltpu.VMEM((MOE_TM, W), F32), pltpu.SemaphoreType.DMA(())],
        ),
        compiler_params=_cparams(("arbitrary",)),
        name="dispatch",
    )(dest_flat, pads[0, :N_EXPERTS], pads[1, :N_EXPERTS], nu, h2)


def _cast_rows(src, dst, rows, chunk=256):
    def body(c, carry):
        r = pl.multiple_of(c * chunk, chunk)
        dst[pl.ds(r, chunk), :] = src[pl.ds(r, chunk), :].astype(BF16)
        return carry
    lax.fori_loop(0, rows // chunk, body, 0)


def _gate_up_kernel(be_ref, nu_ref, first_ref, run_ref, nxt_ref, last_ref, nr_ref,
                    x_ref, bg_ref, bu_ref, w_hbm, o_ref, wbuf, wgb, wub, sem, *, tf, nf):
    f = pl.program_id(0)
    i = pl.program_id(1)
    used = i < nu_ref[0]
    D = wgb.shape[0]

    def w_copy(e, ff, slot, part):
        col = pl.multiple_of(part * D_FF + ff * tf, tf)
        return pltpu.make_async_copy(w_hbm.at[e, :, pl.ds(col, tf)], wbuf.at[slot, part], sem.at[slot])

    @pl.when(used & (first_ref[i] == 1))
    def _():
        run = f * nr_ref[0] + run_ref[i]
        slot = run & 1
        e = be_ref[i]

        @pl.when(run == 0)
        def _():
            w_copy(e, f, slot, 0).start()
            w_copy(e, f, slot, 1).start()

        w_copy(e, f, slot, 0).wait()
        w_copy(e, f, slot, 1).wait()
        last = last_ref[i]

        @pl.when((last == 0) | (f < nf - 1))
        def _():
            w_copy(nxt_ref[i], f + last, 1 - slot, 0).start()
            w_copy(nxt_ref[i], f + last, 1 - slot, 1).start()

        _cast_rows(wbuf.at[slot, 0], wgb, D)
        _cast_rows(wbuf.at[slot, 1], wub, D)

    @pl.when(used)
    def _():
        x = x_ref[...].astype(BF16)
        gate = jnp.minimum(_dot(x, wgb[...]) + bg_ref[...], SWIGLU_LIMIT)
        up = jnp.clip(_dot(x, wub[...]) + bu_ref[...], -SWIGLU_LIMIT, SWIGLU_LIMIT)
        glu = gate * _sigmoid(SWIGLU_ALPHA * gate)
        o_ref[...] = ((up + 1.0) * glu).astype(BF16)

    @pl.when(jnp.logical_not(used))
    def _():
        o_ref[...] = jnp.zeros(o_ref.shape, BF16)


def _gate_up(sched, xs, w_gate_up, b_gate_up3):
    n_pad, D = xs.shape
    tm = MOE_TM
    tf = 1024
    nf = D_FF // tf
    nb = n_pad // tm
    eff = lambda i, nu: jnp.minimum(i, nu[0] - 1)
    return pl.pallas_call(
        functools.partial(_gate_up_kernel, tf=tf, nf=nf),
        out_shape=jax.ShapeDtypeStruct((n_pad, D_FF), BF16),
        grid_spec=pltpu.PrefetchScalarGridSpec(
            num_scalar_prefetch=len(sched),
            grid=(nf, nb),
            in_specs=[
                pl.BlockSpec((tm, D), lambda f, i, be, nu, *_: (eff(i, nu), 0)),
                pl.BlockSpec((None, 1, tf), lambda f, i, be, nu, *_: (be[eff(i, nu)], 0, f)),
                pl.BlockSpec((None, 1, tf), lambda f, i, be, nu, *_: (be[eff(i, nu)], 0, nf + f)),
                pl.BlockSpec(memory_space=pl.ANY),
            ],
            out_specs=pl.BlockSpec((tm, tf), lambda f, i, *_: (i, f)),
            scratch_shapes=[
                pltpu.VMEM((2, 2, D, tf), F32),
                pltpu.VMEM((D, tf), BF16),
                pltpu.VMEM((D, tf), BF16),
                pltpu.SemaphoreType.DMA((2,)),
            ],
        ),
        compiler_params=_cparams(("arbitrary", "arbitrary")),
        name="gate_up",
    )(*sched, xs, b_gate_up3, b_gate_up3, w_gate_up)


def _down_kernel(be_ref, nu_ref, first_ref, run_ref, nxt_ref, last_ref, nr_ref,
                 h_ref, b_ref, w_hbm, o_ref, wbuf, wb, sem):
    i = pl.program_id(0)
    used = i < nu_ref[0]

    def w_copy(e, slot):
        return pltpu.make_async_copy(w_hbm.at[e], wbuf.at[slot], sem.at[slot])

    @pl.when(used & (first_ref[i] == 1))
    def _():
        run = run_ref[i]
        slot = run & 1
        e = be_ref[i]

        @pl.when(run == 0)
        def _():
            w_copy(e, slot).start()

        w_copy(e, slot).wait()

        @pl.when(last_ref[i] == 0)
        def _():
            w_copy(nxt_ref[i], 1 - slot).start()

        _cast_rows(wbuf.at[slot], wb, wb.shape[0])

    @pl.when(used)
    def _():
        o_ref[...] = _dot(h_ref[...], wb[...]) + b_ref[...]

    @pl.when(jnp.logical_not(used))
    def _():
        o_ref[...] = jnp.zeros(o_ref.shape, F32)


def _down(sched, h, w_down, b_down3):
    n_pad, F = h.shape
    D = w_down.shape[2]
    tm = MOE_TM
    nb = n_pad // tm
    eff = lambda i, nu: jnp.minimum(i, nu[0] - 1)
    return pl.pallas_call(
        _down_kernel,
        out_shape=jax.ShapeDtypeStruct((n_pad, D), F32),
        grid_spec=pltpu.PrefetchScalarGridSpec(
            num_scalar_prefetch=len(sched),
            grid=(nb,),
            in_specs=[
                pl.BlockSpec((tm, F), lambda i, be, nu, *_: (eff(i, nu), 0)),
                pl.BlockSpec((None, 1, D), lambda i, be, nu, *_: (be[eff(i, nu)], 0, 0)),
                pl.BlockSpec(memory_space=pl.ANY),
            ],
            out_specs=pl.BlockSpec((tm, D), lambda i, *_: (i, 0)),
            scratch_shapes=[
                pltpu.VMEM((2, F, D), F32),
                pltpu.VMEM((F, D), BF16),
                pltpu.SemaphoreType.DMA((2,)),
            ],
        ),
        compiler_params=_cparams(("arbitrary",)),
        name="down",
    )(*sched, h, b_down3, w_down)


def _combine_kernel(dest_ref, g_ref, x1_ref, fw_ref, y_hbm, o_ref, buf, sem, *, tm):
    base = pl.program_id(0) * (tm * TOP_K)

    def issue(r, c):
        for k in range(TOP_K):
            _row_copy(y_hbm, dest_ref[base + r * TOP_K + k], buf.at[k], r, sem).start(priority=k % 2)
        return c

    lax.fori_loop(0, tm, issue, 0, unroll=4)

    def drain(r, c):
        for k in range(TOP_K):
            _row_copy(y_hbm, dest_ref[base + r * TOP_K + k], buf.at[k], r, sem).wait()
        return c

    lax.fori_loop(0, tm, drain, 0, unroll=4)

    acc = x1_ref[...]
    gts = g_ref[...]
    for k in range(TOP_K):
        acc = acc + gts[:, k:k + 1] * buf[k]
    o_ref[...] = _rms_rows(acc, fw_ref[...])


def _combine(dest_flat, gates, x1, fw, y):
    T, D = x1.shape
    tm = min(256, T)
    return pl.pallas_call(
        functools.partial(_combine_kernel, tm=tm),
        out_shape=jax.ShapeDtypeStruct((T, D), F32),
        grid_spec=pltpu.PrefetchScalarGridSpec(
            num_scalar_prefetch=1,
            grid=(T // tm,),
            in_specs=[
                pl.BlockSpec((tm, LANES), lambda i, d: (i, 0)),
                pl.BlockSpec((tm, D), lambda i, d: (i, 0)),
                pl.BlockSpec((1, D), lambda i, d: (0, 0)),
                pl.BlockSpec(memory_space=pl.ANY),
            ],
            out_specs=pl.BlockSpec((tm, D), lambda i, d: (i, 0)),
            scratch_shapes=[pltpu.VMEM((TOP_K, tm, D), F32), pltpu.SemaphoreType.DMA(())],
        ),
        compiler_params=_cparams(("arbitrary",)),
        name="combine",
    )(dest_flat, gates, x1, fw, y)


def _rope_angles(pos):
    inv = ROPE_THETA ** (-jnp.arange(0, ROPE_DIM, 2, dtype=F32) / ROPE_DIM)
    return pos.astype(F32)[:, None] * inv[None, :]


def _rope_tables128(S):
    ang = _rope_angles(jnp.arange(S, dtype=I32))
    c, s = jnp.cos(ang), jnp.sin(ang)
    one = jnp.ones((S, 64 - ROPE_DIM), F32)
    zero = jnp.zeros((S, 64 - ROPE_DIM), F32)
    z8 = jnp.zeros((S, 8), F32)
    cos64 = jnp.concatenate([c, c, one], axis=1)
    sina64 = jnp.concatenate([z8, s, zero], axis=1)
    sinb64 = jnp.concatenate([-s, z8, zero], axis=1)
    t2 = lambda t: jnp.concatenate([t, t], axis=1)
    return t2(cos64), t2(sina64), t2(sinb64)


def _rope_tables_cmp(ncp):
    pos = jnp.arange(ncp, dtype=I32) * CMP_STRIDE + CMP_BLOCK - 1
    ang = _rope_angles(pos)
    c, s = jnp.cos(ang), jnp.sin(ang)
    cos64 = jnp.concatenate([c, c, jnp.ones((ncp, 64 - ROPE_DIM), F32)], axis=1)
    sin64 = jnp.concatenate([s, s, jnp.zeros((ncp, 64 - ROPE_DIM), F32)], axis=1)
    rot = np.zeros((64, 64), np.float32)
    for d in range(8):
        rot[d + 8, d] = -1.0
        rot[d, d + 8] = 1.0
    return cos64, sin64, jnp.asarray(rot, BF16)


def _overlap_matrix_t(S):
    ncp = S // CMP_STRIDE
    n_c = (S - CMP_BLOCK) // CMP_STRIDE + 1
    ovl = np.zeros((LANES, ncp), np.float32)
    for c in range(n_c):
        for l in range(CMP_BLOCK):
            ovl[(c * CMP_STRIDE + l) // SEL_BLOCK, c] += 1.0
    return jnp.asarray(ovl, BF16)


def _pack_w_in(w_in):
    D = w_in.shape[0]
    widths = [SSM_D_INNER, SSM_CONV_DIM, SSM_HEADS, NSA_D] + [KV_D] * 6 + [3 * NSA_HEADS]
    cuts = np.cumsum([0] + widths)
    seg = lambda k: w_in[:, cuts[k]:cuts[k + 1]].astype(BF16)
    zeros = lambda n: jnp.zeros((D, n), BF16)
    parts = ([seg(3), seg(0), seg(1)] + [seg(k) for k in range(4, 10)]
             + [seg(2), zeros(LANES - widths[2]), seg(10), zeros(LANES - widths[10] + PROJ_W - COL_GATE - LANES)])
    return jnp.concatenate(parts, axis=1)


def _sched_columns(tab, n_blocks):
    per_tile = lambda c: tab[:n_blocks, c]
    once = lambda c: tab[0:1, c]
    return (per_tile(0), once(1), per_tile(2), per_tile(3), per_tile(4), per_tile(5), once(6))


def kernel(x, attn_norm_w, w_in, conv_w, conv_b, dt_bias, a_log, d_skip, ssm_norm_w, cmp_pos_emb, cmp_w1,
           cmp_b1, cmp_w2, cmp_b2, w_out, moe_norm_w, w_router, b_router, w_gate_up, b_gate_up, w_down,
           b_down, final_norm_w):
    B, S, D = x.shape
    T = B * S
    x2d = x.reshape(T, D)
    row = lambda v: v.reshape(1, -1)
    padl = lambda v: jnp.pad(v.reshape(1, -1), ((0, 0), (0, LANES - v.size)))

    proj = _in_proj(x2d, row(attn_norm_w[0]), _pack_w_in(w_in[0]))
    y_ssd = _ssd(proj, B, S, conv_w[0], row(conv_b[0]), padl(dt_bias[0]), padl(a_log[0]),
                 row(jnp.repeat(d_skip[0], SSM_HEAD_DIM)), row(ssm_norm_w[0]))
    cos, sina, sinb = _rope_tables128(S)
    q_rot, kse, vst, kw, vwt, sg = _nsa_prep(proj, B, S, cos, sina, sinb)
    cosc, sinc, rot = _rope_tables_cmp(S // CMP_STRIDE)
    kvc, kvct = _compress(proj, B, S, cmp_pos_emb[0], cmp_w1[0], cmp_b1[0][:, None, :], cmp_w2[0],
                          cmp_b2[0][:, None, :], cosc, sinc, rot)
    y_nsa = _nsa_attn(q_rot, kvc, kvct, kse, vst, kw, vwt, sg, _overlap_matrix_t(S), B, S)

    wr = jnp.pad(w_router[0], ((0, 0), (0, LANES - N_EXPERTS)))
    x1, h2, logits = _out_proj(y_ssd, y_nsa.reshape(T, NSA_D), x2d, w_out[0].astype(BF16),
                                row(moe_norm_w[0]), wr, padl(b_router[0]))

    n_pad = T * TOP_K + N_EXPERTS * MOE_TM
    n_blocks = n_pad // MOE_TM
    idx, rank, gates, cnt = _route(logits)
    dest, be_tab, pads = _route_fin(cnt, idx, rank, n_blocks)
    dest_flat = dest[:, :TOP_K].reshape(-1)
    sched = _sched_columns(be_tab, n_blocks)
    xs = _dispatch(dest_flat, pads, sched[1], h2, n_pad)
    hmid = _gate_up(sched, xs, w_gate_up[0], b_gate_up[0][:, None, :])
    y = _down(sched, hmid, w_down[0], b_down[0][:, None, :])
    out = _combine(dest_flat, gates, x1, row(final_norm_w), y)
    return out.reshape(B, S, D)
```

```python
import functools
import math

import jax
import jax.numpy as jnp
import numpy as np
from jax import lax
from jax.experimental import pallas as pl
from jax.experimental.pallas import tpu as pltpu

F32 = jnp.float32
BF16 = jnp.bfloat16
I32 = jnp.int32
U32 = jnp.uint32

NORM_EPS = 1e-5
SSM_D_INNER = 1024
SSM_HEAD_DIM = 64
SSM_HEADS = 16
SSM_GROUPS = 4
SSM_STATE = 128
SSM_CONV = 4
SSM_CHUNK = 128
SSM_CONV_DIM = 2048
NSA_HEADS = 16
NSA_KV_HEADS = 4
NSA_HEAD_DIM = 64
NSA_D = 1024
KV_D = 256
CMP_BLOCK = 32
CMP_STRIDE = 16
CMP_HIDDEN = 512
SEL_BLOCK = 64
N_SELECT = 16
WINDOW = 512
ROPE_THETA = 500000.0
ROPE_DIM = 16
SEL_FORCE_SCORE = 1.0e4
N_EXPERTS = 32
TOP_K = 4
D_FF = 2048
SWIGLU_LIMIT = 7.0
SWIGLU_ALPHA = 1.702

LANES = 128
SUBLANES = 8
VMEM_LIMIT = 56 * 1024 * 1024

NEG = -1.0e30

COL_Q = 0
COL_Z = 1024
COL_XBC = 2048
COL_KV = 4096
COL_DT = 5632
COL_GATE = 5760
PROJ_W = 6144

MOE_TM = 256
Q_TILE = 256
SLC_TILE = 512
KE_W = LANES + 64
V_ROWS = 80


def _cparams(sem, vmem=VMEM_LIMIT):
    return pltpu.CompilerParams(dimension_semantics=sem, vmem_limit_bytes=vmem)


def _dot(a, b):
    return jnp.dot(a, b, preferred_element_type=F32)


def _dot_nt(a, b):
    return lax.dot_general(a, b, (((1,), (1,)), ((), ())), preferred_element_type=F32)


def _split3(b):
    b1 = b.astype(BF16)
    r1 = b - b1.astype(F32)
    b2 = r1.astype(BF16)
    r2 = r1 - b2.astype(F32)
    return b1, b2, r2.astype(BF16)


def _dot_exact_lhs(a_bf16, b):
    b1, b2, b3 = _split3(b)
    return _dot(a_bf16, b1) + _dot(a_bf16, b2) + _dot(a_bf16, b3)


def _dot_exact_rhs(a, b_bf16):
    a1, a2, a3 = _split3(a)
    return _dot(a1, b_bf16) + _dot(a2, b_bf16) + _dot(a3, b_bf16)


def _sigmoid(x):
    return 1.0 / (1.0 + jnp.exp(-x))


def _silu(x):
    return x * _sigmoid(x)


def _softplus(x):
    return jnp.maximum(x, 0.0) + jnp.log(1.0 + jnp.exp(-jnp.abs(x)))


def _gelu_tanh(x):
    c = math.sqrt(2.0 / math.pi)
    return 0.5 * x * (1.0 + jnp.tanh(c * (x + 0.044715 * (x * x * x))))


def _rms_rows(x, w):
    ms = jnp.mean(x * x, axis=-1, keepdims=True)
    return x * lax.rsqrt(ms + NORM_EPS) * w


def _in_proj_kernel(x_ref, nw_ref, w_ref, o_ref, hn_ref, *, tm):
    @pl.when(pl.program_id(1) == 0)
    def _():
        def body(c, carry):
            r = pl.multiple_of(c * 128, 128)
            hn_ref[pl.ds(r, 128), :] = _rms_rows(x_ref[pl.ds(r, 128), :], nw_ref[...]).astype(BF16)
            return carry
        lax.fori_loop(0, tm // 128, body, 0)

    o_ref[...] = _dot(hn_ref[...], w_ref[...])


def _in_proj(x2d, nw, wp):
    T, D = x2d.shape
    NP = wp.shape[1]
    tm = min(1024, T)
    tn = 1024
    return pl.pallas_call(
        functools.partial(_in_proj_kernel, tm=tm),
        out_shape=jax.ShapeDtypeStruct((T, NP), F32),
        grid=(T // tm, NP // tn),
        in_specs=[
            pl.BlockSpec((tm, D), lambda i, j: (i, 0)),
            pl.BlockSpec((1, D), lambda i, j: (0, 0)),
            pl.BlockSpec((D, tn), lambda i, j: (0, j)),
        ],
        out_specs=pl.BlockSpec((tm, tn), lambda i, j: (i, j)),
        scratch_shapes=[pltpu.VMEM((tm, D), BF16)],
        compiler_params=_cparams(("parallel", "arbitrary")),
        name="in_proj",
    )(x2d, nw, wp)


def _rope128(x, cos, sina, sinb):
    return x * cos + pltpu.roll(x, 8, 1) * sina + pltpu.roll(x, 120, 1) * sinb


def _nsa_prep_kernel(q_ref, ks_ref, kw_ref, gl_ref, cos_ref, sina_ref, sinb_ref,
                     qo_ref, kso_ref, vso_ref, kwo_ref, vwo_ref, sg_ref):
    cos = cos_ref[...]
    sina = sina_ref[...]
    sinb = sinb_ref[...]
    scale = NSA_HEAD_DIM ** -0.5
    for c in range(NSA_D // LANES):
        xq = q_ref[:, c * LANES:(c + 1) * LANES]
        qo_ref[:, c * LANES:(c + 1) * LANES] = (_rope128(xq, cos, sina, sinb) * scale).astype(BF16)
    ts = q_ref.shape[0]
    key = pl.program_id(1) * ts + lax.broadcasted_iota(I32, (ts, LANES), 0)
    blk1h = jnp.where(lax.shift_right_logical(key, 6) == lax.broadcasted_iota(I32, (ts, LANES), 1), 1.0, 0.0)
    ones = jnp.ones((V_ROWS - 64, LANES), BF16)
    for src, ko, vo, ext in ((ks_ref, kso_ref, vso_ref, True), (kw_ref, kwo_ref, vwo_ref, False)):
        for c in range(KV_D // LANES):
            kr = _rope128(src[:, c * LANES:(c + 1) * LANES], cos, sina, sinb)
            vt = src[:, KV_D + c * LANES:KV_D + (c + 1) * LANES].T
            for half in range(2):
                g = 2 * c + half
                kg = kr[:, half * 64:(half + 1) * 64].astype(BF16)
                if ext:
                    ko[g, :, 0:LANES] = blk1h.astype(BF16)
                    ko[g, :, LANES:LANES + 64] = kg
                else:
                    ko[g] = kg
                for j in range(ts // LANES):
                    vo[g, j, 0:64, :] = vt[half * 64:(half + 1) * 64, j * LANES:(j + 1) * LANES].astype(BF16)
                    vo[g, j, 64:V_ROWS, :] = ones
    sg_ref[...] = _sigmoid(gl_ref[...])


def _nsa_prep(proj, B, S, cos, sina, sinb):
    ts = min(512, S)
    nst = S // ts
    G = NSA_KV_HEADS
    kv_spec = pl.BlockSpec((None, G, ts, 64), lambda b, s: (b, 0, s, 0))
    kv_shape = jax.ShapeDtypeStruct((B, G, S, 64), BF16)
    ke_spec = pl.BlockSpec((None, G, ts, KE_W), lambda b, s: (b, 0, s, 0))
    ke_shape = jax.ShapeDtypeStruct((B, G, S, KE_W), BF16)
    vt_spec = pl.BlockSpec((None, G, ts // LANES, V_ROWS, LANES), lambda b, s: (b, 0, s, 0, 0))
    vt_shape = jax.ShapeDtypeStruct((B, G, S // LANES, V_ROWS, LANES), BF16)
    return pl.pallas_call(
        _nsa_prep_kernel,
        out_shape=(
            jax.ShapeDtypeStruct((B, S, NSA_D), BF16),
            ke_shape, vt_shape, kv_shape, vt_shape,
            jax.ShapeDtypeStruct((B, S, LANES), F32),
        ),
        grid=(B, nst),
        in_specs=[
            pl.BlockSpec((ts, NSA_D), lambda b, s: (b * nst + s, COL_Q // NSA_D)),
            pl.BlockSpec((ts, 512), lambda b, s: (b * nst + s, (COL_KV + 512) // 512)),
            pl.BlockSpec((ts, 512), lambda b, s: (b * nst + s, (COL_KV + 1024) // 512)),
            pl.BlockSpec((ts, LANES), lambda b, s: (b * nst + s, COL_GATE // LANES)),
            pl.BlockSpec((ts, LANES), lambda b, s: (s, 0)),
            pl.BlockSpec((ts, LANES), lambda b, s: (s, 0)),
            pl.BlockSpec((ts, LANES), lambda b, s: (s, 0)),
        ],
        out_specs=(
            pl.BlockSpec((None, ts, NSA_D), lambda b, s: (b, s, 0)),
            ke_spec, vt_spec, kv_spec, vt_spec,
            pl.BlockSpec((None, ts, LANES), lambda b, s: (b, s, 0)),
        ),
        compiler_params=_cparams(("parallel", "parallel")),
        name="nsa_prep",
    )(proj, proj, proj, proj, cos, sina, sinb)


def _ssd_kernel(z_ref, xbc_ref, dt_ref, cw_ref, cb_ref, dtb_ref, alog_ref, dsk_ref, nw_ref,
                y_ref, buf, st):
    L = SSM_CHUNK
    c = pl.program_id(1)

    @pl.when(c == 0)
    def _():
        buf[0:8, :] = jnp.zeros((8, SSM_CONV_DIM), F32)
        st[...] = jnp.zeros(st.shape, F32)

    buf[8:8 + L, :] = xbc_ref[...]
    acc = jnp.broadcast_to(cb_ref[...], (L, SSM_CONV_DIM))
    for k in range(SSM_CONV):
        acc = acc + cw_ref[k:k + 1, :] * buf[5 + k:5 + k + L, :]
    xc = _silu(acc)
    buf[0:8, :] = xbc_ref[L - 8:L, :]

    lane = lax.broadcasted_iota(I32, (L, LANES), 1)
    row = lax.broadcasted_iota(I32, (L, LANES), 0)
    lo = lane < 64
    dtv = jnp.where(lane < SSM_HEADS, _softplus(dt_ref[...] + dtb_ref[...]), 0.0)
    a = -jnp.exp(alog_ref[...])
    tri = jnp.where(row >= lane, 1.0, 0.0).astype(BF16)
    acs = _dot_exact_lhs(tri, dtv * a)
    acs_t = acs.T
    causal = row >= lane

    ys = []
    for g in range(SSM_GROUPS):
        bg = xc[:, SSM_D_INNER + g * SSM_STATE:SSM_D_INNER + (g + 1) * SSM_STATE]
        cg = xc[:, SSM_D_INNER + (SSM_GROUPS + g) * SSM_STATE:SSM_D_INNER + (SSM_GROUPS + g + 1) * SSM_STATE]
        bgt = bg.T.astype(BF16)
        cgb = cg.astype(BF16)
        gmat = _dot(cgb, bgt)
        for p in (2 * g, 2 * g + 1):
            h0, h1 = 2 * p, 2 * p + 1
            xs_pair = xc[:, p * LANES:(p + 1) * LANES]
            col0 = acs[:, h0:h0 + 1]
            col1 = acs[:, h1:h1 + 1]
            colp = jnp.where(lo, col0, col1)
            last = jnp.where(lo[0:1, :], acs[L - 1:L, h0:h0 + 1], acs[L - 1:L, h1:h1 + 1])
            x = xs_pair * jnp.where(lo, dtv[:, h0:h0 + 1], dtv[:, h1:h1 + 1])
            m0 = (gmat * jnp.where(causal, jnp.exp(col0 - acs_t[h0:h0 + 1, :]), 0.0)).astype(BF16)
            m1 = (gmat * jnp.where(causal, jnp.exp(col1 - acs_t[h1:h1 + 1, :]), 0.0)).astype(BF16)
            y_diag = _dot(m0, jnp.where(lo, x, 0.0).astype(BF16)) + _dot(m1, jnp.where(lo, 0.0, x).astype(BF16))
            s_prev = st[p]
            y_off = _dot(cgb, s_prev.astype(BF16)) * jnp.exp(colp)
            w = (x * jnp.exp(last - colp)).astype(BF16)
            st[p] = jnp.exp(last) * s_prev + _dot(bgt, w)
            ys.append(y_diag + y_off + dsk_ref[:, p * LANES:(p + 1) * LANES] * xs_pair)
    y = jnp.concatenate(ys, axis=1)
    gte = y * _silu(z_ref[...])
    gw = SSM_D_INNER // SSM_GROUPS
    outs = []
    for k in range(SSM_GROUPS):
        gk = gte[:, k * gw:(k + 1) * gw]
        ms = jnp.mean(gk * gk, axis=-1, keepdims=True)
        outs.append(gk * lax.rsqrt(ms + NORM_EPS))
    y_ref[...] = (jnp.concatenate(outs, axis=1) * nw_ref[...]).astype(BF16)


def _ssd(proj, B, S, conv_w, conv_b, dtb, alog, dskip, nw):
    L = SSM_CHUNK
    nc = S // L
    small = lambda shp: pl.BlockSpec(shp, lambda b, c: (0, 0))
    return pl.pallas_call(
        _ssd_kernel,
        out_shape=jax.ShapeDtypeStruct((B * S, SSM_D_INNER), BF16),
        grid=(B, nc),
        in_specs=[
            pl.BlockSpec((L, SSM_D_INNER), lambda b, c: (b * nc + c, COL_Z // SSM_D_INNER)),
            pl.BlockSpec((L, SSM_CONV_DIM), lambda b, c: (b * nc + c, COL_XBC // SSM_CONV_DIM)),
            pl.BlockSpec((L, LANES), lambda b, c: (b * nc + c, COL_DT // LANES)),
            small((SSM_CONV, SSM_CONV_DIM)),
            small((1, SSM_CONV_DIM)),
            small((1, LANES)),
            small((1, LANES)),
            small((1, SSM_D_INNER)),
            small((1, SSM_D_INNER)),
        ],
        out_specs=pl.BlockSpec((L, SSM_D_INNER), lambda b, c: (b * nc + c, 0)),
        scratch_shapes=[
            pltpu.VMEM((L + 8, SSM_CONV_DIM), F32),
            pltpu.VMEM((SSM_HEADS // 2, SSM_STATE, LANES), F32),
        ],
        compiler_params=_cparams(("parallel", "arbitrary")),
        name="ssd",
    )(proj, proj, proj, conv_w, conv_b, dtb, alog, dskip, nw)


def _compress_kernel(xa_ref, xb_ref, pos_ref, w1_ref, b1_ref, w2_ref, b2_ref, cos_ref, sin_ref, rot_ref,
                     o_ref, ot_ref, w1b, feat, *, ncp):
    kv = pl.program_id(1)
    outs = []
    half = CMP_STRIDE
    hw = half * 64
    w1b[...] = w1_ref[...].astype(BF16)
    w2 = w2_ref[...].astype(BF16)
    G = NSA_KV_HEADS
    is_k = kv == 0
    for g in range(G):
        x_ref = (xa_ref, xb_ref)[g // 2]
        for l in range(half):
            xg = x_ref[pl.ds(l, ncp, stride=half), :][:, (g % 2) * 64:(g % 2 + 1) * 64]
            feat[:, l * 64:(l + 1) * 64] = xg + pos_ref[l:l + 1, :]
            feat[:, hw + l * 64:hw + (l + 1) * 64] = xg + pos_ref[half + l:half + l + 1, :]
        first = _dot(feat[:, 0:hw].astype(BF16), w1b[0:hw, :])
        second = _dot(feat[:, hw:2 * hw].astype(BF16), w1b[hw:2 * hw, :])
        h = first + pltpu.roll(second, ncp - 1, 0) + b1_ref[...]
        h = _gelu_tanh(h)
        o = _dot(h.astype(BF16), w2) + b2_ref[...]
        roped = o * cos_ref[...] + _dot_exact_rhs(o, rot_ref[...]) * sin_ref[...]
        outs.append(jnp.where(is_k, roped, o))
        o_ref[g] = outs[g]
    for c in range(G // 2):
        t = jnp.concatenate([outs[2 * c], outs[2 * c + 1]], axis=1).T
        ot_ref[2 * c] = t[0:64, :]
        ot_ref[2 * c + 1] = t[64:128, :]


def _compress(proj, B, S, pos, w1, b1, w2, b2, cosc, sinc, rot):
    ncp = S // CMP_STRIDE
    G = NSA_KV_HEADS
    kvblk = COL_KV // KV_D
    return pl.pallas_call(
        functools.partial(_compress_kernel, ncp=ncp),
        out_shape=(jax.ShapeDtypeStruct((B, 2, G, ncp, 64), F32),
                   jax.ShapeDtypeStruct((B, 2, G, 64, ncp), F32)),
        grid=(B, 2),
        in_specs=[
            pl.BlockSpec((S, LANES), lambda b, k: (b, 2 * (kvblk + k))),
            pl.BlockSpec((S, LANES), lambda b, k: (b, 2 * (kvblk + k) + 1)),
            pl.BlockSpec((None, CMP_BLOCK, 64), lambda b, k: (k, 0, 0)),
            pl.BlockSpec((None, CMP_BLOCK * 64, CMP_HIDDEN), lambda b, k: (k, 0, 0)),
            pl.BlockSpec((None, 1, CMP_HIDDEN), lambda b, k: (k, 0, 0)),
            pl.BlockSpec((None, CMP_HIDDEN, 64), lambda b, k: (k, 0, 0)),
            pl.BlockSpec((None, 1, 64), lambda b, k: (k, 0, 0)),
            pl.BlockSpec((ncp, 64), lambda b, k: (0, 0)),
            pl.BlockSpec((ncp, 64), lambda b, k: (0, 0)),
            pl.BlockSpec((64, 64), lambda b, k: (0, 0)),
        ],
        out_specs=(pl.BlockSpec((None, None, G, ncp, 64), lambda b, k: (b, k, 0, 0, 0)),
                   pl.BlockSpec((None, None, G, 64, ncp), lambda b, k: (b, k, 0, 0, 0))),
        scratch_shapes=[pltpu.VMEM((CMP_BLOCK * 64, CMP_HIDDEN), BF16),
                        pltpu.VMEM((ncp, CMP_BLOCK * 64), F32)],
        compiler_params=_cparams(("parallel", "parallel")),
        name="compress",
    )(proj, proj, pos, w1, b1, w2, b2, cosc, sinc, rot)


def _nsa_attn_kernel(q_ref, kc_ref, vct_ref, ks_ref, vst_ref, kw_ref, vwt_ref, sg_ref, ovlt_ref,
                     o_ref, *, ncp, n_c, n_sel, k_sel):
    g = pl.program_id(1)
    qi = pl.program_id(2)
    tq = Q_TILE
    R = NSA_HEADS // NSA_KV_HEADS
    t0 = qi * tq
    qt = q_ref[...].astype(F32).T
    q4t = jnp.concatenate([qt[r * 64:(r + 1) * 64, :] for r in range(R)], axis=1).astype(BF16)
    rep = lambda a: jnp.concatenate([a] * R, axis=1)
    tpos = t0 + lax.broadcasted_iota(I32, (1, tq), 1)
    cmax = lambda a: jnp.max(a, axis=0, keepdims=True)
    csum = lambda a: jnp.sum(a, axis=0, keepdims=True)
    jl = lax.broadcasted_iota(I32, (LANES, tq), 0)

    def online(carry, s, vt):
        m_old, acc = carry
        m_new = jnp.maximum(m_old, cmax(s))
        alpha = jnp.exp(m_old - m_new)
        return m_new, alpha * acc + _dot(vt, jnp.exp(s - m_new).astype(BF16))

    sgt = sg_ref[...].T
    gts = [[csum(jnp.where(jl == 3 * (g * R + r) + k, sgt, 0.0)) for k in range(3)] for r in range(R)]

    span = min(WINDOW + tq, kw_ref.shape[0])
    w0 = pl.multiple_of(jnp.maximum(t0 + tq - span, 0), tq)
    jw = lax.shift_right_logical(w0, int(math.log2(LANES)))
    vt = jnp.concatenate([vwt_ref[jw + c] for c in range(span // LANES)], axis=1)
    dist = tpos - (w0 + lax.broadcasted_iota(I32, (span, tq), 0))
    s = _dot(kw_ref[pl.ds(w0, span), :], q4t) + rep(jnp.where((dist >= 0) & (dist < WINDOW), 0.0, NEG))
    acc_win = _dot(vt, jnp.exp(s - cmax(s)).astype(BF16))
    o_win = acc_win[0:64] / acc_win[64:65]

    cidx = lax.broadcasted_iota(I32, (ncp, tq), 0)
    maskc = rep(jnp.where((cidx * CMP_STRIDE + (CMP_BLOCK - 1) <= tpos) & (cidx < n_c), 1.0, 0.0)) > 0.5
    s = _dot(kc_ref[...].astype(BF16), q4t)
    e = jnp.where(maskc, jnp.exp(s - cmax(jnp.where(maskc, s, NEG))), 0.0)
    p = e / jnp.maximum(csum(e), 1e-30)
    o_cmp = _dot(vct_ref[...].astype(BF16), p.astype(BF16))
    psum = p[:, 0:tq]
    for r in range(1, R):
        psum = psum + p[:, r * tq:(r + 1) * tq]

    p_hi = psum.astype(BF16)
    p_lo = (psum - p_hi.astype(F32)).astype(BF16)
    imp = _dot(ovlt_ref[...], p_hi) + _dot(ovlt_ref[...], p_lo)
    cur = lax.shift_right_logical(tpos, 6)
    forced = (jl == 0) | (jl == cur) | (jl == cur - 1)
    st = jnp.where(forced, SEL_FORCE_SCORE, jnp.where(jl <= cur, imp, -SEL_FORCE_SCORE))
    nsp = max(n_sel, SUBLANES)
    sel_rows = []
    for v in range(nsp // SUBLANES):
        gv = st[v * SUBLANES:(v + 1) * SUBLANES, :]
        jrow = v * SUBLANES + lax.broadcasted_iota(I32, (SUBLANES, tq), 0)
        cnt = jnp.zeros((SUBLANES, tq), F32)
        for i in range(n_sel):
            ri = st[i:i + 1, :]
            if v * SUBLANES + SUBLANES - 1 < i:
                beats = ri > gv
            elif v * SUBLANES > i:
                beats = ri >= gv
            else:
                beats = ((ri >= gv) & (jrow > i)) | ((ri > gv) & (jrow < i))
            cnt = cnt + jnp.where(beats, 1.0, 0.0)
        sel_rows.append(jnp.where((cnt < float(k_sel)) & (jrow < n_sel), 1.0, 0.0))
    sel_rows.append(jnp.zeros((LANES - nsp, tq), F32))
    sel = jnp.concatenate(sel_rows, axis=0)

    tk = min(SLC_TILE, ks_ref.shape[0])
    nsub = tk // LANES
    picked = sel > 0.5
    before = jl < lax.shift_right_logical(t0, 6)
    qext_diag = jnp.concatenate([rep(jnp.where(picked, 0.0, NEG)).astype(BF16), q4t], axis=0)
    qext = jnp.concatenate([rep(jnp.where(before & picked, 0.0, NEG)).astype(BF16), q4t], axis=0)
    r_i = lax.broadcasted_iota(I32, (tq, tq), 0)
    c_i = lax.broadcasted_iota(I32, (tq, tq), 1)
    s = _dot(ks_ref[pl.ds(pl.multiple_of(t0, tq), tq), :], qext_diag) + rep(jnp.where(r_i <= c_i, 0.0, NEG))
    nd = tq // LANES
    vd = jnp.concatenate([vst_ref[qi * nd + c] for c in range(nd)], axis=1)
    m_diag, acc_diag = online((jnp.full((1, R * tq), NEG, F32), jnp.zeros((V_ROWS, R * tq), F32)), s, vd)
    n_tiles = ks_ref.shape[0] // tk

    def scores(j):
        jj = jnp.minimum(j, n_tiles - 1)
        return _dot(ks_ref[pl.ds(pl.multiple_of(jj * tk, tk), tk), :], qext)

    def slc_body(j, carry):
        vt = jnp.concatenate([vst_ref[j * nsub + c] for c in range(nsub)], axis=1)
        return online(carry, scores(j), vt)

    n_main = lax.shift_right_logical(t0 + (tk - 1), int(math.log2(tk)))
    _, acc_slc = lax.fori_loop(0, n_main, slc_body, (m_diag, acc_diag))
    o_slc = acc_slc[0:64] / acc_slc[64:65]

    outs = []
    for r in range(R):
        cols = slice(r * tq, (r + 1) * tq)
        outs.append(gts[r][0] * o_cmp[:, cols] + gts[r][1] * o_slc[:, cols] + gts[r][2] * o_win[:, cols])
    for c in range(R // 2):
        o_ref[:, c * LANES:(c + 1) * LANES] = jnp.concatenate(outs[2 * c:2 * c + 2], axis=0).T.astype(BF16)


def _nsa_attn(q_rot, kvc, kvct, kse, vst, kw, vwt, sg, ovlt, B, S):
    G = NSA_KV_HEADS
    nq = S // Q_TILE
    ncp = S // CMP_STRIDE
    n_c = (S - CMP_BLOCK) // CMP_STRIDE + 1
    n_sel = S // SEL_BLOCK
    k_sel = min(N_SELECT, n_sel)
    keys = pl.BlockSpec((None, None, S, 64), lambda b, g, i: (b, g, 0, 0))
    keys_ext = pl.BlockSpec((None, None, S, KE_W), lambda b, g, i: (b, g, 0, 0))
    vals = pl.BlockSpec((None, None, S // LANES, V_ROWS, LANES), lambda b, g, i: (b, g, 0, 0, 0))
    return pl.pallas_call(
        functools.partial(_nsa_attn_kernel, ncp=ncp, n_c=n_c, n_sel=n_sel, k_sel=k_sel),
        out_shape=jax.ShapeDtypeStruct((B, S, NSA_D), BF16),
        grid=(B, G, nq),
        in_specs=[
            pl.BlockSpec((None, Q_TILE, KV_D), lambda b, g, i: (b, i, g)),
            pl.BlockSpec((None, None, None, ncp, 64), lambda b, g, i: (b, 0, g, 0, 0)),
            pl.BlockSpec((None, None, None, 64, ncp), lambda b, g, i: (b, 1, g, 0, 0)),
            keys_ext, vals, keys, vals,
            pl.BlockSpec((None, Q_TILE, LANES), lambda b, g, i: (b, i, 0)),
            pl.BlockSpec(ovlt.shape, lambda b, g, i: (0, 0)),
        ],
        out_specs=pl.BlockSpec((None, Q_TILE, KV_D), lambda b, g, i: (b, i, g)),
        compiler_params=_cparams(("parallel", "parallel", "arbitrary")),
        name="nsa_attn",
    )(q_rot, kvc, kvct, kse, vst, kw, vwt, sg, ovlt)


def _out_proj_kernel(ys_ref, yn_ref, x_ref, w_ref, nw_ref, wr_ref, br_ref, x1_ref, h2_ref, lg_ref):
    half = ys_ref.shape[1]
    y = _dot(ys_ref[...], w_ref[0:half, :]) + _dot(yn_ref[...], w_ref[half:2 * half, :])
    x1 = x_ref[...] + y
    x1_ref[...] = x1
    hn = _rms_rows(x1, nw_ref[...])
    h2_ref[...] = hn
    h1, h2, _ = _split3(hn)
    w1, w2, _ = _split3(wr_ref[...])
    lg_ref[...] = _dot(h1, w1) + _dot(h1, w2) + _dot(h2, w1) + br_ref[...]


def _out_proj(y_ssd, y_nsa, x2d, w_out_b, nw, wr, br):
    T, D = x2d.shape
    tm = min(512, T)
    half = y_ssd.shape[1]
    return pl.pallas_call(
        _out_proj_kernel,
        out_shape=(
            jax.ShapeDtypeStruct((T, D), F32),
            jax.ShapeDtypeStruct((T, D), F32),
            jax.ShapeDtypeStruct((T, LANES), F32),
        ),
        grid=(T // tm,),
        in_specs=[
            pl.BlockSpec((tm, half), lambda i: (i, 0)),
            pl.BlockSpec((tm, half), lambda i: (i, 0)),
            pl.BlockSpec((tm, D), lambda i: (i, 0)),
            pl.BlockSpec((D, D), lambda i: (0, 0)),
            pl.BlockSpec((1, D), lambda i: (0, 0)),
            pl.BlockSpec((D, LANES), lambda i: (0, 0)),
            pl.BlockSpec((1, LANES), lambda i: (0, 0)),
        ],
        out_specs=(
            pl.BlockSpec((tm, D), lambda i: (i, 0)),
            pl.BlockSpec((tm, D), lambda i: (i, 0)),
            pl.BlockSpec((tm, LANES), lambda i: (i, 0)),
        ),
        compiler_params=_cparams(("parallel",)),
        name="out_proj",
    )(y_ssd, y_nsa, x2d, w_out_b, nw, wr, br)


def _route_kernel(lg_ref, idx_ref, rank_ref, gate_ref, cnt_ref, carry, *, tr):
    i = pl.program_id(0)

    @pl.when(i == 0)
    def _():
        carry[...] = jnp.zeros(carry.shape, F32)

    lane = lax.broadcasted_iota(I32, (tr, LANES), 1)
    lanef = lane.astype(F32)
    l = jnp.where(lane < N_EXPERTS, lg_ref[...], NEG)
    ohs, vals, idxs = [], [], []
    for _ in range(TOP_K):
        m = jnp.max(l, axis=-1, keepdims=True)
        idx = jnp.min(jnp.where(l == m, lanef, float(LANES)), axis=-1, keepdims=True)
        oh = lanef == idx
        l = jnp.where(oh, 2.0 * NEG, l)
        ohs.append(oh)
        vals.append(m)
        idxs.append(idx)
    es = [jnp.exp(v - vals[0]) for v in vals]
    den = es[0] + es[1] + es[2] + es[3]
    oh_sum = jnp.zeros((tr, LANES), F32)
    for oh in ohs:
        oh_sum = oh_sum + jnp.where(oh, 1.0, 0.0)
    r_i = lax.broadcasted_iota(I32, (tr, tr), 0)
    c_i = lax.broadcasted_iota(I32, (tr, tr), 1)
    strict = jnp.where(r_i > c_i, 1.0, 0.0).astype(BF16)
    base = carry[0:1, :] + _dot(strict, oh_sum.astype(BF16))
    idx_o = jnp.zeros((tr, LANES), F32)
    rank_o = jnp.zeros((tr, LANES), F32)
    gate_o = jnp.zeros((tr, LANES), F32)
    for k in range(TOP_K):
        rk = jnp.sum(jnp.where(ohs[k], base, 0.0), axis=-1, keepdims=True)
        idx_o = jnp.where(lane == k, idxs[k], idx_o)
        rank_o = jnp.where(lane == k, rk, rank_o)
        gate_o = jnp.where(lane == k, es[k] / den, gate_o)
    idx_ref[...] = idx_o.astype(I32)
    rank_ref[...] = rank_o.astype(I32)
    gate_ref[...] = gate_o
    carry[...] = carry[...] + jnp.sum(oh_sum, axis=0, keepdims=True)
    cnt_ref[...] = carry[...]


def _route(logits):
    T = logits.shape[0]
    tr = min(512, T)
    tok = pl.BlockSpec((tr, LANES), lambda i: (i, 0))
    return pl.pallas_call(
        functools.partial(_route_kernel, tr=tr),
        out_shape=(
            jax.ShapeDtypeStruct((T, LANES), I32),
            jax.ShapeDtypeStruct((T, LANES), I32),
            jax.ShapeDtypeStruct((T, LANES), F32),
            jax.ShapeDtypeStruct((SUBLANES, LANES), F32),
        ),
        grid=(T // tr,),
        in_specs=[tok],
        out_specs=(tok, tok, tok, pl.BlockSpec((SUBLANES, LANES), lambda i: (0, 0))),
        scratch_shapes=[pltpu.VMEM((SUBLANES, LANES), F32)],
        compiler_params=_cparams(("arbitrary",)),
        name="route",
    )(logits)


def _route_fin_kernel(cnt_ref, idx_ref, rank_ref, dest_ref, be_ref, pads_ref, *, tr, nbp, tm_shift):
    lane = lax.broadcasted_iota(I32, (SUBLANES, LANES), 1)
    cnt = cnt_ref[...].astype(I32)
    tm = 1 << tm_shift
    nblk = jnp.where(lane < N_EXPERTS, lax.shift_right_logical(cnt + (tm - 1), tm_shift), 0)
    r_i = lax.broadcasted_iota(I32, (LANES, LANES), 0)
    c_i = lax.broadcasted_iota(I32, (LANES, LANES), 1)
    upper = jnp.where(r_i <= c_i, 1.0, 0.0).astype(BF16)
    end_blk = _dot(nblk.astype(F32).astype(BF16), upper)
    start_row = (end_blk - nblk.astype(F32)) * float(tm)
    idx = idx_ref[...].astype(F32)
    lane_t = lax.broadcasted_iota(I32, (tr, LANES), 1)
    lane_f = lane_t.astype(F32)
    dest = jnp.zeros((tr, LANES), F32)
    for k in range(TOP_K):
        e_k = jnp.sum(jnp.where(lane_t == k, idx, 0.0), axis=-1, keepdims=True)
        s_k = jnp.sum(jnp.where(lane_f == e_k, start_row[0:1, :], 0.0), axis=-1, keepdims=True)
        dest = jnp.where(lane_t == k, s_k, dest)
    dest_ref[...] = dest.astype(I32) + jnp.where(lane_t < TOP_K, rank_ref[...], 0)
    blk = lax.broadcasted_iota(I32, (nbp, LANES), 0).astype(F32)
    lane_b = lax.broadcasted_iota(I32, (nbp, LANES), 1)
    lane_bf = lane_b.astype(F32)
    rsum = lambda v: jnp.sum(v, axis=-1, keepdims=True)
    end_row = end_blk[0:1, :]
    nblk_row = nblk.astype(F32)[0:1, :]
    is_exp = lane_b < N_EXPERTS
    nonempty = is_exp & (nblk_row > 0.0)
    be = jnp.minimum(rsum(jnp.where(is_exp & (end_row <= blk), 1.0, 0.0)), float(N_EXPERTS - 1))
    n_used = rsum(jnp.where(lane_b == N_EXPERTS - 1, end_row, 0.0))
    start_of = rsum(jnp.where(lane_bf == be, end_row - nblk_row, 0.0))
    first = jnp.where((start_of == blk[:, 0:1]) & (blk[:, 0:1] < n_used), 1.0, 0.0)
    run = rsum(jnp.where(nonempty & (lane_bf < be), 1.0, 0.0))
    none = float(LANES)
    nxt = jnp.min(jnp.where(nonempty & (lane_bf > be), lane_bf, none), axis=-1, keepdims=True)
    first_e = jnp.min(jnp.where(nonempty, lane_bf, none), axis=-1, keepdims=True)
    last = jnp.where(nxt == none, 1.0, 0.0)
    nxt = jnp.where(nxt == none, first_e, nxt)
    n_runs = rsum(jnp.where(nonempty, 1.0, 0.0))
    tab = jnp.zeros((nbp, LANES), F32)
    for c, v in enumerate((be, n_used, first, run, nxt, last, n_runs)):
        tab = jnp.where(lane_b == c, v, tab)
    be_ref[...] = tab.astype(I32)
    sub = lax.broadcasted_iota(I32, (SUBLANES, LANES), 0)
    pad_start = start_row.astype(I32) + cnt
    pad_len = jnp.where(lane < N_EXPERTS, nblk * tm - cnt, 0)
    pads_ref[...] = jnp.where(sub == 0, pad_start, jnp.where(sub == 1, pad_len, 0))


def _route_fin(cnt, idx, rank, n_blocks):
    T = idx.shape[0]
    tr = min(512, T)
    nbp = ((n_blocks + SUBLANES - 1) // SUBLANES) * SUBLANES
    tok = pl.BlockSpec((tr, LANES), lambda i: (i, 0))
    return pl.pallas_call(
        functools.partial(_route_fin_kernel, tr=tr, nbp=nbp, tm_shift=int(math.log2(MOE_TM))),
        out_shape=(
            jax.ShapeDtypeStruct((T, LANES), I32),
            jax.ShapeDtypeStruct((nbp, LANES), I32),
            jax.ShapeDtypeStruct((SUBLANES, LANES), I32),
        ),
        grid=(T // tr,),
        in_specs=[pl.BlockSpec((SUBLANES, LANES), lambda i: (0, 0)), tok, tok],
        out_specs=(tok, pl.BlockSpec((nbp, LANES), lambda i: (0, 0)),
                   pl.BlockSpec((SUBLANES, LANES), lambda i: (0, 0))),
        compiler_params=_cparams(("arbitrary",)),
        name="route_fin",
    )(cnt, idx, rank)


def _row_copy(src_ref, src_row, dst_ref, dst_row, sem):
    return pltpu.make_async_copy(src_ref.at[pl.ds(src_row, 1)], dst_ref.at[pl.ds(dst_row, 1)], sem)


def _dispatch_kernel(dest_ref, pstart_ref, plen_ref, nu_ref, h2_ref, xs_out, zeros, sem, *, tm, nb):
    base = pl.program_id(0) * (tm * TOP_K)

    @pl.when(pl.program_id(0) == 0)
    def _():
        zeros[...] = jnp.zeros(zeros.shape, F32)

        def pad_rows(start):
            def per_expert(e, c):
                def one(r, c2):
                    cp = _row_copy(zeros, 0, xs_out, pstart_ref[e] + r, sem)
                    cp.start() if start else cp.wait()
                    return c2
                lax.fori_loop(0, plen_ref[e], one, 0)
                return c
            lax.fori_loop(0, N_EXPERTS, per_expert, 0)

            def tail(t, c):
                row0 = pl.multiple_of((nu_ref[0] + t) * MOE_TM, MOE_TM)
                cp = pltpu.make_async_copy(zeros, xs_out.at[pl.ds(row0, MOE_TM)], sem)
                cp.start() if start else cp.wait()
                return c
            lax.fori_loop(0, nb - nu_ref[0], tail, 0)

        pad_rows(True)
        pad_rows(False)

    def issue(r, c):
        for k in range(TOP_K):
            _row_copy(h2_ref, r, xs_out, dest_ref[base + r * TOP_K + k], sem).start(priority=k % 2)
        return c

    lax.fori_loop(0, tm, issue, 0, unroll=4)

    def drain(r, c):
        for k in range(TOP_K):
            _row_copy(h2_ref, r, xs_out, dest_ref[base + r * TOP_K + k], sem).wait()
        return c

    lax.fori_loop(0, tm, drain, 0, unroll=4)


def _dispatch(dest_flat, pads, nu, h2, n_pad):
    T, W = h2.shape
    tm = min(256, T)
    return pl.pallas_call(
        functools.partial(_dispatch_kernel, tm=tm, nb=n_pad // MOE_TM),
        out_shape=jax.ShapeDtypeStruct((n_pad, W), F32),
        grid_spec=pltpu.PrefetchScalarGridSpec(
            num_scalar_prefetch=4,
            grid=(T // tm,),
            in_specs=[pl.BlockSpec((tm, W), lambda i, *_: (i, 0))],
            out_specs=pl.BlockSpec(memory_space=pl.ANY),
            scratch_shapes=[pltpu.VMEM((MOE_TM, W), F32), pltpu.SemaphoreType.DMA(())],
        ),
        compiler_params=_cparams(("arbitrary",)),
        name="dispatch",
    )(dest_flat, pads[0, :N_EXPERTS], pads[1, :N_EXPERTS], nu, h2)


def _gate_up_kernel(be_ref, nu_ref, first_ref, run_ref, nxt_ref, last_ref, nr_ref,
                    x_ref, bg_ref, bu_ref, w_hbm, o_ref, wbuf, sem, *, tf, nf):
    f = pl.program_id(0)
    i = pl.program_id(1)
    used = i < nu_ref[0]

    def w_copy(e, ff, slot, part):
        col = pl.multiple_of(part * D_FF + ff * tf, tf)
        return pltpu.make_async_copy(w_hbm.at[e, :, pl.ds(col, tf)], wbuf.at[slot, part], sem.at[slot])

    @pl.when(used & (first_ref[i] == 1))
    def _():
        run = f * nr_ref[0] + run_ref[i]
        slot = run & 1
        e = be_ref[i]

        @pl.when(run == 0)
        def _():
            w_copy(e, f, slot, 0).start()
            w_copy(e, f, slot, 1).start()

        w_copy(e, f, slot, 0).wait()
        w_copy(e, f, slot, 1).wait()
        last = last_ref[i]

        @pl.when((last == 0) | (f < nf - 1))
        def _():
            w_copy(nxt_ref[i], f + last, 1 - slot, 0).start()
            w_copy(nxt_ref[i], f + last, 1 - slot, 1).start()

    @pl.when(used)
    def _():
        slot = (f * nr_ref[0] + run_ref[i]) & 1
        x = x_ref[...]
        gate = jnp.minimum(_dot(x, wbuf[slot, 0]) + bg_ref[...], SWIGLU_LIMIT)
        up = jnp.clip(_dot(x, wbuf[slot, 1]) + bu_ref[...], -SWIGLU_LIMIT, SWIGLU_LIMIT)
        glu = gate * _sigmoid(SWIGLU_ALPHA * gate)
        o_ref[...] = ((up + 1.0) * glu).astype(BF16)

    @pl.when(jnp.logical_not(used))
    def _():
        o_ref[...] = jnp.zeros(o_ref.shape, BF16)


def _gate_up(sched, xs, w_gate_up, b_gate_up3):
    n_pad, D = xs.shape
    tm = MOE_TM
    tf = 1024
    nf = D_FF // tf
    nb = n_pad // tm
    eff = lambda i, nu: jnp.minimum(i, nu[0] - 1)
    return pl.pallas_call(
        functools.partial(_gate_up_kernel, tf=tf, nf=nf),
        out_shape=jax.ShapeDtypeStruct((n_pad, D_FF), BF16),
        grid_spec=pltpu.PrefetchScalarGridSpec(
            num_scalar_prefetch=len(sched),
            grid=(nf, nb),
            in_specs=[
                pl.BlockSpec((tm, D), lambda f, i, be, nu, *_: (eff(i, nu), 0)),
                pl.BlockSpec((None, 1, tf), lambda f, i, be, nu, *_: (be[eff(i, nu)], 0, f)),
                pl.BlockSpec((None, 1, tf), lambda f, i, be, nu, *_: (be[eff(i, nu)], 0, nf + f)),
                pl.BlockSpec(memory_space=pl.ANY),
            ],
            out_specs=pl.BlockSpec((tm, tf), lambda f, i, *_: (i, f)),
            scratch_shapes=[
                pltpu.VMEM((2, 2, D, tf), F32),
                pltpu.SemaphoreType.DMA((2,)),
            ],
        ),
        compiler_params=_cparams(("arbitrary", "arbitrary")),
        name="gate_up",
    )(*sched, xs, b_gate_up3, b_gate_up3, w_gate_up)


def _down_kernel(be_ref, nu_ref, first_ref, run_ref, nxt_ref, last_ref, nr_ref,
                 h_ref, b_ref, w_hbm, o_ref, wbuf, sem):
    i = pl.program_id(0)
    used = i < nu_ref[0]

    def w_copy(e, slot):
        return pltpu.make_async_copy(w_hbm.at[e], wbuf.at[slot], sem.at[slot])

    @pl.when(used & (first_ref[i] == 1))
    def _():
        run = run_ref[i]
        slot = run & 1
        e = be_ref[i]

        @pl.when(run == 0)
        def _():
            w_copy(e, slot).start()

        w_copy(e, slot).wait()

        @pl.when(last_ref[i] == 0)
        def _():
            w_copy(nxt_ref[i], 1 - slot).start()

    @pl.when(used)
    def _():
        o_ref[...] = _dot(h_ref[...].astype(F32), wbuf[run_ref[i] & 1]) + b_ref[...]

    @pl.when(jnp.logical_not(used))
    def _():
        o_ref[...] = jnp.zeros(o_ref.shape, F32)


def _down(sched, h, w_down, b_down3):
    n_pad, F = h.shape
    D = w_down.shape[2]
    tm = MOE_TM
    nb = n_pad // tm
    eff = lambda i, nu: jnp.minimum(i, nu[0] - 1)
    return pl.pallas_call(
        _down_kernel,
        out_shape=jax.ShapeDtypeStruct((n_pad, D), F32),
        grid_spec=pltpu.PrefetchScalarGridSpec(
            num_scalar_prefetch=len(sched),
            grid=(nb,),
            in_specs=[
                pl.BlockSpec((tm, F), lambda i, be, nu, *_: (eff(i, nu), 0)),
                pl.BlockSpec((None, 1, D), lambda i, be, nu, *_: (be[eff(i, nu)], 0, 0)),
                pl.BlockSpec(memory_space=pl.ANY),
            ],
            out_specs=pl.BlockSpec((tm, D), lambda i, *_: (i, 0)),
            scratch_shapes=[
                pltpu.VMEM((2, F, D), F32),
                pltpu.SemaphoreType.DMA((2,)),
            ],
        ),
        compiler_params=_cparams(("arbitrary",)),
        name="down",
    )(*sched, h, b_down3, w_down)


def _combine_kernel(dest_ref, g_ref, x1_ref, fw_ref, y_hbm, o_ref, buf, sem, *, tm):
    base = pl.program_id(0) * (tm * TOP_K)

    def issue(r, c):
        for k in range(TOP_K):
            _row_copy(y_hbm, dest_ref[base + r * TOP_K + k], buf.at[k], r, sem).start(priority=k % 2)
        return c

    lax.fori_loop(0, tm, issue, 0, unroll=4)

    def drain(r, c):
        for k in range(TOP_K):
            _row_copy(y_hbm, dest_ref[base + r * TOP_K + k], buf.at[k], r, sem).wait()
        return c

    lax.fori_loop(0, tm, drain, 0, unroll=4)

    acc = x1_ref[...]
    gts = g_ref[...]
    for k in range(TOP_K):
        acc = acc + gts[:, k:k + 1] * buf[k]
    o_ref[...] = _rms_rows(acc, fw_ref[...])


def _combine(dest_flat, gates, x1, fw, y):
    T, D = x1.shape
    tm = min(256, T)
    return pl.pallas_call(
        functools.partial(_combine_kernel, tm=tm),
        out_shape=jax.ShapeDtypeStruct((T, D), F32),
        grid_spec=pltpu.PrefetchScalarGridSpec(
            num_scalar_prefetch=1,
            grid=(T // tm,),
            in_specs=[
                pl.BlockSpec((tm, LANES), lambda i, d: (i, 0)),
                pl.BlockSpec((tm, D), lambda i, d: (i, 0)),
                pl.BlockSpec((1, D), lambda i, d: (0, 0)),
                pl.BlockSpec(memory_space=pl.ANY),
            ],
            out_specs=pl.BlockSpec((tm, D), lambda i, d: (i, 0)),
            scratch_shapes=[pltpu.VMEM((TOP_K, tm, D), F32), pltpu.SemaphoreType.DMA(())],
        ),
        compiler_params=_cparams(("arbitrary",)),
        name="combine",
    )(dest_flat, gates, x1, fw, y)


def _rope_angles(pos):
    inv = ROPE_THETA ** (-np.arange(0, ROPE_DIM, 2, dtype=np.float64) / ROPE_DIM)
    return pos.astype(np.float64)[:, None] * inv[None, :]


def _rope_tables128(S):
    ang = _rope_angles(np.arange(S))
    c, s = np.cos(ang), np.sin(ang)
    one = np.ones((S, 64 - ROPE_DIM))
    zero = np.zeros((S, 64 - ROPE_DIM))
    z8 = np.zeros((S, 8))
    cos64 = np.concatenate([c, c, one], axis=1)
    sina64 = np.concatenate([z8, s, zero], axis=1)
    sinb64 = np.concatenate([-s, z8, zero], axis=1)
    t2 = lambda t: jnp.asarray(np.concatenate([t, t], axis=1), F32)
    return t2(cos64), t2(sina64), t2(sinb64)


def _rope_tables_cmp(ncp):
    ang = _rope_angles(np.arange(ncp) * CMP_STRIDE + CMP_BLOCK - 1)
    c, s = np.cos(ang), np.sin(ang)
    cos64 = np.concatenate([c, c, np.ones((ncp, 64 - ROPE_DIM))], axis=1)
    sin64 = np.concatenate([s, s, np.zeros((ncp, 64 - ROPE_DIM))], axis=1)
    rot = np.zeros((64, 64), np.float32)
    for d in range(8):
        rot[d + 8, d] = -1.0
        rot[d, d + 8] = 1.0
    return jnp.asarray(cos64, F32), jnp.asarray(sin64, F32), jnp.asarray(rot, BF16)


def _overlap_matrix_t(S):
    ncp = S // CMP_STRIDE
    n_c = (S - CMP_BLOCK) // CMP_STRIDE + 1
    ovl = np.zeros((LANES, ncp), np.float32)
    for c in range(n_c):
        for l in range(CMP_BLOCK):
            ovl[(c * CMP_STRIDE + l) // SEL_BLOCK, c] += 1.0
    return jnp.asarray(ovl, BF16)


def _pack_segments():
    widths = [SSM_D_INNER, SSM_CONV_DIM, SSM_HEADS, NSA_D] + [KV_D] * 6 + [3 * NSA_HEADS]
    cuts = [int(c) for c in np.cumsum([0] + widths)]
    return ((COL_Q, cuts[3], NSA_D), (COL_Z, cuts[0], SSM_D_INNER), (COL_XBC, cuts[1], SSM_CONV_DIM),
            (COL_KV, cuts[4], 6 * KV_D), (COL_DT, cuts[2], SSM_HEADS), (COL_GATE, cuts[10], 3 * NSA_HEADS))


def _pack_w_in_kernel(w_ref, o_ref):
    rows = w_ref.shape[0]
    covered = np.zeros(PROJ_W, bool)
    for dst, src, width in _pack_segments():
        covered[dst:dst + width] = True
        for c in range(0, width, LANES):
            n = min(LANES, width - c)
            o_ref[:, dst + c:dst + c + n] = w_ref[:, src + c:src + c + n].astype(BF16)
    col = 0
    while col < PROJ_W:
        if covered[col]:
            col += 1
            continue
        end = col
        while end < PROJ_W and not covered[end] and (end == col or end % LANES):
            end += 1
        o_ref[:, col:end] = jnp.zeros((rows, end - col), BF16)
        col = end


def _pack_w_in(w_in):
    D, W = w_in.shape
    tr = 256
    return pl.pallas_call(
        _pack_w_in_kernel,
        out_shape=jax.ShapeDtypeStruct((D, PROJ_W), BF16),
        grid=(D // tr,),
        in_specs=[pl.BlockSpec((tr, W), lambda i: (i, 0))],
        out_specs=pl.BlockSpec((tr, PROJ_W), lambda i: (i, 0)),
        compiler_params=_cparams(("parallel",)),
        name="pack_w_in",
    )(w_in)


def _sched_columns(tab, n_blocks):
    per_tile = lambda c: tab[:n_blocks, c]
    once = lambda c: tab[0:1, c]
    return (per_tile(0), once(1), per_tile(2), per_tile(3), per_tile(4), per_tile(5), once(6))


def kernel(x, attn_norm_w, w_in, conv_w, conv_b, dt_bias, a_log, d_skip, ssm_norm_w, cmp_pos_emb, cmp_w1,
           cmp_b1, cmp_w2, cmp_b2, w_out, moe_norm_w, w_router, b_router, w_gate_up, b_gate_up, w_down,
           b_down, final_norm_w):
    B, S, D = x.shape
    T = B * S
    x2d = x.reshape(T, D)
    row = lambda v: v.reshape(1, -1)
    padl = lambda v: jnp.pad(v.reshape(1, -1), ((0, 0), (0, LANES - v.size)))

    proj = _in_proj(x2d, row(attn_norm_w[0]), _pack_w_in(w_in[0]))
    y_ssd = _ssd(proj, B, S, conv_w[0], row(conv_b[0]), padl(dt_bias[0]), padl(a_log[0]),
                 row(jnp.repeat(d_skip[0], SSM_HEAD_DIM)), row(ssm_norm_w[0]))
    cos, sina, sinb = _rope_tables128(S)
    q_rot, kse, vst, kw, vwt, sg = _nsa_prep(proj, B, S, cos, sina, sinb)
    cosc, sinc, rot = _rope_tables_cmp(S // CMP_STRIDE)
    kvc, kvct = _compress(proj, B, S, cmp_pos_emb[0], cmp_w1[0], cmp_b1[0][:, None, :], cmp_w2[0],
                          cmp_b2[0][:, None, :], cosc, sinc, rot)
    y_nsa = _nsa_attn(q_rot, kvc, kvct, kse, vst, kw, vwt, sg, _overlap_matrix_t(S), B, S)

    wr = jnp.pad(w_router[0], ((0, 0), (0, LANES - N_EXPERTS)))
    x1, h2, logits = _out_proj(y_ssd, y_nsa.reshape(T, NSA_D), x2d, w_out[0].astype(BF16),
                                row(moe_norm_w[0]), wr, padl(b_router[0]))

    n_pad = T * TOP_K + N_EXPERTS * MOE_TM
    n_blocks = n_pad // MOE_TM
    idx, rank, gates, cnt = _route(logits)
    dest, be_tab, pads = _route_fin(cnt, idx, rank, n_blocks)
    dest_flat = dest[:, :TOP_K].reshape(-1)
    sched = _sched_columns(be_tab, n_blocks)
    xs = _dispatch(dest_flat, pads, sched[1], h2, n_pad)
    hmid = _gate_up(sched, xs, w_gate_up[0], b_gate_up[0][:, None, :])
    y = _down(sched, hmid, w_down[0], b_down[0][:, None, :])
    out = _combine(dest_flat, gates, x1, row(final_norm_w), y)
    return out.reshape(B, S, D)
```

```python
import functools
import math

import jax
import jax.numpy as jnp
import numpy as np
from jax import lax
from jax.experimental import pallas as pl
from jax.experimental.pallas import tpu as pltpu

F32 = jnp.float32
BF16 = jnp.bfloat16
I32 = jnp.int32
U32 = jnp.uint32

NORM_EPS = 1e-5
SSM_D_INNER = 1024
SSM_HEAD_DIM = 64
SSM_HEADS = 16
SSM_GROUPS = 4
SSM_STATE = 128
SSM_CONV = 4
SSM_CHUNK = 128
SSM_CONV_DIM = 2048
NSA_HEADS = 16
NSA_KV_HEADS = 4
NSA_HEAD_DIM = 64
NSA_D = 1024
KV_D = 256
CMP_BLOCK = 32
CMP_STRIDE = 16
CMP_HIDDEN = 512
SEL_BLOCK = 64
N_SELECT = 16
WINDOW = 512
ROPE_THETA = 500000.0
ROPE_DIM = 16
SEL_FORCE_SCORE = 1.0e4
N_EXPERTS = 32
TOP_K = 4
D_FF = 2048
SWIGLU_LIMIT = 7.0
SWIGLU_ALPHA = 1.702

LANES = 128
SUBLANES = 8
VMEM_LIMIT = 56 * 1024 * 1024

NEG = -1.0e30

COL_Q = 0
COL_Z = 1024
COL_XBC = 2048
COL_KV = 4096
COL_DT = 5632
COL_GATE = 5760
PROJ_W = 6144

MOE_TM = 256
Q_TILE = 256
SLC_TILE = 512
KE_W = LANES + 64
V_ROWS = 80


def _cparams(sem, vmem=VMEM_LIMIT):
    return pltpu.CompilerParams(dimension_semantics=sem, vmem_limit_bytes=vmem)


def _dot(a, b):
    return jnp.dot(a, b, preferred_element_type=F32)


def _dot_nt(a, b):
    return lax.dot_general(a, b, (((1,), (1,)), ((), ())), preferred_element_type=F32)


def _split3(b):
    b1 = b.astype(BF16)
    r1 = b - b1.astype(F32)
    b2 = r1.astype(BF16)
    r2 = r1 - b2.astype(F32)
    return b1, b2, r2.astype(BF16)


def _dot_exact_lhs(a_bf16, b):
    b1, b2, b3 = _split3(b)
    return _dot(a_bf16, b1) + _dot(a_bf16, b2) + _dot(a_bf16, b3)


def _dot_exact_rhs(a, b_bf16):
    a1, a2, a3 = _split3(a)
    return _dot(a1, b_bf16) + _dot(a2, b_bf16) + _dot(a3, b_bf16)


def _sigmoid(x):
    return 1.0 / (1.0 + jnp.exp(-x))


def _silu(x):
    return x * _sigmoid(x)


def _softplus(x):
    return jnp.maximum(x, 0.0) + jnp.log(1.0 + jnp.exp(-jnp.abs(x)))


def _gelu_tanh(x):
    c = math.sqrt(2.0 / math.pi)
    return 0.5 * x * (1.0 + jnp.tanh(c * (x + 0.044715 * (x * x * x))))


def _rms_rows(x, w):
    ms = jnp.mean(x * x, axis=-1, keepdims=True)
    return x * lax.rsqrt(ms + NORM_EPS) * w


def _in_proj_kernel(x_ref, nw_ref, w_ref, o_ref, hn_ref, *, tm):
    @pl.when(pl.program_id(1) == 0)
    def _():
        def body(c, carry):
            r = pl.multiple_of(c * 128, 128)
            hn_ref[pl.ds(r, 128), :] = _rms_rows(x_ref[pl.ds(r, 128), :], nw_ref[...]).astype(BF16)
            return carry
        lax.fori_loop(0, tm // 128, body, 0)

    o_ref[...] = _dot_nt(hn_ref[...], w_ref[...])


def _in_proj(x2d, nw, wpt):
    T, D = x2d.shape
    NP = wpt.shape[0]
    tm = min(1024, T)
    tn = 1024
    return pl.pallas_call(
        functools.partial(_in_proj_kernel, tm=tm),
        out_shape=jax.ShapeDtypeStruct((T, NP), F32),
        grid=(T // tm, NP // tn),
        in_specs=[
            pl.BlockSpec((tm, D), lambda i, j: (i, 0)),
            pl.BlockSpec((1, D), lambda i, j: (0, 0)),
            pl.BlockSpec((tn, D), lambda i, j: (j, 0)),
        ],
        out_specs=pl.BlockSpec((tm, tn), lambda i, j: (i, j)),
        scratch_shapes=[pltpu.VMEM((tm, D), BF16)],
        compiler_params=_cparams(("parallel", "arbitrary")),
        name="in_proj",
    )(x2d, nw, wpt)


def _rope128(x, cos, sina, sinb):
    return x * cos + pltpu.roll(x, 8, 1) * sina + pltpu.roll(x, 120, 1) * sinb


def _nsa_prep_kernel(q_ref, ks_ref, kw_ref, gl_ref, cos_ref, sina_ref, sinb_ref,
                     qo_ref, kso_ref, vso_ref, kwo_ref, vwo_ref, sg_ref):
    cos = cos_ref[...]
    sina = sina_ref[...]
    sinb = sinb_ref[...]
    scale = NSA_HEAD_DIM ** -0.5
    for c in range(NSA_D // LANES):
        xq = q_ref[:, c * LANES:(c + 1) * LANES]
        qo_ref[:, c * LANES:(c + 1) * LANES] = (_rope128(xq, cos, sina, sinb) * scale).astype(BF16)
    ts = q_ref.shape[0]
    key = pl.program_id(1) * ts + lax.broadcasted_iota(I32, (ts, LANES), 0)
    blk1h = jnp.where(lax.shift_right_logical(key, 6) == lax.broadcasted_iota(I32, (ts, LANES), 1), 1.0, 0.0)
    ones = jnp.ones((V_ROWS - 64, LANES), BF16)
    for src, ko, vo, ext in ((ks_ref, kso_ref, vso_ref, True), (kw_ref, kwo_ref, vwo_ref, False)):
        for c in range(KV_D // LANES):
            kr = _rope128(src[:, c * LANES:(c + 1) * LANES], cos, sina, sinb)
            vt = src[:, KV_D + c * LANES:KV_D + (c + 1) * LANES].T
            for half in range(2):
                g = 2 * c + half
                kg = kr[:, half * 64:(half + 1) * 64].astype(BF16)
                if ext:
                    ko[g, :, 0:LANES] = blk1h.astype(BF16)
                    ko[g, :, LANES:LANES + 64] = kg
                else:
                    ko[g] = kg
                for j in range(ts // LANES):
                    vo[g, j, 0:64, :] = vt[half * 64:(half + 1) * 64, j * LANES:(j + 1) * LANES].astype(BF16)
                    vo[g, j, 64:V_ROWS, :] = ones
    sg_ref[...] = _sigmoid(gl_ref[...])


def _nsa_prep(proj, B, S, cos, sina, sinb):
    ts = min(512, S)
    nst = S // ts
    G = NSA_KV_HEADS
    kv_spec = pl.BlockSpec((None, G, ts, 64), lambda b, s: (b, 0, s, 0))
    kv_shape = jax.ShapeDtypeStruct((B, G, S, 64), BF16)
    ke_spec = pl.BlockSpec((None, G, ts, KE_W), lambda b, s: (b, 0, s, 0))
    ke_shape = jax.ShapeDtypeStruct((B, G, S, KE_W), BF16)
    vt_spec = pl.BlockSpec((None, G, ts // LANES, V_ROWS, LANES), lambda b, s: (b, 0, s, 0, 0))
    vt_shape = jax.ShapeDtypeStruct((B, G, S // LANES, V_ROWS, LANES), BF16)
    return pl.pallas_call(
        _nsa_prep_kernel,
        out_shape=(
            jax.ShapeDtypeStruct((B, S, NSA_D), BF16),
            ke_shape, vt_shape, kv_shape, vt_shape,
            jax.ShapeDtypeStruct((B, S, LANES), F32),
        ),
        grid=(B, nst),
        in_specs=[
            pl.BlockSpec((ts, NSA_D), lambda b, s: (b * nst + s, COL_Q // NSA_D)),
            pl.BlockSpec((ts, 512), lambda b, s: (b * nst + s, (COL_KV + 512) // 512)),
            pl.BlockSpec((ts, 512), lambda b, s: (b * nst + s, (COL_KV + 1024) // 512)),
            pl.BlockSpec((ts, LANES), lambda b, s: (b * nst + s, COL_GATE // LANES)),
            pl.BlockSpec((ts, LANES), lambda b, s: (s, 0)),
            pl.BlockSpec((ts, LANES), lambda b, s: (s, 0)),
            pl.BlockSpec((ts, LANES), lambda b, s: (s, 0)),
        ],
        out_specs=(
            pl.BlockSpec((None, ts, NSA_D), lambda b, s: (b, s, 0)),
            ke_spec, vt_spec, kv_spec, vt_spec,
            pl.BlockSpec((None, ts, LANES), lambda b, s: (b, s, 0)),
        ),
        compiler_params=_cparams(("parallel", "parallel")),
        name="nsa_prep",
    )(proj, proj, proj, proj, cos, sina, sinb)


def _ssd_kernel(z_ref, xbc_ref, dt_ref, cw_ref, cb_ref, dtb_ref, alog_ref, dsk_ref, nw_ref,
                y_ref, buf, st):
    L = SSM_CHUNK
    c = pl.program_id(1)

    @pl.when(c == 0)
    def _():
        buf[0:8, :] = jnp.zeros((8, SSM_CONV_DIM), F32)
        st[...] = jnp.zeros(st.shape, F32)

    buf[8:8 + L, :] = xbc_ref[...]
    acc = jnp.broadcast_to(cb_ref[...], (L, SSM_CONV_DIM))
    for k in range(SSM_CONV):
        acc = acc + cw_ref[k:k + 1, :] * buf[5 + k:5 + k + L, :]
    xc = _silu(acc)
    buf[0:8, :] = xbc_ref[L - 8:L, :]

    lane = lax.broadcasted_iota(I32, (L, LANES), 1)
    row = lax.broadcasted_iota(I32, (L, LANES), 0)
    lo = lane < 64
    dtv = jnp.where(lane < SSM_HEADS, _softplus(dt_ref[...] + dtb_ref[...]), 0.0)
    a = -jnp.exp(alog_ref[...])
    tri = jnp.where(row >= lane, 1.0, 0.0).astype(BF16)
    acs = _dot_exact_lhs(tri, dtv * a)
    acs_t = acs.T
    causal = row >= lane

    ys = []
    for g in range(SSM_GROUPS):
        bg = xc[:, SSM_D_INNER + g * SSM_STATE:SSM_D_INNER + (g + 1) * SSM_STATE]
        cg = xc[:, SSM_D_INNER + (SSM_GROUPS + g) * SSM_STATE:SSM_D_INNER + (SSM_GROUPS + g + 1) * SSM_STATE]
        bgt = bg.T.astype(BF16)
        cgb = cg.astype(BF16)
        gmat = _dot(cgb, bgt)
        for p in (2 * g, 2 * g + 1):
            h0, h1 = 2 * p, 2 * p + 1
            xs_pair = xc[:, p * LANES:(p + 1) * LANES]
            col0 = acs[:, h0:h0 + 1]
            col1 = acs[:, h1:h1 + 1]
            colp = jnp.where(lo, col0, col1)
            last = jnp.where(lo[0:1, :], acs[L - 1:L, h0:h0 + 1], acs[L - 1:L, h1:h1 + 1])
            x = xs_pair * jnp.where(lo, dtv[:, h0:h0 + 1], dtv[:, h1:h1 + 1])
            m0 = (gmat * jnp.where(causal, jnp.exp(col0 - acs_t[h0:h0 + 1, :]), 0.0)).astype(BF16)
            m1 = (gmat * jnp.where(causal, jnp.exp(col1 - acs_t[h1:h1 + 1, :]), 0.0)).astype(BF16)
            y_diag = _dot(m0, jnp.where(lo, x, 0.0).astype(BF16)) + _dot(m1, jnp.where(lo, 0.0, x).astype(BF16))
            s_prev = st[p]
            y_off = _dot(cgb, s_prev.astype(BF16)) * jnp.exp(colp)
            w = (x * jnp.exp(last - colp)).astype(BF16)
            st[p] = jnp.exp(last) * s_prev + _dot(bgt, w)
            ys.append(y_diag + y_off + dsk_ref[:, p * LANES:(p + 1) * LANES] * xs_pair)
    y = jnp.concatenate(ys, axis=1)
    gte = y * _silu(z_ref[...])
    gw = SSM_D_INNER // SSM_GROUPS
    outs = []
    for k in range(SSM_GROUPS):
        gk = gte[:, k * gw:(k + 1) * gw]
        ms = jnp.mean(gk * gk, axis=-1, keepdims=True)
        outs.append(gk * lax.rsqrt(ms + NORM_EPS))
    y_ref[...] = (jnp.concatenate(outs, axis=1) * nw_ref[...]).astype(BF16)


def _ssd(proj, B, S, conv_w, conv_b, dtb, alog, dskip, nw):
    L = SSM_CHUNK
    nc = S // L
    small = lambda shp: pl.BlockSpec(shp, lambda b, c: (0, 0))
    return pl.pallas_call(
        _ssd_kernel,
        out_shape=jax.ShapeDtypeStruct((B * S, SSM_D_INNER), BF16),
        grid=(B, nc),
        in_specs=[
            pl.BlockSpec((L, SSM_D_INNER), lambda b, c: (b * nc + c, COL_Z // SSM_D_INNER)),
            pl.BlockSpec((L, SSM_CONV_DIM), lambda b, c: (b * nc + c, COL_XBC // SSM_CONV_DIM)),
            pl.BlockSpec((L, LANES), lambda b, c: (b * nc + c, COL_DT // LANES)),
            small((SSM_CONV, SSM_CONV_DIM)),
            small((1, SSM_CONV_DIM)),
            small((1, LANES)),
            small((1, LANES)),
            small((1, SSM_D_INNER)),
            small((1, SSM_D_INNER)),
        ],
        out_specs=pl.BlockSpec((L, SSM_D_INNER), lambda b, c: (b * nc + c, 0)),
        scratch_shapes=[
            pltpu.VMEM((L + 8, SSM_CONV_DIM), F32),
            pltpu.VMEM((SSM_HEADS // 2, SSM_STATE, LANES), F32),
        ],
        compiler_params=_cparams(("parallel", "arbitrary")),
        name="ssd",
    )(proj, proj, proj, conv_w, conv_b, dtb, alog, dskip, nw)


def _compress_kernel(xa_ref, xb_ref, pos_ref, w1_ref, b1_ref, w2_ref, b2_ref, cos_ref, sin_ref, rot_ref,
                     o_ref, ot_ref, w1b, feat, *, ncp):
    kv = pl.program_id(1)
    outs = []
    half = CMP_STRIDE
    hw = half * 64
    w1b[...] = w1_ref[...].astype(BF16)
    w2 = w2_ref[...].astype(BF16)
    G = NSA_KV_HEADS
    is_k = kv == 0
    for g in range(G):
        x_ref = (xa_ref, xb_ref)[g // 2]
        for l in range(half):
            xg = x_ref[pl.ds(l, ncp, stride=half), :][:, (g % 2) * 64:(g % 2 + 1) * 64]
            feat[:, l * 64:(l + 1) * 64] = xg + pos_ref[l:l + 1, :]
            feat[:, hw + l * 64:hw + (l + 1) * 64] = xg + pos_ref[half + l:half + l + 1, :]
        first = _dot(feat[:, 0:hw].astype(BF16), w1b[0:hw, :])
        second = _dot(feat[:, hw:2 * hw].astype(BF16), w1b[hw:2 * hw, :])
        h = first + pltpu.roll(second, ncp - 1, 0) + b1_ref[...]
        h = _gelu_tanh(h)
        o = _dot(h.astype(BF16), w2) + b2_ref[...]
        roped = o * cos_ref[...] + _dot_exact_rhs(o, rot_ref[...]) * sin_ref[...]
        outs.append(jnp.where(is_k, roped, o))
        o_ref[g] = outs[g]
    for c in range(G // 2):
        t = jnp.concatenate([outs[2 * c], outs[2 * c + 1]], axis=1).T
        ot_ref[2 * c] = t[0:64, :]
        ot_ref[2 * c + 1] = t[64:128, :]


def _compress(proj, B, S, pos, w1, b1, w2, b2, cosc, sinc, rot):
    ncp = S // CMP_STRIDE
    G = NSA_KV_HEADS
    kvblk = COL_KV // KV_D
    return pl.pallas_call(
        functools.partial(_compress_kernel, ncp=ncp),
        out_shape=(jax.ShapeDtypeStruct((B, 2, G, ncp, 64), F32),
                   jax.ShapeDtypeStruct((B, 2, G, 64, ncp), F32)),
        grid=(B, 2),
        in_specs=[
            pl.BlockSpec((S, LANES), lambda b, k: (b, 2 * (kvblk + k))),
            pl.BlockSpec((S, LANES), lambda b, k: (b, 2 * (kvblk + k) + 1)),
            pl.BlockSpec((None, CMP_BLOCK, 64), lambda b, k: (k, 0, 0)),
            pl.BlockSpec((None, CMP_BLOCK * 64, CMP_HIDDEN), lambda b, k: (k, 0, 0)),
            pl.BlockSpec((None, 1, CMP_HIDDEN), lambda b, k: (k, 0, 0)),
            pl.BlockSpec((None, CMP_HIDDEN, 64), lambda b, k: (k, 0, 0)),
            pl.BlockSpec((None, 1, 64), lambda b, k: (k, 0, 0)),
            pl.BlockSpec((ncp, 64), lambda b, k: (0, 0)),
            pl.BlockSpec((ncp, 64), lambda b, k: (0, 0)),
            pl.BlockSpec((64, 64), lambda b, k: (0, 0)),
        ],
        out_specs=(pl.BlockSpec((None, None, G, ncp, 64), lambda b, k: (b, k, 0, 0, 0)),
                   pl.BlockSpec((None, None, G, 64, ncp), lambda b, k: (b, k, 0, 0, 0))),
        scratch_shapes=[pltpu.VMEM((CMP_BLOCK * 64, CMP_HIDDEN), BF16),
                        pltpu.VMEM((ncp, CMP_BLOCK * 64), F32)],
        compiler_params=_cparams(("parallel", "parallel")),
        name="compress",
    )(proj, proj, pos, w1, b1, w2, b2, cosc, sinc, rot)


def _nsa_attn_kernel(q_ref, kc_ref, vct_ref, ks_ref, vst_ref, kw_ref, vwt_ref, sg_ref, ovlt_ref,
                     o_ref, *, ncp, n_c, n_sel, k_sel):
    g = pl.program_id(1)
    qi = pl.program_id(2)
    tq = Q_TILE
    R = NSA_HEADS // NSA_KV_HEADS
    t0 = qi * tq
    qt = q_ref[...].astype(F32).T
    q4t = jnp.concatenate([qt[r * 64:(r + 1) * 64, :] for r in range(R)], axis=1).astype(BF16)
    rep = lambda a: jnp.concatenate([a] * R, axis=1)
    tpos = t0 + lax.broadcasted_iota(I32, (1, tq), 1)
    cmax = lambda a: jnp.max(a, axis=0, keepdims=True)
    csum = lambda a: jnp.sum(a, axis=0, keepdims=True)
    jl = lax.broadcasted_iota(I32, (LANES, tq), 0)

    def online(carry, s, vt):
        m_old, acc = carry
        m_new = jnp.maximum(m_old, cmax(s))
        alpha = jnp.exp(m_old - m_new)
        return m_new, alpha * acc + _dot(vt, jnp.exp(s - m_new).astype(BF16))

    sgt = sg_ref[...].T
    gts = [[csum(jnp.where(jl == 3 * (g * R + r) + k, sgt, 0.0)) for k in range(3)] for r in range(R)]

    span = min(WINDOW + tq, kw_ref.shape[0])
    w0 = pl.multiple_of(jnp.maximum(t0 + tq - span, 0), tq)
    jw = lax.shift_right_logical(w0, int(math.log2(LANES)))
    vt = jnp.concatenate([vwt_ref[jw + c] for c in range(span // LANES)], axis=1)
    dist = tpos - (w0 + lax.broadcasted_iota(I32, (span, tq), 0))
    s = _dot(kw_ref[pl.ds(w0, span), :], q4t) + rep(jnp.where((dist >= 0) & (dist < WINDOW), 0.0, NEG))
    acc_win = _dot(vt, jnp.exp(s - cmax(s)).astype(BF16))
    o_win = acc_win[0:64] / acc_win[64:65]

    cidx = lax.broadcasted_iota(I32, (ncp, tq), 0)
    maskc = rep(jnp.where((cidx * CMP_STRIDE + (CMP_BLOCK - 1) <= tpos) & (cidx < n_c), 1.0, 0.0)) > 0.5
    s = _dot(kc_ref[...].astype(BF16), q4t)
    e = jnp.where(maskc, jnp.exp(s - cmax(jnp.where(maskc, s, NEG))), 0.0)
    p = e / jnp.maximum(csum(e), 1e-30)
    o_cmp = _dot(vct_ref[...].astype(BF16), p.astype(BF16))
    psum = p[:, 0:tq]
    for r in range(1, R):
        psum = psum + p[:, r * tq:(r + 1) * tq]

    p_hi = psum.astype(BF16)
    p_lo = (psum - p_hi.astype(F32)).astype(BF16)
    imp = _dot(ovlt_ref[...], p_hi) + _dot(ovlt_ref[...], p_lo)
    cur = lax.shift_right_logical(tpos, 6)
    forced = (jl == 0) | (jl == cur) | (jl == cur - 1)
    st = jnp.where(forced, SEL_FORCE_SCORE, jnp.where(jl <= cur, imp, -SEL_FORCE_SCORE))
    nsp = max(n_sel, SUBLANES)
    sel_rows = []
    for v in range(nsp // SUBLANES):
        gv = st[v * SUBLANES:(v + 1) * SUBLANES, :]
        jrow = v * SUBLANES + lax.broadcasted_iota(I32, (SUBLANES, tq), 0)
        cnt = jnp.zeros((SUBLANES, tq), F32)
        for i in range(n_sel):
            ri = st[i:i + 1, :]
            if v * SUBLANES + SUBLANES - 1 < i:
                beats = ri > gv
            elif v * SUBLANES > i:
                beats = ri >= gv
            else:
                beats = ((ri >= gv) & (jrow > i)) | ((ri > gv) & (jrow < i))
            cnt = cnt + jnp.where(beats, 1.0, 0.0)
        sel_rows.append(jnp.where((cnt < float(k_sel)) & (jrow < n_sel), 1.0, 0.0))
    sel_rows.append(jnp.zeros((LANES - nsp, tq), F32))
    sel = jnp.concatenate(sel_rows, axis=0)

    tk = min(SLC_TILE, ks_ref.shape[0])
    nsub = tk // LANES
    picked = sel > 0.5
    before = jl < lax.shift_right_logical(t0, 6)
    qext_diag = jnp.concatenate([rep(jnp.where(picked, 0.0, NEG)).astype(BF16), q4t], axis=0)
    qext = jnp.concatenate([rep(jnp.where(before & picked, 0.0, NEG)).astype(BF16), q4t], axis=0)
    r_i = lax.broadcasted_iota(I32, (tq, tq), 0)
    c_i = lax.broadcasted_iota(I32, (tq, tq), 1)
    s = _dot(ks_ref[pl.ds(pl.multiple_of(t0, tq), tq), :], qext_diag) + rep(jnp.where(r_i <= c_i, 0.0, NEG))
    nd = tq // LANES
    vd = jnp.concatenate([vst_ref[qi * nd + c] for c in range(nd)], axis=1)
    m_diag, acc_diag = online((jnp.full((1, R * tq), NEG, F32), jnp.zeros((V_ROWS, R * tq), F32)), s, vd)
    n_tiles = ks_ref.shape[0] // tk

    def scores(j):
        jj = jnp.minimum(j, n_tiles - 1)
        return _dot(ks_ref[pl.ds(pl.multiple_of(jj * tk, tk), tk), :], qext)

    def slc_body(j, carry):
        vt = jnp.concatenate([vst_ref[j * nsub + c] for c in range(nsub)], axis=1)
        return online(carry, scores(j), vt)

    n_main = lax.shift_right_logical(t0 + (tk - 1), int(math.log2(tk)))
    _, acc_slc = lax.fori_loop(0, n_main, slc_body, (m_diag, acc_diag))
    o_slc = acc_slc[0:64] / acc_slc[64:65]

    outs = []
    for r in range(R):
        cols = slice(r * tq, (r + 1) * tq)
        outs.append(gts[r][0] * o_cmp[:, cols] + gts[r][1] * o_slc[:, cols] + gts[r][2] * o_win[:, cols])
    for c in range(R // 2):
        o_ref[:, c * LANES:(c + 1) * LANES] = jnp.concatenate(outs[2 * c:2 * c + 2], axis=0).T.astype(BF16)


def _nsa_attn(q_rot, kvc, kvct, kse, vst, kw, vwt, sg, ovlt, B, S):
    G = NSA_KV_HEADS
    nq = S // Q_TILE
    ncp = S // CMP_STRIDE
    n_c = (S - CMP_BLOCK) // CMP_STRIDE + 1
    n_sel = S // SEL_BLOCK
    k_sel = min(N_SELECT, n_sel)
    keys = pl.BlockSpec((None, None, S, 64), lambda b, g, i: (b, g, 0, 0))
    keys_ext = pl.BlockSpec((None, None, S, KE_W), lambda b, g, i: (b, g, 0, 0))
    vals = pl.BlockSpec((None, None, S // LANES, V_ROWS, LANES), lambda b, g, i: (b, g, 0, 0, 0))
    return pl.pallas_call(
        functools.partial(_nsa_attn_kernel, ncp=ncp, n_c=n_c, n_sel=n_sel, k_sel=k_sel),
        out_shape=jax.ShapeDtypeStruct((B, S, NSA_D), BF16),
        grid=(B, G, nq),
        in_specs=[
            pl.BlockSpec((None, Q_TILE, KV_D), lambda b, g, i: (b, i, g)),
            pl.BlockSpec((None, None, None, ncp, 64), lambda b, g, i: (b, 0, g, 0, 0)),
            pl.BlockSpec((None, None, None, 64, ncp), lambda b, g, i: (b, 1, g, 0, 0)),
            keys_ext, vals, keys, vals,
            pl.BlockSpec((None, Q_TILE, LANES), lambda b, g, i: (b, i, 0)),
            pl.BlockSpec(ovlt.shape, lambda b, g, i: (0, 0)),
        ],
        out_specs=pl.BlockSpec((None, Q_TILE, KV_D), lambda b, g, i: (b, i, g)),
        compiler_params=_cparams(("parallel", "parallel", "arbitrary")),
        name="nsa_attn",
    )(q_rot, kvc, kvct, kse, vst, kw, vwt, sg, ovlt)


def _out_proj_kernel(ys_ref, yn_ref, x_ref, w_ref, nw_ref, wr_ref, br_ref, x1_ref, h2_ref, lg_ref):
    half = ys_ref.shape[1]
    y = _dot(ys_ref[...], w_ref[0:half, :]) + _dot(yn_ref[...], w_ref[half:2 * half, :])
    x1 = x_ref[...] + y
    x1_ref[...] = x1
    hn = _rms_rows(x1, nw_ref[...])
    h2_ref[...] = hn
    h1, h2, _ = _split3(hn)
    w1, w2, _ = _split3(wr_ref[...])
    lg_ref[...] = _dot(h1, w1) + _dot(h1, w2) + _dot(h2, w1) + br_ref[...]


def _out_proj(y_ssd, y_nsa, x2d, w_out_b, nw, wr, br):
    T, D = x2d.shape
    tm = min(512, T)
    half = y_ssd.shape[1]
    return pl.pallas_call(
        _out_proj_kernel,
        out_shape=(
            jax.ShapeDtypeStruct((T, D), F32),
            jax.ShapeDtypeStruct((T, D), F32),
            jax.ShapeDtypeStruct((T, LANES), F32),
        ),
        grid=(T // tm,),
        in_specs=[
            pl.BlockSpec((tm, half), lambda i: (i, 0)),
            pl.BlockSpec((tm, half), lambda i: (i, 0)),
            pl.BlockSpec((tm, D), lambda i: (i, 0)),
            pl.BlockSpec((D, D), lambda i: (0, 0)),
            pl.BlockSpec((1, D), lambda i: (0, 0)),
            pl.BlockSpec((D, LANES), lambda i: (0, 0)),
            pl.BlockSpec((1, LANES), lambda i: (0, 0)),
        ],
        out_specs=(
            pl.BlockSpec((tm, D), lambda i: (i, 0)),
            pl.BlockSpec((tm, D), lambda i: (i, 0)),
            pl.BlockSpec((tm, LANES), lambda i: (i, 0)),
        ),
        compiler_params=_cparams(("parallel",)),
        name="out_proj",
    )(y_ssd, y_nsa, x2d, w_out_b, nw, wr, br)


def _route_kernel(lg_ref, idx_ref, rank_ref, gate_ref, cnt_ref, carry, *, tr):
    i = pl.program_id(0)

    @pl.when(i == 0)
    def _():
        carry[...] = jnp.zeros(carry.shape, F32)

    lane = lax.broadcasted_iota(I32, (tr, LANES), 1)
    lanef = lane.astype(F32)
    l = jnp.where(lane < N_EXPERTS, lg_ref[...], NEG)
    ohs, vals, idxs = [], [], []
    for _ in range(TOP_K):
        m = jnp.max(l, axis=-1, keepdims=True)
        idx = jnp.min(jnp.where(l == m, lanef, float(LANES)), axis=-1, keepdims=True)
        oh = lanef == idx
        l = jnp.where(oh, 2.0 * NEG, l)
        ohs.append(oh)
        vals.append(m)
        idxs.append(idx)
    es = [jnp.exp(v - vals[0]) for v in vals]
    den = es[0] + es[1] + es[2] + es[3]
    oh_sum = jnp.zeros((tr, LANES), F32)
    for oh in ohs:
        oh_sum = oh_sum + jnp.where(oh, 1.0, 0.0)
    r_i = lax.broadcasted_iota(I32, (tr, tr), 0)
    c_i = lax.broadcasted_iota(I32, (tr, tr), 1)
    strict = jnp.where(r_i > c_i, 1.0, 0.0).astype(BF16)
    base = carry[0:1, :] + _dot(strict, oh_sum.astype(BF16))
    idx_o = jnp.zeros((tr, LANES), F32)
    rank_o = jnp.zeros((tr, LANES), F32)
    gate_o = jnp.zeros((tr, LANES), F32)
    for k in range(TOP_K):
        rk = jnp.sum(jnp.where(ohs[k], base, 0.0), axis=-1, keepdims=True)
        idx_o = jnp.where(lane == k, idxs[k], idx_o)
        rank_o = jnp.where(lane == k, rk, rank_o)
        gate_o = jnp.where(lane == k, es[k] / den, gate_o)
    idx_ref[...] = idx_o.astype(I32)
    rank_ref[...] = rank_o.astype(I32)
    gate_ref[...] = gate_o
    carry[...] = carry[...] + jnp.sum(oh_sum, axis=0, keepdims=True)
    cnt_ref[...] = carry[...]


def _route(logits):
    T = logits.shape[0]
    tr = min(512, T)
    tok = pl.BlockSpec((tr, LANES), lambda i: (i, 0))
    return pl.pallas_call(
        functools.partial(_route_kernel, tr=tr),
        out_shape=(
            jax.ShapeDtypeStruct((T, LANES), I32),
            jax.ShapeDtypeStruct((T, LANES), I32),
            jax.ShapeDtypeStruct((T, LANES), F32),
            jax.ShapeDtypeStruct((SUBLANES, LANES), F32),
        ),
        grid=(T // tr,),
        in_specs=[tok],
        out_specs=(tok, tok, tok, pl.BlockSpec((SUBLANES, LANES), lambda i: (0, 0))),
        scratch_shapes=[pltpu.VMEM((SUBLANES, LANES), F32)],
        compiler_params=_cparams(("arbitrary",)),
        name="route",
    )(logits)


def _route_fin_kernel(cnt_ref, idx_ref, rank_ref, dest_ref, be_ref, pads_ref, *, tr, nbp, tm_shift):
    lane = lax.broadcasted_iota(I32, (SUBLANES, LANES), 1)
    cnt = cnt_ref[...].astype(I32)
    tm = 1 << tm_shift
    nblk = jnp.where(lane < N_EXPERTS, lax.shift_right_logical(cnt + (tm - 1), tm_shift), 0)
    r_i = lax.broadcasted_iota(I32, (LANES, LANES), 0)
    c_i = lax.broadcasted_iota(I32, (LANES, LANES), 1)
    upper = jnp.where(r_i <= c_i, 1.0, 0.0).astype(BF16)
    end_blk = _dot(nblk.astype(F32).astype(BF16), upper)
    start_row = (end_blk - nblk.astype(F32)) * float(tm)
    idx = idx_ref[...].astype(F32)
    lane_t = lax.broadcasted_iota(I32, (tr, LANES), 1)
    lane_f = lane_t.astype(F32)
    dest = jnp.zeros((tr, LANES), F32)
    for k in range(TOP_K):
        e_k = jnp.sum(jnp.where(lane_t == k, idx, 0.0), axis=-1, keepdims=True)
        s_k = jnp.sum(jnp.where(lane_f == e_k, start_row[0:1, :], 0.0), axis=-1, keepdims=True)
        dest = jnp.where(lane_t == k, s_k, dest)
    dest_ref[...] = dest.astype(I32) + jnp.where(lane_t < TOP_K, rank_ref[...], 0)
    blk = lax.broadcasted_iota(I32, (nbp, LANES), 0).astype(F32)
    lane_b = lax.broadcasted_iota(I32, (nbp, LANES), 1)
    lane_bf = lane_b.astype(F32)
    rsum = lambda v: jnp.sum(v, axis=-1, keepdims=True)
    end_row = end_blk[0:1, :]
    nblk_row = nblk.astype(F32)[0:1, :]
    is_exp = lane_b < N_EXPERTS
    nonempty = is_exp & (nblk_row > 0.0)
    be = jnp.minimum(rsum(jnp.where(is_exp & (end_row <= blk), 1.0, 0.0)), float(N_EXPERTS - 1))
    n_used = rsum(jnp.where(lane_b == N_EXPERTS - 1, end_row, 0.0))
    start_of = rsum(jnp.where(lane_bf == be, end_row - nblk_row, 0.0))
    first = jnp.where((start_of == blk[:, 0:1]) & (blk[:, 0:1] < n_used), 1.0, 0.0)
    run = rsum(jnp.where(nonempty & (lane_bf < be), 1.0, 0.0))
    none = float(LANES)
    nxt = jnp.min(jnp.where(nonempty & (lane_bf > be), lane_bf, none), axis=-1, keepdims=True)
    first_e = jnp.min(jnp.where(nonempty, lane_bf, none), axis=-1, keepdims=True)
    last = jnp.where(nxt == none, 1.0, 0.0)
    nxt = jnp.where(nxt == none, first_e, nxt)
    n_runs = rsum(jnp.where(nonempty, 1.0, 0.0))
    cnt_of = rsum(jnp.where(lane_bf == be, cnt.astype(F32)[0:1, :], 0.0))
    vrows = jnp.clip(cnt_of - (blk[:, 0:1] - start_of) * float(tm), 0.0, float(tm))
    tab = jnp.zeros((nbp, LANES), F32)
    for c, v in enumerate((be, n_used, first, run, nxt, last, n_runs, vrows)):
        tab = jnp.where(lane_b == c, v, tab)
    be_ref[...] = tab.astype(I32)
    sub = lax.broadcasted_iota(I32, (SUBLANES, LANES), 0)
    pad_start = start_row.astype(I32) + cnt
    pad_len = jnp.where(lane < N_EXPERTS, nblk * tm - cnt, 0)
    pads_ref[...] = jnp.where(sub == 0, pad_start, jnp.where(sub == 1, pad_len, 0))


def _route_fin(cnt, idx, rank, n_blocks):
    T = idx.shape[0]
    tr = min(512, T)
    nbp = ((n_blocks + SUBLANES - 1) // SUBLANES) * SUBLANES
    tok = pl.BlockSpec((tr, LANES), lambda i: (i, 0))
    return pl.pallas_call(
        functools.partial(_route_fin_kernel, tr=tr, nbp=nbp, tm_shift=int(math.log2(MOE_TM))),
        out_shape=(
            jax.ShapeDtypeStruct((T, LANES), I32),
            jax.ShapeDtypeStruct((nbp, LANES), I32),
            jax.ShapeDtypeStruct((SUBLANES, LANES), I32),
        ),
        grid=(T // tr,),
        in_specs=[pl.BlockSpec((SUBLANES, LANES), lambda i: (0, 0)), tok, tok],
        out_specs=(tok, pl.BlockSpec((nbp, LANES), lambda i: (0, 0)),
                   pl.BlockSpec((SUBLANES, LANES), lambda i: (0, 0))),
        compiler_params=_cparams(("arbitrary",)),
        name="route_fin",
    )(cnt, idx, rank)


def _row_copy(src_ref, src_row, dst_ref, dst_row, sem):
    return pltpu.make_async_copy(src_ref.at[pl.ds(src_row, 1)], dst_ref.at[pl.ds(dst_row, 1)], sem)


def _dispatch_kernel(dest_ref, pstart_ref, plen_ref, nu_ref, h2_ref, xs_out, zeros, sem, *, tm, nb):
    base = pl.program_id(0) * (tm * TOP_K)

    @pl.when(pl.program_id(0) == 0)
    def _():
        zeros[...] = jnp.zeros(zeros.shape, F32)

        def pad_rows(start):
            def per_expert(e, c):
                def one(r, c2):
                    cp = _row_copy(zeros, 0, xs_out, pstart_ref[e] + r, sem)
                    cp.start() if start else cp.wait()
                    return c2
                lax.fori_loop(0, plen_ref[e], one, 0)
                return c
            lax.fori_loop(0, N_EXPERTS, per_expert, 0)

            def tail(t, c):
                row0 = pl.multiple_of((nu_ref[0] + t) * MOE_TM, MOE_TM)
                cp = pltpu.make_async_copy(zeros, xs_out.at[pl.ds(row0, MOE_TM)], sem)
                cp.start() if start else cp.wait()
                return c
            lax.fori_loop(0, nb - nu_ref[0], tail, 0)

        pad_rows(True)
        pad_rows(False)

    def issue(r, c):
        for k in range(TOP_K):
            _row_copy(h2_ref, r, xs_out, dest_ref[base + r * TOP_K + k], sem).start(priority=k % 2)
        return c

    lax.fori_loop(0, tm, issue, 0, unroll=4)

    def drain(r, c):
        for k in range(TOP_K):
            _row_copy(h2_ref, r, xs_out, dest_ref[base + r * TOP_K + k], sem).wait()
        return c

    lax.fori_loop(0, tm, drain, 0, unroll=4)


def _dispatch(dest_flat, pads, nu, h2, n_pad):
    T, W = h2.shape
    tm = min(256, T)
    return pl.pallas_call(
        functools.partial(_dispatch_kernel, tm=tm, nb=n_pad // MOE_TM),
        out_shape=jax.ShapeDtypeStruct((n_pad, W), F32),
        grid_spec=pltpu.PrefetchScalarGridSpec(
            num_scalar_prefetch=4,
            grid=(T // tm,),
            in_specs=[pl.BlockSpec((tm, W), lambda i, *_: (i, 0))],
            out_specs=pl.BlockSpec(memory_space=pl.ANY),
            scratch_shapes=[pltpu.VMEM((MOE_TM, W), F32), pltpu.SemaphoreType.DMA(())],
        ),
        compiler_params=_cparams(("arbitrary",)),
        name="dispatch",
    )(dest_flat, pads[0, :N_EXPERTS], pads[1, :N_EXPERTS], nu, h2)


def _gate_up_kernel(be_ref, nu_ref, first_ref, run_ref, nxt_ref, last_ref, nr_ref, vr_ref,
                    x_ref, bg_ref, bu_ref, w_hbm, o_ref, wbuf, sem, *, tf, nf):
    f = pl.program_id(0)
    i = pl.program_id(1)
    used = i < nu_ref[0]

    def w_copy(e, ff, slot, part):
        col = pl.multiple_of(part * D_FF + ff * tf, tf)
        return pltpu.make_async_copy(w_hbm.at[e, :, pl.ds(col, tf)], wbuf.at[slot, part], sem.at[slot])

    @pl.when(used & (first_ref[i] == 1))
    def _():
        run = f * nr_ref[0] + run_ref[i]
        slot = run & 1
        e = be_ref[i]

        @pl.when(run == 0)
        def _():
            w_copy(e, f, slot, 0).start()
            w_copy(e, f, slot, 1).start()

        w_copy(e, f, slot, 0).wait()
        w_copy(e, f, slot, 1).wait()
        last = last_ref[i]

        @pl.when((last == 0) | (f < nf - 1))
        def _():
            w_copy(nxt_ref[i], f + last, 1 - slot, 0).start()
            w_copy(nxt_ref[i], f + last, 1 - slot, 1).start()

    def swiglu(x, slot):
        gate = jnp.minimum(_dot(x, wbuf[slot, 0]) + bg_ref[...], SWIGLU_LIMIT)
        up = jnp.clip(_dot(x, wbuf[slot, 1]) + bu_ref[...], -SWIGLU_LIMIT, SWIGLU_LIMIT)
        return ((up + 1.0) * (gate * _sigmoid(SWIGLU_ALPHA * gate))).astype(BF16)

    half = x_ref.shape[0] // 2
    full = vr_ref[i] > half

    @pl.when(used & full)
    def _():
        o_ref[...] = swiglu(x_ref[...], (f * nr_ref[0] + run_ref[i]) & 1)

    @pl.when(used & jnp.logical_not(full))
    def _():
        o_ref[0:half, :] = swiglu(x_ref[0:half, :], (f * nr_ref[0] + run_ref[i]) & 1)
        o_ref[half:2 * half, :] = jnp.zeros((half, o_ref.shape[1]), BF16)

    @pl.when(jnp.logical_not(used))
    def _():
        o_ref[...] = jnp.zeros(o_ref.shape, BF16)


def _gate_up(sched, xs, w_gate_up, b_gate_up3):
    n_pad, D = xs.shape
    tm = MOE_TM
    tf = 1024
    nf = D_FF // tf
    nb = n_pad // tm
    eff = lambda i, nu: jnp.minimum(i, nu[0] - 1)
    return pl.pallas_call(
        functools.partial(_gate_up_kernel, tf=tf, nf=nf),
        out_shape=jax.ShapeDtypeStruct((n_pad, D_FF), BF16),
        grid_spec=pltpu.PrefetchScalarGridSpec(
            num_scalar_prefetch=len(sched),
            grid=(nf, nb),
            in_specs=[
                pl.BlockSpec((tm, D), lambda f, i, be, nu, *_: (eff(i, nu), 0)),
                pl.BlockSpec((None, 1, tf), lambda f, i, be, nu, *_: (be[eff(i, nu)], 0, f)),
                pl.BlockSpec((None, 1, tf), lambda f, i, be, nu, *_: (be[eff(i, nu)], 0, nf + f)),
                pl.BlockSpec(memory_space=pl.ANY),
            ],
            out_specs=pl.BlockSpec((tm, tf), lambda f, i, *_: (i, f)),
            scratch_shapes=[
                pltpu.VMEM((2, 2, D, tf), F32),
                pltpu.SemaphoreType.DMA((2,)),
            ],
        ),
        compiler_params=_cparams(("arbitrary", "arbitrary")),
        name="gate_up",
    )(*sched, xs, b_gate_up3, b_gate_up3, w_gate_up)


def _down_kernel(be_ref, nu_ref, first_ref, run_ref, nxt_ref, last_ref, nr_ref, vr_ref,
                 h_ref, b_ref, w_hbm, o_ref, wbuf, sem):
    i = pl.program_id(0)
    used = i < nu_ref[0]

    def w_copy(e, slot):
        return pltpu.make_async_copy(w_hbm.at[e], wbuf.at[slot], sem.at[slot])

    @pl.when(used & (first_ref[i] == 1))
    def _():
        run = run_ref[i]
        slot = run & 1
        e = be_ref[i]

        @pl.when(run == 0)
        def _():
            w_copy(e, slot).start()

        w_copy(e, slot).wait()

        @pl.when(last_ref[i] == 0)
        def _():
            w_copy(nxt_ref[i], 1 - slot).start()

    half = h_ref.shape[0] // 2
    full = vr_ref[i] > half

    @pl.when(used & full)
    def _():
        o_ref[...] = _dot(h_ref[...].astype(F32), wbuf[run_ref[i] & 1]) + b_ref[...]

    @pl.when(used & jnp.logical_not(full))
    def _():
        o_ref[0:half, :] = _dot(h_ref[0:half, :].astype(F32), wbuf[run_ref[i] & 1]) + b_ref[...]
        o_ref[half:2 * half, :] = jnp.zeros((half, o_ref.shape[1]), F32)

    @pl.when(jnp.logical_not(used))
    def _():
        o_ref[...] = jnp.zeros(o_ref.shape, F32)


def _down(sched, h, w_down, b_down3):
    n_pad, F = h.shape
    D = w_down.shape[2]
    tm = MOE_TM
    nb = n_pad // tm
    eff = lambda i, nu: jnp.minimum(i, nu[0] - 1)
    return pl.pallas_call(
        _down_kernel,
        out_shape=jax.ShapeDtypeStruct((n_pad, D), F32),
        grid_spec=pltpu.PrefetchScalarGridSpec(
            num_scalar_prefetch=len(sched),
            grid=(nb,),
            in_specs=[
                pl.BlockSpec((tm, F), lambda i, be, nu, *_: (eff(i, nu), 0)),
                pl.BlockSpec((None, 1, D), lambda i, be, nu, *_: (be[eff(i, nu)], 0, 0)),
                pl.BlockSpec(memory_space=pl.ANY),
            ],
            out_specs=pl.BlockSpec((tm, D), lambda i, *_: (i, 0)),
            scratch_shapes=[
                pltpu.VMEM((2, F, D), F32),
                pltpu.SemaphoreType.DMA((2,)),
            ],
        ),
        compiler_params=_cparams(("arbitrary",)),
        name="down",
    )(*sched, h, b_down3, w_down)


def _combine_kernel(dest_ref, g_ref, x1_ref, fw_ref, y_hbm, o_ref, buf, sem, *, tm):
    base = pl.program_id(0) * (tm * TOP_K)

    def issue(r, c):
        for k in range(TOP_K):
            _row_copy(y_hbm, dest_ref[base + r * TOP_K + k], buf.at[k], r, sem).start(priority=k % 2)
        return c

    lax.fori_loop(0, tm, issue, 0, unroll=4)

    def drain(r, c):
        for k in range(TOP_K):
            _row_copy(y_hbm, dest_ref[base + r * TOP_K + k], buf.at[k], r, sem).wait()
        return c

    lax.fori_loop(0, tm, drain, 0, unroll=4)

    acc = x1_ref[...]
    gts = g_ref[...]
    for k in range(TOP_K):
        acc = acc + gts[:, k:k + 1] * buf[k]
    o_ref[...] = _rms_rows(acc, fw_ref[...])


def _combine(dest_flat, gates, x1, fw, y):
    T, D = x1.shape
    tm = min(256, T)
    return pl.pallas_call(
        functools.partial(_combine_kernel, tm=tm),
        out_shape=jax.ShapeDtypeStruct((T, D), F32),
        grid_spec=pltpu.PrefetchScalarGridSpec(
            num_scalar_prefetch=1,
            grid=(T // tm,),
            in_specs=[
                pl.BlockSpec((tm, LANES), lambda i, d: (i, 0)),
                pl.BlockSpec((tm, D), lambda i, d: (i, 0)),
                pl.BlockSpec((1, D), lambda i, d: (0, 0)),
                pl.BlockSpec(memory_space=pl.ANY),
            ],
            out_specs=pl.BlockSpec((tm, D), lambda i, d: (i, 0)),
            scratch_shapes=[pltpu.VMEM((TOP_K, tm, D), F32), pltpu.SemaphoreType.DMA(())],
        ),
        compiler_params=_cparams(("arbitrary",)),
        name="combine",
    )(dest_flat, gates, x1, fw, y)


def _rope_angles(pos):
    inv = ROPE_THETA ** (-np.arange(0, ROPE_DIM, 2, dtype=np.float64) / ROPE_DIM)
    return pos.astype(np.float64)[:, None] * inv[None, :]


def _rope_tables128(S):
    ang = _rope_angles(np.arange(S))
    c, s = np.cos(ang), np.sin(ang)
    one = np.ones((S, 64 - ROPE_DIM))
    zero = np.zeros((S, 64 - ROPE_DIM))
    z8 = np.zeros((S, 8))
    cos64 = np.concatenate([c, c, one], axis=1)
    sina64 = np.concatenate([z8, s, zero], axis=1)
    sinb64 = np.concatenate([-s, z8, zero], axis=1)
    t2 = lambda t: jnp.asarray(np.concatenate([t, t], axis=1), F32)
    return t2(cos64), t2(sina64), t2(sinb64)


def _rope_tables_cmp(ncp):
    ang = _rope_angles(np.arange(ncp) * CMP_STRIDE + CMP_BLOCK - 1)
    c, s = np.cos(ang), np.sin(ang)
    cos64 = np.concatenate([c, c, np.ones((ncp, 64 - ROPE_DIM))], axis=1)
    sin64 = np.concatenate([s, s, np.zeros((ncp, 64 - ROPE_DIM))], axis=1)
    rot = np.zeros((64, 64), np.float32)
    for d in range(8):
        rot[d + 8, d] = -1.0
        rot[d, d + 8] = 1.0
    return jnp.asarray(cos64, F32), jnp.asarray(sin64, F32), jnp.asarray(rot, BF16)


def _overlap_matrix_t(S):
    ncp = S // CMP_STRIDE
    n_c = (S - CMP_BLOCK) // CMP_STRIDE + 1
    ovl = np.zeros((LANES, ncp), np.float32)
    for c in range(n_c):
        for l in range(CMP_BLOCK):
            ovl[(c * CMP_STRIDE + l) // SEL_BLOCK, c] += 1.0
    return jnp.asarray(ovl, BF16)


def _pack_segments():
    widths = [SSM_D_INNER, SSM_CONV_DIM, SSM_HEADS, NSA_D] + [KV_D] * 6 + [3 * NSA_HEADS]
    cuts = [int(c) for c in np.cumsum([0] + widths)]
    return ((COL_Q, cuts[3], NSA_D), (COL_Z, cuts[0], SSM_D_INNER), (COL_XBC, cuts[1], SSM_CONV_DIM),
            (COL_KV, cuts[4], 6 * KV_D), (COL_DT, cuts[2], SSM_HEADS), (COL_GATE, cuts[10], 3 * NSA_HEADS))


def _pack_w_in_kernel(w_ref, o_ref):
    cols = w_ref.shape[1]
    covered = np.zeros(PROJ_W, bool)
    for dst, src, width in _pack_segments():
        covered[dst:dst + width] = True
        for c in range(0, width, 512):
            n = min(512, width - c)
            o_ref[dst + c:dst + c + n, :] = w_ref[src + c:src + c + n, :].astype(BF16)
    row = 0
    while row < PROJ_W:
        if covered[row]:
            row += 1
            continue
        end = row
        while end < PROJ_W and not covered[end]:
            end += 1
        o_ref[row:end, :] = jnp.zeros((end - row, cols), BF16)
        row = end


def _pack_w_in(w_in):
    wt = jnp.swapaxes(w_in, 1, 2)
    _, W, D = wt.shape
    tc = 512
    return pl.pallas_call(
        _pack_w_in_kernel,
        out_shape=jax.ShapeDtypeStruct((PROJ_W, D), BF16),
        grid=(D // tc,),
        in_specs=[pl.BlockSpec((None, W, tc), lambda i: (0, 0, i))],
        out_specs=pl.BlockSpec((PROJ_W, tc), lambda i: (0, i)),
        compiler_params=_cparams(("parallel",)),
        name="pack_w_in",
    )(wt)


def _sched_columns(tab, n_blocks):
    per_tile = lambda c: tab[:n_blocks, c]
    once = lambda c: tab[0:1, c]
    return (per_tile(0), once(1), per_tile(2), per_tile(3), per_tile(4), per_tile(5), once(6), per_tile(7))


def kernel(x, attn_norm_w, w_in, conv_w, conv_b, dt_bias, a_log, d_skip, ssm_norm_w, cmp_pos_emb, cmp_w1,
           cmp_b1, cmp_w2, cmp_b2, w_out, moe_norm_w, w_router, b_router, w_gate_up, b_gate_up, w_down,
           b_down, final_norm_w):
    B, S, D = x.shape
    T = B * S
    x2d = x.reshape(T, D)
    row = lambda v: v.reshape(1, -1)
    padl = lambda v: jnp.pad(v.reshape(1, -1), ((0, 0), (0, LANES - v.size)))

    proj = _in_proj(x2d, row(attn_norm_w[0]), _pack_w_in(w_in))
    y_ssd = _ssd(proj, B, S, conv_w[0], row(conv_b[0]), padl(dt_bias[0]), padl(a_log[0]),
                 row(jnp.repeat(d_skip[0], SSM_HEAD_DIM)), row(ssm_norm_w[0]))
    cos, sina, sinb = _rope_tables128(S)
    q_rot, kse, vst, kw, vwt, sg = _nsa_prep(proj, B, S, cos, sina, sinb)
    cosc, sinc, rot = _rope_tables_cmp(S // CMP_STRIDE)
    kvc, kvct = _compress(proj, B, S, cmp_pos_emb[0], cmp_w1[0], cmp_b1[0][:, None, :], cmp_w2[0],
                          cmp_b2[0][:, None, :], cosc, sinc, rot)
    y_nsa = _nsa_attn(q_rot, kvc, kvct, kse, vst, kw, vwt, sg, _overlap_matrix_t(S), B, S)

    wr = jnp.pad(w_router[0], ((0, 0), (0, LANES - N_EXPERTS)))
    x1, h2, logits = _out_proj(y_ssd, y_nsa.reshape(T, NSA_D), x2d, w_out[0].astype(BF16),
                                row(moe_norm_w[0]), wr, padl(b_router[0]))

    n_pad = T * TOP_K + N_EXPERTS * MOE_TM
    n_blocks = n_pad // MOE_TM
    idx, rank, gates, cnt = _route(logits)
    dest, be_tab, pads = _route_fin(cnt, idx, rank, n_blocks)
    dest_flat = dest[:, :TOP_K].reshape(-1)
    sched = _sched_columns(be_tab, n_blocks)
    xs = _dispatch(dest_flat, pads, sched[1], h2, n_pad)
    hmid = _gate_up(sched, xs, w_gate_up[0], b_gate_up[0][:, None, :])
    y = _down(sched, hmid, w_down[0], b_down[0][:, None, :])
    out = _combine(dest_flat, gates, x1, row(final_norm_w), y)
    return out.reshape(B, S, D)
```

```python
import functools
import math

import jax
import jax.numpy as jnp
import numpy as np
from jax import lax
from jax.experimental import pallas as pl
from jax.experimental.pallas import tpu as pltpu

F32 = jnp.float32
BF16 = jnp.bfloat16
I32 = jnp.int32
U32 = jnp.uint32

NORM_EPS = 1e-5
SSM_D_INNER = 1024
SSM_HEAD_DIM = 64
SSM_HEADS = 16
SSM_GROUPS = 4
SSM_STATE = 128
SSM_CONV = 4
SSM_CHUNK = 128
SSM_CONV_DIM = 2048
NSA_HEADS = 16
NSA_KV_HEADS = 4
NSA_HEAD_DIM = 64
NSA_D = 1024
KV_D = 256
CMP_BLOCK = 32
CMP_STRIDE = 16
CMP_HIDDEN = 512
SEL_BLOCK = 64
N_SELECT = 16
WINDOW = 512
ROPE_THETA = 500000.0
ROPE_DIM = 16
SEL_FORCE_SCORE = 1.0e4
N_EXPERTS = 32
TOP_K = 4
D_FF = 2048
SWIGLU_LIMIT = 7.0
SWIGLU_ALPHA = 1.702

LANES = 128
SUBLANES = 8
VMEM_LIMIT = 56 * 1024 * 1024

NEG = -1.0e30

COL_Q = 0
COL_Z = 1024
COL_XBC = 2048
COL_KV = 4096
COL_DT = 5632
COL_GATE = 5760
PROJ_W = 6144

MOE_TM = 256
Q_TILE = 256
SLC_TILE = 512
KE_W = LANES + 64
V_ROWS = 80


def _cparams(sem, vmem=VMEM_LIMIT):
    return pltpu.CompilerParams(dimension_semantics=sem, vmem_limit_bytes=vmem)


def _dot(a, b):
    return jnp.dot(a, b, preferred_element_type=F32)


def _dot_nt(a, b):
    return lax.dot_general(a, b, (((1,), (1,)), ((), ())), preferred_element_type=F32)


def _split3(b):
    b1 = b.astype(BF16)
    r1 = b - b1.astype(F32)
    b2 = r1.astype(BF16)
    r2 = r1 - b2.astype(F32)
    return b1, b2, r2.astype(BF16)


def _dot_exact_lhs(a_bf16, b):
    b1, b2, b3 = _split3(b)
    return _dot(a_bf16, b1) + _dot(a_bf16, b2) + _dot(a_bf16, b3)


def _dot_exact_rhs(a, b_bf16):
    a1, a2, a3 = _split3(a)
    return _dot(a1, b_bf16) + _dot(a2, b_bf16) + _dot(a3, b_bf16)


def _sigmoid(x):
    return 1.0 / (1.0 + jnp.exp(-x))


def _silu(x):
    return x * _sigmoid(x)


def _softplus(x):
    return jnp.maximum(x, 0.0) + jnp.log(1.0 + jnp.exp(-jnp.abs(x)))


def _gelu_tanh(x):
    c = math.sqrt(2.0 / math.pi)
    return 0.5 * x * (1.0 + jnp.tanh(c * (x + 0.044715 * (x * x * x))))


def _rms_rows(x, w):
    ms = jnp.mean(x * x, axis=-1, keepdims=True)
    return x * lax.rsqrt(ms + NORM_EPS) * w


def _in_proj_kernel(x_ref, nw_ref, w_ref, o_ref, hn_ref, *, tm):
    @pl.when(pl.program_id(1) == 0)
    def _():
        def body(c, carry):
            r = pl.multiple_of(c * 128, 128)
            hn_ref[pl.ds(r, 128), :] = _rms_rows(x_ref[pl.ds(r, 128), :], nw_ref[...]).astype(BF16)
            return carry
        lax.fori_loop(0, tm // 128, body, 0)

    o_ref[...] = _dot_nt(hn_ref[...], w_ref[...])


def _in_proj(x2d, nw, wpt):
    T, D = x2d.shape
    NP = wpt.shape[0]
    tm = min(1024, T)
    tn = 1024
    return pl.pallas_call(
        functools.partial(_in_proj_kernel, tm=tm),
        out_shape=jax.ShapeDtypeStruct((T, NP), F32),
        grid=(T // tm, NP // tn),
        in_specs=[
            pl.BlockSpec((tm, D), lambda i, j: (i, 0)),
            pl.BlockSpec((1, D), lambda i, j: (0, 0)),
            pl.BlockSpec((tn, D), lambda i, j: (j, 0)),
        ],
        out_specs=pl.BlockSpec((tm, tn), lambda i, j: (i, j)),
        scratch_shapes=[pltpu.VMEM((tm, D), BF16)],
        compiler_params=_cparams(("parallel", "arbitrary")),
        name="in_proj",
    )(x2d, nw, wpt)


def _rope128(x, cos, sina, sinb):
    return x * cos + pltpu.roll(x, 8, 1) * sina + pltpu.roll(x, 120, 1) * sinb


def _nsa_prep_kernel(q_ref, ks_ref, kw_ref, gl_ref, cos_ref, sina_ref, sinb_ref,
                     qo_ref, kso_ref, vso_ref, kwo_ref, vwo_ref, sg_ref):
    cos = cos_ref[...]
    sina = sina_ref[...]
    sinb = sinb_ref[...]
    scale = NSA_HEAD_DIM ** -0.5
    for c in range(NSA_D // LANES):
        xq = q_ref[:, c * LANES:(c + 1) * LANES]
        qo_ref[:, c * LANES:(c + 1) * LANES] = (_rope128(xq, cos, sina, sinb) * scale).astype(BF16)
    ts = q_ref.shape[0]
    key = pl.program_id(1) * ts + lax.broadcasted_iota(I32, (ts, LANES), 0)
    blk1h = jnp.where(lax.shift_right_logical(key, 6) == lax.broadcasted_iota(I32, (ts, LANES), 1), 1.0, 0.0)
    ones = jnp.ones((V_ROWS - 64, LANES), BF16)
    for src, ko, vo, ext in ((ks_ref, kso_ref, vso_ref, True), (kw_ref, kwo_ref, vwo_ref, False)):
        for c in range(KV_D // LANES):
            kr = _rope128(src[:, c * LANES:(c + 1) * LANES], cos, sina, sinb)
            vt = src[:, KV_D + c * LANES:KV_D + (c + 1) * LANES].T
            for half in range(2):
                g = 2 * c + half
                kg = kr[:, half * 64:(half + 1) * 64].astype(BF16)
                if ext:
                    ko[g, :, 0:LANES] = blk1h.astype(BF16)
                    ko[g, :, LANES:LANES + 64] = kg
                else:
                    ko[g] = kg
                for j in range(ts // LANES):
                    vo[g, j, 0:64, :] = vt[half * 64:(half + 1) * 64, j * LANES:(j + 1) * LANES].astype(BF16)
                    vo[g, j, 64:V_ROWS, :] = ones
    sg_ref[...] = _sigmoid(gl_ref[...])


def _nsa_prep(proj, B, S, cos, sina, sinb):
    ts = min(512, S)
    nst = S // ts
    G = NSA_KV_HEADS
    kv_spec = pl.BlockSpec((None, G, ts, 64), lambda b, s: (b, 0, s, 0))
    kv_shape = jax.ShapeDtypeStruct((B, G, S, 64), BF16)
    ke_spec = pl.BlockSpec((None, G, ts, KE_W), lambda b, s: (b, 0, s, 0))
    ke_shape = jax.ShapeDtypeStruct((B, G, S, KE_W), BF16)
    vt_spec = pl.BlockSpec((None, G, ts // LANES, V_ROWS, LANES), lambda b, s: (b, 0, s, 0, 0))
    vt_shape = jax.ShapeDtypeStruct((B, G, S // LANES, V_ROWS, LANES), BF16)
    return pl.pallas_call(
        _nsa_prep_kernel,
        out_shape=(
            jax.ShapeDtypeStruct((B, S, NSA_D), BF16),
            ke_shape, vt_shape, kv_shape, vt_shape,
            jax.ShapeDtypeStruct((B, S, LANES), F32),
        ),
        grid=(B, nst),
        in_specs=[
            pl.BlockSpec((ts, NSA_D), lambda b, s: (b * nst + s, COL_Q // NSA_D)),
            pl.BlockSpec((ts, 512), lambda b, s: (b * nst + s, (COL_KV + 512) // 512)),
            pl.BlockSpec((ts, 512), lambda b, s: (b * nst + s, (COL_KV + 1024) // 512)),
            pl.BlockSpec((ts, LANES), lambda b, s: (b * nst + s, COL_GATE // LANES)),
            pl.BlockSpec((ts, LANES), lambda b, s: (s, 0)),
            pl.BlockSpec((ts, LANES), lambda b, s: (s, 0)),
            pl.BlockSpec((ts, LANES), lambda b, s: (s, 0)),
        ],
        out_specs=(
            pl.BlockSpec((None, ts, NSA_D), lambda b, s: (b, s, 0)),
            ke_spec, vt_spec, kv_spec, vt_spec,
            pl.BlockSpec((None, ts, LANES), lambda b, s: (b, s, 0)),
        ),
        compiler_params=_cparams(("parallel", "parallel")),
        name="nsa_prep",
    )(proj, proj, proj, proj, cos, sina, sinb)


def _ssd_kernel(z_ref, xbc_ref, dt_ref, cw_ref, cb_ref, dtb_ref, alog_ref, dsk_ref, nw_ref,
                y_ref, buf, st):
    L = SSM_CHUNK
    c = pl.program_id(1)

    @pl.when(c == 0)
    def _():
        buf[...] = jnp.zeros(buf.shape, F32)
        st[...] = jnp.zeros(st.shape, F32)

    cur = xbc_ref[...]
    tail = buf[...]
    row8 = lax.broadcasted_iota(I32, (SUBLANES, SSM_CONV_DIM), 0)
    acc = cb_ref[...] + cw_ref[SSM_CONV - 1:SSM_CONV, :] * cur
    for s in range(1, SSM_CONV):
        rolled = pltpu.roll(cur, s, 0)
        head = jnp.where(row8 < s, pltpu.roll(tail, s, 0), rolled[0:SUBLANES, :])
        shifted = jnp.concatenate([head, rolled[SUBLANES:, :]], axis=0)
        acc = acc + cw_ref[SSM_CONV - 1 - s:SSM_CONV - s, :] * shifted
    xc = _silu(acc)
    buf[...] = xbc_ref[L - SUBLANES:L, :]

    lane = lax.broadcasted_iota(I32, (L, LANES), 1)
    row = lax.broadcasted_iota(I32, (L, LANES), 0)
    lo = lane < 64
    dtv = jnp.where(lane < SSM_HEADS, _softplus(dt_ref[...] + dtb_ref[...]), 0.0)
    a = -jnp.exp(alog_ref[...])
    tri = jnp.where(row >= lane, 1.0, 0.0).astype(BF16)
    acs = _dot_exact_lhs(tri, dtv * a)
    acs_t = acs.T
    causal = row >= lane

    ys = []
    for g in range(SSM_GROUPS):
        bg = xc[:, SSM_D_INNER + g * SSM_STATE:SSM_D_INNER + (g + 1) * SSM_STATE]
        cg = xc[:, SSM_D_INNER + (SSM_GROUPS + g) * SSM_STATE:SSM_D_INNER + (SSM_GROUPS + g + 1) * SSM_STATE]
        bgt = bg.T.astype(BF16)
        cgb = cg.astype(BF16)
        gmat = _dot(cgb, bgt)
        for p in (2 * g, 2 * g + 1):
            h0, h1 = 2 * p, 2 * p + 1
            xs_pair = xc[:, p * LANES:(p + 1) * LANES]
            col0 = acs[:, h0:h0 + 1]
            col1 = acs[:, h1:h1 + 1]
            colp = jnp.where(lo, col0, col1)
            last = jnp.where(lo[0:1, :], acs[L - 1:L, h0:h0 + 1], acs[L - 1:L, h1:h1 + 1])
            x = xs_pair * jnp.where(lo, dtv[:, h0:h0 + 1], dtv[:, h1:h1 + 1])
            m0 = (gmat * jnp.where(causal, jnp.exp(col0 - acs_t[h0:h0 + 1, :]), 0.0)).astype(BF16)
            m1 = (gmat * jnp.where(causal, jnp.exp(col1 - acs_t[h1:h1 + 1, :]), 0.0)).astype(BF16)
            y_diag = _dot(m0, jnp.where(lo, x, 0.0).astype(BF16)) + _dot(m1, jnp.where(lo, 0.0, x).astype(BF16))
            s_prev = st[p]
            y_off = _dot(cgb, s_prev.astype(BF16)) * jnp.exp(colp)
            w = (x * jnp.exp(last - colp)).astype(BF16)
            st[p] = jnp.exp(last) * s_prev + _dot(bgt, w)
            ys.append(y_diag + y_off + dsk_ref[:, p * LANES:(p + 1) * LANES] * xs_pair)
    y = jnp.concatenate(ys, axis=1)
    gte = y * _silu(z_ref[...])
    gw = SSM_D_INNER // SSM_GROUPS
    outs = []
    for k in range(SSM_GROUPS):
        gk = gte[:, k * gw:(k + 1) * gw]
        ms = jnp.mean(gk * gk, axis=-1, keepdims=True)
        outs.append(gk * lax.rsqrt(ms + NORM_EPS))
    y_ref[...] = (jnp.concatenate(outs, axis=1) * nw_ref[...]).astype(BF16)


def _ssd(proj, B, S, conv_w, conv_b, dtb, alog, dskip, nw):
    L = SSM_CHUNK
    nc = S // L
    small = lambda shp: pl.BlockSpec(shp, lambda b, c: (0, 0))
    return pl.pallas_call(
        _ssd_kernel,
        out_shape=jax.ShapeDtypeStruct((B * S, SSM_D_INNER), BF16),
        grid=(B, nc),
        in_specs=[
            pl.BlockSpec((L, SSM_D_INNER), lambda b, c: (b * nc + c, COL_Z // SSM_D_INNER)),
            pl.BlockSpec((L, SSM_CONV_DIM), lambda b, c: (b * nc + c, COL_XBC // SSM_CONV_DIM)),
            pl.BlockSpec((L, LANES), lambda b, c: (b * nc + c, COL_DT // LANES)),
            small((SSM_CONV, SSM_CONV_DIM)),
            small((1, SSM_CONV_DIM)),
            small((1, LANES)),
            small((1, LANES)),
            small((1, SSM_D_INNER)),
            small((1, SSM_D_INNER)),
        ],
        out_specs=pl.BlockSpec((L, SSM_D_INNER), lambda b, c: (b * nc + c, 0)),
        scratch_shapes=[
            pltpu.VMEM((SUBLANES, SSM_CONV_DIM), F32),
            pltpu.VMEM((SSM_HEADS // 2, SSM_STATE, LANES), F32),
        ],
        compiler_params=_cparams(("parallel", "arbitrary")),
        name="ssd",
    )(proj, proj, proj, conv_w, conv_b, dtb, alog, dskip, nw)


def _compress_kernel(xa_ref, xb_ref, pos_ref, w1_ref, b1_ref, w2_ref, b2_ref, cos_ref, sin_ref, rot_ref,
                     o_ref, ot_ref, w1b, feat, *, ncp):
    kv = pl.program_id(1)
    outs = []
    half = CMP_STRIDE
    hw = half * 64
    w1b[...] = w1_ref[...].astype(BF16)
    w2 = w2_ref[...].astype(BF16)
    G = NSA_KV_HEADS
    is_k = kv == 0
    for g in range(G):
        x_ref = (xa_ref, xb_ref)[g // 2]
        for l in range(half):
            xg = x_ref[pl.ds(l, ncp, stride=half), :][:, (g % 2) * 64:(g % 2 + 1) * 64]
            feat[:, l * 64:(l + 1) * 64] = xg + pos_ref[l:l + 1, :]
            feat[:, hw + l * 64:hw + (l + 1) * 64] = xg + pos_ref[half + l:half + l + 1, :]
        first = _dot(feat[:, 0:hw].astype(BF16), w1b[0:hw, :])
        second = _dot(feat[:, hw:2 * hw].astype(BF16), w1b[hw:2 * hw, :])
        h = first + pltpu.roll(second, ncp - 1, 0) + b1_ref[...]
        h = _gelu_tanh(h)
        o = _dot(h.astype(BF16), w2) + b2_ref[...]
        roped = o * cos_ref[...] + _dot_exact_rhs(o, rot_ref[...]) * sin_ref[...]
        outs.append(jnp.where(is_k, roped, o))
        o_ref[g] = outs[g]
    for c in range(G // 2):
        t = jnp.concatenate([outs[2 * c], outs[2 * c + 1]], axis=1).T
        ot_ref[2 * c] = t[0:64, :]
        ot_ref[2 * c + 1] = t[64:128, :]


def _compress(proj, B, S, pos, w1, b1, w2, b2, cosc, sinc, rot):
    ncp = S // CMP_STRIDE
    G = NSA_KV_HEADS
    kvblk = COL_KV // KV_D
    return pl.pallas_call(
        functools.partial(_compress_kernel, ncp=ncp),
        out_shape=(jax.ShapeDtypeStruct((B, 2, G, ncp, 64), F32),
                   jax.ShapeDtypeStruct((B, 2, G, 64, ncp), F32)),
        grid=(B, 2),
        in_specs=[
            pl.BlockSpec((S, LANES), lambda b, k: (b, 2 * (kvblk + k))),
            pl.BlockSpec((S, LANES), lambda b, k: (b, 2 * (kvblk + k) + 1)),
            pl.BlockSpec((None, CMP_BLOCK, 64), lambda b, k: (k, 0, 0)),
            pl.BlockSpec((None, CMP_BLOCK * 64, CMP_HIDDEN), lambda b, k: (k, 0, 0)),
            pl.BlockSpec((None, 1, CMP_HIDDEN), lambda b, k: (k, 0, 0)),
            pl.BlockSpec((None, CMP_HIDDEN, 64), lambda b, k: (k, 0, 0)),
            pl.BlockSpec((None, 1, 64), lambda b, k: (k, 0, 0)),
            pl.BlockSpec((ncp, 64), lambda b, k: (0, 0)),
            pl.BlockSpec((ncp, 64), lambda b, k: (0, 0)),
            pl.BlockSpec((64, 64), lambda b, k: (0, 0)),
        ],
        out_specs=(pl.BlockSpec((None, None, G, ncp, 64), lambda b, k: (b, k, 0, 0, 0)),
                   pl.BlockSpec((None, None, G, 64, ncp), lambda b, k: (b, k, 0, 0, 0))),
        scratch_shapes=[pltpu.VMEM((CMP_BLOCK * 64, CMP_HIDDEN), BF16),
                        pltpu.VMEM((ncp, CMP_BLOCK * 64), F32)],
        compiler_params=_cparams(("parallel", "parallel")),
        name="compress",
    )(proj, proj, pos, w1, b1, w2, b2, cosc, sinc, rot)


def _nsa_attn_kernel(q_ref, kc_ref, vct_ref, ks_ref, vst_ref, kw_ref, vwt_ref, sg_ref, ovlt_ref,
                     o_ref, *, ncp, n_c, n_sel, k_sel):
    g = pl.program_id(1)
    qi = pl.program_id(2)
    tq = Q_TILE
    R = NSA_HEADS // NSA_KV_HEADS
    t0 = qi * tq
    qt = q_ref[...].astype(F32).T
    q4t = jnp.concatenate([qt[r * 64:(r + 1) * 64, :] for r in range(R)], axis=1).astype(BF16)
    rep = lambda a: jnp.concatenate([a] * R, axis=1)
    tpos = t0 + lax.broadcasted_iota(I32, (1, tq), 1)
    cmax = lambda a: jnp.max(a, axis=0, keepdims=True)
    csum = lambda a: jnp.sum(a, axis=0, keepdims=True)
    jl = lax.broadcasted_iota(I32, (LANES, tq), 0)

    def online(carry, s, vt):
        m_old, acc = carry
        m_new = jnp.maximum(m_old, cmax(s))
        alpha = jnp.exp(m_old - m_new)
        return m_new, alpha * acc + _dot(vt, jnp.exp(s - m_new).astype(BF16))

    sgt = sg_ref[...].T
    gts = [[csum(jnp.where(jl == 3 * (g * R + r) + k, sgt, 0.0)) for k in range(3)] for r in range(R)]

    span = min(WINDOW + tq, kw_ref.shape[0])
    w0 = pl.multiple_of(jnp.maximum(t0 + tq - span, 0), tq)
    jw = lax.shift_right_logical(w0, int(math.log2(LANES)))
    vt = jnp.concatenate([vwt_ref[jw + c] for c in range(span // LANES)], axis=1)
    dist = tpos - (w0 + lax.broadcasted_iota(I32, (span, tq), 0))
    s = _dot(kw_ref[pl.ds(w0, span), :], q4t) + rep(jnp.where((dist >= 0) & (dist < WINDOW), 0.0, NEG))
    acc_win = _dot(vt, jnp.exp(s - cmax(s)).astype(BF16))
    o_win = acc_win[0:64] / acc_win[64:65]

    cidx = lax.broadcasted_iota(I32, (ncp, tq), 0)
    maskc = rep(jnp.where((cidx * CMP_STRIDE + (CMP_BLOCK - 1) <= tpos) & (cidx < n_c), 1.0, 0.0)) > 0.5
    s = _dot(kc_ref[...].astype(BF16), q4t)
    e = jnp.where(maskc, jnp.exp(s - cmax(jnp.where(maskc, s, NEG))), 0.0)
    p = e * (1.0 / jnp.maximum(csum(e), 1e-30))
    o_cmp = _dot(vct_ref[...].astype(BF16), p.astype(BF16))
    psum = p[:, 0:tq]
    for r in range(1, R):
        psum = psum + p[:, r * tq:(r + 1) * tq]

    p_hi = psum.astype(BF16)
    p_lo = (psum - p_hi.astype(F32)).astype(BF16)
    imp = _dot(ovlt_ref[...], p_hi) + _dot(ovlt_ref[...], p_lo)
    cur = lax.shift_right_logical(tpos, 6)
    forced = (jl == 0) | (jl == cur) | (jl == cur - 1)
    st = jnp.where(forced, SEL_FORCE_SCORE, jnp.where(jl <= cur, imp, -SEL_FORCE_SCORE))
    nsp = max(n_sel, SUBLANES)
    sel_rows = []
    for v in range(nsp // SUBLANES):
        gv = st[v * SUBLANES:(v + 1) * SUBLANES, :]
        jrow = v * SUBLANES + lax.broadcasted_iota(I32, (SUBLANES, tq), 0)
        cnt = jnp.zeros((SUBLANES, tq), F32)
        for i in range(n_sel):
            ri = st[i:i + 1, :]
            if v * SUBLANES + SUBLANES - 1 < i:
                beats = ri > gv
            elif v * SUBLANES > i:
                beats = ri >= gv
            else:
                beats = ((ri >= gv) & (jrow > i)) | ((ri > gv) & (jrow < i))
            cnt = cnt + jnp.where(beats, 1.0, 0.0)
        sel_rows.append(jnp.where((cnt < float(k_sel)) & (jrow < n_sel), 1.0, 0.0))
    sel_rows.append(jnp.zeros((LANES - nsp, tq), F32))
    sel = jnp.concatenate(sel_rows, axis=0)

    tk = min(SLC_TILE, ks_ref.shape[0])
    nsub = tk // LANES
    picked = sel > 0.5
    before = jl < lax.shift_right_logical(t0, 6)
    qext_diag = jnp.concatenate([rep(jnp.where(picked, 0.0, NEG)).astype(BF16), q4t], axis=0)
    qext = jnp.concatenate([rep(jnp.where(before & picked, 0.0, NEG)).astype(BF16), q4t], axis=0)
    r_i = lax.broadcasted_iota(I32, (tq, tq), 0)
    c_i = lax.broadcasted_iota(I32, (tq, tq), 1)
    s = _dot(ks_ref[pl.ds(pl.multiple_of(t0, tq), tq), :], qext_diag) + rep(jnp.where(r_i <= c_i, 0.0, NEG))
    nd = tq // LANES
    vd = jnp.concatenate([vst_ref[qi * nd + c] for c in range(nd)], axis=1)
    m_diag, acc_diag = online((jnp.full((1, R * tq), NEG, F32), jnp.zeros((V_ROWS, R * tq), F32)), s, vd)
    n_tiles = ks_ref.shape[0] // tk

    def scores(j):
        jj = jnp.minimum(j, n_tiles - 1)
        return _dot(ks_ref[pl.ds(pl.multiple_of(jj * tk, tk), tk), :], qext)

    def slc_body(j, carry):
        vt = jnp.concatenate([vst_ref[j * nsub + c] for c in range(nsub)], axis=1)
        return online(carry, scores(j), vt)

    n_main = lax.shift_right_logical(t0 + (tk - 1), int(math.log2(tk)))
    _, acc_slc = lax.fori_loop(0, n_main, slc_body, (m_diag, acc_diag))
    o_slc = acc_slc[0:64] / acc_slc[64:65]

    outs = []
    for r in range(R):
        cols = slice(r * tq, (r + 1) * tq)
        outs.append(gts[r][0] * o_cmp[:, cols] + gts[r][1] * o_slc[:, cols] + gts[r][2] * o_win[:, cols])
    for c in range(R // 2):
        o_ref[:, c * LANES:(c + 1) * LANES] = jnp.concatenate(outs[2 * c:2 * c + 2], axis=0).T.astype(BF16)


def _nsa_attn(q_rot, kvc, kvct, kse, vst, kw, vwt, sg, ovlt, B, S):
    G = NSA_KV_HEADS
    nq = S // Q_TILE
    ncp = S // CMP_STRIDE
    n_c = (S - CMP_BLOCK) // CMP_STRIDE + 1
    n_sel = S // SEL_BLOCK
    k_sel = min(N_SELECT, n_sel)
    keys = pl.BlockSpec((None, None, S, 64), lambda b, g, i: (b, g, 0, 0))
    keys_ext = pl.BlockSpec((None, None, S, KE_W), lambda b, g, i: (b, g, 0, 0))
    vals = pl.BlockSpec((None, None, S // LANES, V_ROWS, LANES), lambda b, g, i: (b, g, 0, 0, 0))
    return pl.pallas_call(
        functools.partial(_nsa_attn_kernel, ncp=ncp, n_c=n_c, n_sel=n_sel, k_sel=k_sel),
        out_shape=jax.ShapeDtypeStruct((B, S, NSA_D), BF16),
        grid=(B, G, nq),
        in_specs=[
            pl.BlockSpec((None, Q_TILE, KV_D), lambda b, g, i: (b, i, g)),
            pl.BlockSpec((None, None, None, ncp, 64), lambda b, g, i: (b, 0, g, 0, 0)),
            pl.BlockSpec((None, None, None, 64, ncp), lambda b, g, i: (b, 1, g, 0, 0)),
            keys_ext, vals, keys, vals,
            pl.BlockSpec((None, Q_TILE, LANES), lambda b, g, i: (b, i, 0)),
            pl.BlockSpec(ovlt.shape, lambda b, g, i: (0, 0)),
        ],
        out_specs=pl.BlockSpec((None, Q_TILE, KV_D), lambda b, g, i: (b, i, g)),
        compiler_params=_cparams(("parallel", "parallel", "arbitrary")),
        name="nsa_attn",
    )(q_rot, kvc, kvct, kse, vst, kw, vwt, sg, ovlt)


def _out_proj_kernel(ys_ref, yn_ref, x_ref, w_ref, nw_ref, wr_ref, br_ref, x1_ref, h2_ref, lg_ref):
    half = ys_ref.shape[1]
    y = _dot(ys_ref[...], w_ref[0:half, :]) + _dot(yn_ref[...], w_ref[half:2 * half, :])
    x1 = x_ref[...] + y
    x1_ref[...] = x1
    hn = _rms_rows(x1, nw_ref[...])
    h2_ref[...] = hn
    tm = hn.shape[0]
    h1, h2, _ = _split3(hn)
    w1, w2, _ = _split3(wr_ref[...])
    r = _dot(jnp.concatenate([h1, h2], axis=0), jnp.concatenate([w1, w2], axis=1))
    lg_ref[...] = r[0:tm, 0:LANES] + r[0:tm, LANES:2 * LANES] + r[tm:2 * tm, 0:LANES] + br_ref[...]


def _out_proj(y_ssd, y_nsa, x2d, w_out_b, nw, wr, br):
    T, D = x2d.shape
    tm = min(512, T)
    half = y_ssd.shape[1]
    return pl.pallas_call(
        _out_proj_kernel,
        out_shape=(
            jax.ShapeDtypeStruct((T, D), F32),
            jax.ShapeDtypeStruct((T, D), F32),
            jax.ShapeDtypeStruct((T, LANES), F32),
        ),
        grid=(T // tm,),
        in_specs=[
            pl.BlockSpec((tm, half), lambda i: (i, 0)),
            pl.BlockSpec((tm, half), lambda i: (i, 0)),
            pl.BlockSpec((tm, D), lambda i: (i, 0)),
            pl.BlockSpec((D, D), lambda i: (0, 0)),
            pl.BlockSpec((1, D), lambda i: (0, 0)),
            pl.BlockSpec((D, LANES), lambda i: (0, 0)),
            pl.BlockSpec((1, LANES), lambda i: (0, 0)),
        ],
        out_specs=(
            pl.BlockSpec((tm, D), lambda i: (i, 0)),
            pl.BlockSpec((tm, D), lambda i: (i, 0)),
            pl.BlockSpec((tm, LANES), lambda i: (i, 0)),
        ),
        compiler_params=_cparams(("parallel",)),
        name="out_proj",
    )(y_ssd, y_nsa, x2d, w_out_b, nw, wr, br)


def _route_kernel(lg_ref, idx_ref, rank_ref, gate_ref, cnt_ref, carry, *, tr):
    i = pl.program_id(0)

    @pl.when(i == 0)
    def _():
        carry[...] = jnp.zeros(carry.shape, F32)

    lane = lax.broadcasted_iota(I32, (tr, LANES), 1)
    lanef = lane.astype(F32)
    l = jnp.where(lane < N_EXPERTS, lg_ref[...], NEG)
    ohs, vals, idxs = [], [], []
    for _ in range(TOP_K):
        m = jnp.max(l, axis=-1, keepdims=True)
        idx = jnp.min(jnp.where(l == m, lanef, float(LANES)), axis=-1, keepdims=True)
        oh = lanef == idx
        l = jnp.where(oh, 2.0 * NEG, l)
        ohs.append(oh)
        vals.append(m)
        idxs.append(idx)
    es = [jnp.exp(v - vals[0]) for v in vals]
    den = es[0] + es[1] + es[2] + es[3]
    oh_sum = jnp.zeros((tr, LANES), F32)
    for oh in ohs:
        oh_sum = oh_sum + jnp.where(oh, 1.0, 0.0)
    r_i = lax.broadcasted_iota(I32, (tr, tr), 0)
    c_i = lax.broadcasted_iota(I32, (tr, tr), 1)
    strict = jnp.where(r_i > c_i, 1.0, 0.0).astype(BF16)
    base = carry[0:1, :] + _dot(strict, oh_sum.astype(BF16))
    idx_o = jnp.zeros((tr, LANES), F32)
    rank_o = jnp.zeros((tr, LANES), F32)
    gate_o = jnp.zeros((tr, LANES), F32)
    for k in range(TOP_K):
        rk = jnp.sum(jnp.where(ohs[k], base, 0.0), axis=-1, keepdims=True)
        idx_o = jnp.where(lane == k, idxs[k], idx_o)
        rank_o = jnp.where(lane == k, rk, rank_o)
        gate_o = jnp.where(lane == k, es[k] / den, gate_o)
    idx_ref[...] = idx_o.astype(I32)
    rank_ref[...] = rank_o.astype(I32)
    gate_ref[...] = gate_o
    carry[...] = carry[...] + jnp.sum(oh_sum, axis=0, keepdims=True)
    cnt_ref[...] = carry[...]


def _route(logits):
    T = logits.shape[0]
    tr = min(512, T)
    tok = pl.BlockSpec((tr, LANES), lambda i: (i, 0))
    return pl.pallas_call(
        functools.partial(_route_kernel, tr=tr),
        out_shape=(
            jax.ShapeDtypeStruct((T, LANES), I32),
            jax.ShapeDtypeStruct((T, LANES), I32),
            jax.ShapeDtypeStruct((T, LANES), F32),
            jax.ShapeDtypeStruct((SUBLANES, LANES), F32),
        ),
        grid=(T // tr,),
        in_specs=[tok],
        out_specs=(tok, tok, tok, pl.BlockSpec((SUBLANES, LANES), lambda i: (0, 0))),
        scratch_shapes=[pltpu.VMEM((SUBLANES, LANES), F32)],
        compiler_params=_cparams(("arbitrary",)),
        name="route",
    )(logits)


def _route_fin_kernel(cnt_ref, idx_ref, rank_ref, dest_ref, be_ref, pads_ref, *, tr, nbp, tm_shift):
    lane = lax.broadcasted_iota(I32, (SUBLANES, LANES), 1)
    cnt = cnt_ref[...].astype(I32)
    tm = 1 << tm_shift
    nblk = jnp.where(lane < N_EXPERTS, lax.shift_right_logical(cnt + (tm - 1), tm_shift), 0)
    r_i = lax.broadcasted_iota(I32, (LANES, LANES), 0)
    c_i = lax.broadcasted_iota(I32, (LANES, LANES), 1)
    upper = jnp.where(r_i <= c_i, 1.0, 0.0).astype(BF16)
    end_blk = _dot(nblk.astype(F32).astype(BF16), upper)
    start_row = (end_blk - nblk.astype(F32)) * float(tm)
    idx = idx_ref[...].astype(F32)
    lane_t = lax.broadcasted_iota(I32, (tr, LANES), 1)
    lane_f = lane_t.astype(F32)
    dest = jnp.zeros((tr, LANES), F32)
    for k in range(TOP_K):
        e_k = jnp.sum(jnp.where(lane_t == k, idx, 0.0), axis=-1, keepdims=True)
        s_k = jnp.sum(jnp.where(lane_f == e_k, start_row[0:1, :], 0.0), axis=-1, keepdims=True)
        dest = jnp.where(lane_t == k, s_k, dest)
    dest_ref[...] = dest.astype(I32) + jnp.where(lane_t < TOP_K, rank_ref[...], 0)
    blk = lax.broadcasted_iota(I32, (nbp, LANES), 0).astype(F32)
    lane_b = lax.broadcasted_iota(I32, (nbp, LANES), 1)
    lane_bf = lane_b.astype(F32)
    rsum = lambda v: jnp.sum(v, axis=-1, keepdims=True)
    end_row = end_blk[0:1, :]
    nblk_row = nblk.astype(F32)[0:1, :]
    is_exp = lane_b < N_EXPERTS
    nonempty = is_exp & (nblk_row > 0.0)
    be = jnp.minimum(rsum(jnp.where(is_exp & (end_row <= blk), 1.0, 0.0)), float(N_EXPERTS - 1))
    n_used = rsum(jnp.where(lane_b == N_EXPERTS - 1, end_row, 0.0))
    start_of = rsum(jnp.where(lane_bf == be, end_row - nblk_row, 0.0))
    first = jnp.where((start_of == blk[:, 0:1]) & (blk[:, 0:1] < n_used), 1.0, 0.0)
    run = rsum(jnp.where(nonempty & (lane_bf < be), 1.0, 0.0))
    none = float(LANES)
    nxt = jnp.min(jnp.where(nonempty & (lane_bf > be), lane_bf, none), axis=-1, keepdims=True)
    first_e = jnp.min(jnp.where(nonempty, lane_bf, none), axis=-1, keepdims=True)
    last = jnp.where(nxt == none, 1.0, 0.0)
    nxt = jnp.where(nxt == none, first_e, nxt)
    n_runs = rsum(jnp.where(nonempty, 1.0, 0.0))
    cnt_of = rsum(jnp.where(lane_bf == be, cnt.astype(F32)[0:1, :], 0.0))
    vrows = jnp.clip(cnt_of - (blk[:, 0:1] - start_of) * float(tm), 0.0, float(tm))
    tab = jnp.zeros((nbp, LANES), F32)
    for c, v in enumerate((be, n_used, first, run, nxt, last, n_runs, vrows)):
        tab = jnp.where(lane_b == c, v, tab)
    be_ref[...] = tab.astype(I32)
    sub = lax.broadcasted_iota(I32, (SUBLANES, LANES), 0)
    pad_start = start_row.astype(I32) + cnt
    pad_len = jnp.where(lane < N_EXPERTS, nblk * tm - cnt, 0)
    pads_ref[...] = jnp.where(sub == 0, pad_start, jnp.where(sub == 1, pad_len, 0))


def _route_fin(cnt, idx, rank, n_blocks):
    T = idx.shape[0]
    tr = min(512, T)
    nbp = ((n_blocks + SUBLANES - 1) // SUBLANES) * SUBLANES
    tok = pl.BlockSpec((tr, LANES), lambda i: (i, 0))
    return pl.pallas_call(
        functools.partial(_route_fin_kernel, tr=tr, nbp=nbp, tm_shift=int(math.log2(MOE_TM))),
        out_shape=(
            jax.ShapeDtypeStruct((T, LANES), I32),
            jax.ShapeDtypeStruct((nbp, LANES), I32),
            jax.ShapeDtypeStruct((SUBLANES, LANES), I32),
        ),
        grid=(T // tr,),
        in_specs=[pl.BlockSpec((SUBLANES, LANES), lambda i: (0, 0)), tok, tok],
        out_specs=(tok, pl.BlockSpec((nbp, LANES), lambda i: (0, 0)),
                   pl.BlockSpec((SUBLANES, LANES), lambda i: (0, 0))),
        compiler_params=_cparams(("arbitrary",)),
        name="route_fin",
    )(cnt, idx, rank)


def _row_copy(src_ref, src_row, dst_ref, dst_row, sem):
    return pltpu.make_async_copy(src_ref.at[pl.ds(src_row, 1)], dst_ref.at[pl.ds(dst_row, 1)], sem)


def _dispatch_kernel(dest_ref, pstart_ref, plen_ref, nu_ref, h2_ref, xs_out, zeros, sem, *, tm, nb):
    base = pl.program_id(0) * (tm * TOP_K)

    @pl.when(pl.program_id(0) == 0)
    def _():
        zeros[...] = jnp.zeros(zeros.shape, F32)

        def pad_rows(start):
            def per_expert(e, c):
                def one(r, c2):
                    cp = _row_copy(zeros, 0, xs_out, pstart_ref[e] + r, sem)
                    cp.start() if start else cp.wait()
                    return c2
                lax.fori_loop(0, plen_ref[e], one, 0)
                return c
            lax.fori_loop(0, N_EXPERTS, per_expert, 0)

            def tail(t, c):
                row0 = pl.multiple_of((nu_ref[0] + t) * MOE_TM, MOE_TM)
                cp = pltpu.make_async_copy(zeros, xs_out.at[pl.ds(row0, MOE_TM)], sem)
                cp.start() if start else cp.wait()
                return c
            lax.fori_loop(0, nb - nu_ref[0], tail, 0)

        pad_rows(True)
        pad_rows(False)

    def issue(r, c):
        for k in range(TOP_K):
            _row_copy(h2_ref, r, xs_out, dest_ref[base + r * TOP_K + k], sem).start(priority=k % 2)
        return c

    lax.fori_loop(0, tm, issue, 0, unroll=4)

    def drain(r, c):
        for k in range(TOP_K):
            _row_copy(h2_ref, r, xs_out, dest_ref[base + r * TOP_K + k], sem).wait()
        return c

    lax.fori_loop(0, tm, drain, 0, unroll=4)


def _dispatch(dest_flat, pads, nu, h2, n_pad):
    T, W = h2.shape
    tm = min(256, T)
    return pl.pallas_call(
        functools.partial(_dispatch_kernel, tm=tm, nb=n_pad // MOE_TM),
        out_shape=jax.ShapeDtypeStruct((n_pad, W), F32),
        grid_spec=pltpu.PrefetchScalarGridSpec(
            num_scalar_prefetch=4,
            grid=(T // tm,),
            in_specs=[pl.BlockSpec((tm, W), lambda i, *_: (i, 0))],
            out_specs=pl.BlockSpec(memory_space=pl.ANY),
            scratch_shapes=[pltpu.VMEM((MOE_TM, W), F32), pltpu.SemaphoreType.DMA(())],
        ),
        compiler_params=_cparams(("arbitrary",)),
        name="dispatch",
    )(dest_flat, pads[0, :N_EXPERTS], pads[1, :N_EXPERTS], nu, h2)


def _gate_up_kernel(be_ref, nu_ref, first_ref, run_ref, nxt_ref, last_ref, nr_ref, vr_ref,
                    x_ref, bg_ref, bu_ref, w_hbm, o_ref, wbuf, sem, *, tf, nf):
    f = pl.program_id(0)
    i = pl.program_id(1)
    used = i < nu_ref[0]

    def w_copy(e, ff, slot, part):
        col = pl.multiple_of(part * D_FF + ff * tf, tf)
        return pltpu.make_async_copy(w_hbm.at[e, :, pl.ds(col, tf)], wbuf.at[slot, part], sem.at[slot])

    @pl.when(used & (first_ref[i] == 1))
    def _():
        run = f * nr_ref[0] + run_ref[i]
        slot = run & 1
        e = be_ref[i]

        @pl.when(run == 0)
        def _():
            w_copy(e, f, slot, 0).start()
            w_copy(e, f, slot, 1).start()

        w_copy(e, f, slot, 0).wait()
        w_copy(e, f, slot, 1).wait()
        last = last_ref[i]

        @pl.when((last == 0) | (f < nf - 1))
        def _():
            w_copy(nxt_ref[i], f + last, 1 - slot, 0).start()
            w_copy(nxt_ref[i], f + last, 1 - slot, 1).start()

    def swiglu(x, slot):
        gate = jnp.minimum(_dot(x, wbuf[slot, 0]) + bg_ref[...], SWIGLU_LIMIT)
        up = jnp.clip(_dot(x, wbuf[slot, 1]) + bu_ref[...], -SWIGLU_LIMIT, SWIGLU_LIMIT)
        return ((up + 1.0) * (gate * _sigmoid(SWIGLU_ALPHA * gate))).astype(BF16)

    half = x_ref.shape[0] // 2
    full = vr_ref[i] > half

    @pl.when(used & full)
    def _():
        o_ref[...] = swiglu(x_ref[...], (f * nr_ref[0] + run_ref[i]) & 1)

    @pl.when(used & jnp.logical_not(full))
    def _():
        o_ref[0:half, :] = swiglu(x_ref[0:half, :], (f * nr_ref[0] + run_ref[i]) & 1)
        o_ref[half:2 * half, :] = jnp.zeros((half, o_ref.shape[1]), BF16)

    @pl.when(jnp.logical_not(used))
    def _():
        o_ref[...] = jnp.zeros(o_ref.shape, BF16)


def _gate_up(sched, xs, w_gate_up, b_gate_up3):
    n_pad, D = xs.shape
    tm = MOE_TM
    tf = 1024
    nf = D_FF // tf
    nb = n_pad // tm
    eff = lambda i, nu: jnp.minimum(i, nu[0] - 1)
    return pl.pallas_call(
        functools.partial(_gate_up_kernel, tf=tf, nf=nf),
        out_shape=jax.ShapeDtypeStruct((n_pad, D_FF), BF16),
        grid_spec=pltpu.PrefetchScalarGridSpec(
            num_scalar_prefetch=len(sched),
            grid=(nf, nb),
            in_specs=[
                pl.BlockSpec((tm, D), lambda f, i, be, nu, *_: (eff(i, nu), 0)),
                pl.BlockSpec((None, 1, tf), lambda f, i, be, nu, *_: (be[eff(i, nu)], 0, f)),
                pl.BlockSpec((None, 1, tf), lambda f, i, be, nu, *_: (be[eff(i, nu)], 0, nf + f)),
                pl.BlockSpec(memory_space=pl.ANY),
            ],
            out_specs=pl.BlockSpec((tm, tf), lambda f, i, *_: (i, f)),
            scratch_shapes=[
                pltpu.VMEM((2, 2, D, tf), F32),
                pltpu.SemaphoreType.DMA((2,)),
            ],
        ),
        compiler_params=_cparams(("arbitrary", "arbitrary")),
        name="gate_up",
    )(*sched, xs, b_gate_up3, b_gate_up3, w_gate_up)


def _down_kernel(be_ref, nu_ref, first_ref, run_ref, nxt_ref, last_ref, nr_ref, vr_ref,
                 h_ref, b_ref, w_hbm, o_ref, wbuf, sem):
    i = pl.program_id(0)
    used = i < nu_ref[0]

    def w_copy(e, slot):
        return pltpu.make_async_copy(w_hbm.at[e], wbuf.at[slot], sem.at[slot])

    @pl.when(used & (first_ref[i] == 1))
    def _():
        run = run_ref[i]
        slot = run & 1
        e = be_ref[i]

        @pl.when(run == 0)
        def _():
            w_copy(e, slot).start()

        w_copy(e, slot).wait()

        @pl.when(last_ref[i] == 0)
        def _():
            w_copy(nxt_ref[i], 1 - slot).start()

    half = h_ref.shape[0] // 2
    full = vr_ref[i] > half

    @pl.when(used & full)
    def _():
        o_ref[...] = _dot(h_ref[...].astype(F32), wbuf[run_ref[i] & 1]) + b_ref[...]

    @pl.when(used & jnp.logical_not(full))
    def _():
        o_ref[0:half, :] = _dot(h_ref[0:half, :].astype(F32), wbuf[run_ref[i] & 1]) + b_ref[...]
        o_ref[half:2 * half, :] = jnp.zeros((half, o_ref.shape[1]), F32)

    @pl.when(jnp.logical_not(used))
    def _():
        o_ref[...] = jnp.zeros(o_ref.shape, F32)


def _down(sched, h, w_down, b_down3):
    n_pad, F = h.shape
    D = w_down.shape[2]
    tm = MOE_TM
    nb = n_pad // tm
    eff = lambda i, nu: jnp.minimum(i, nu[0] - 1)
    return pl.pallas_call(
        _down_kernel,
        out_shape=jax.ShapeDtypeStruct((n_pad, D), F32),
        grid_spec=pltpu.PrefetchScalarGridSpec(
            num_scalar_prefetch=len(sched),
            grid=(nb,),
            in_specs=[
                pl.BlockSpec((tm, F), lambda i, be, nu, *_: (eff(i, nu), 0)),
                pl.BlockSpec((None, 1, D), lambda i, be, nu, *_: (be[eff(i, nu)], 0, 0)),
                pl.BlockSpec(memory_space=pl.ANY),
            ],
            out_specs=pl.BlockSpec((tm, D), lambda i, *_: (i, 0)),
            scratch_shapes=[
                pltpu.VMEM((2, F, D), F32),
                pltpu.SemaphoreType.DMA((2,)),
            ],
        ),
        compiler_params=_cparams(("arbitrary",)),
        name="down",
    )(*sched, h, b_down3, w_down)


def _combine_kernel(dest_ref, g_ref, x1_ref, fw_ref, y_hbm, o_ref, buf, sem, *, tm):
    base = pl.program_id(0) * (tm * TOP_K)

    def issue(r, c):
        for k in range(TOP_K):
            _row_copy(y_hbm, dest_ref[base + r * TOP_K + k], buf.at[k], r, sem).start(priority=k % 2)
        return c

    lax.fori_loop(0, tm, issue, 0, unroll=4)

    def drain(r, c):
        for k in range(TOP_K):
            _row_copy(y_hbm, dest_ref[base + r * TOP_K + k], buf.at[k], r, sem).wait()
        return c

    lax.fori_loop(0, tm, drain, 0, unroll=4)

    acc = x1_ref[...]
    gts = g_ref[...]
    for k in range(TOP_K):
        acc = acc + gts[:, k:k + 1] * buf[k]
    o_ref[...] = _rms_rows(acc, fw_ref[...])


def _combine(dest_flat, gates, x1, fw, y):
    T, D = x1.shape
    tm = min(256, T)
    return pl.pallas_call(
        functools.partial(_combine_kernel, tm=tm),
        out_shape=jax.ShapeDtypeStruct((T, D), F32),
        grid_spec=pltpu.PrefetchScalarGridSpec(
            num_scalar_prefetch=1,
            grid=(T // tm,),
            in_specs=[
                pl.BlockSpec((tm, LANES), lambda i, d: (i, 0)),
                pl.BlockSpec((tm, D), lambda i, d: (i, 0)),
                pl.BlockSpec((1, D), lambda i, d: (0, 0)),
                pl.BlockSpec(memory_space=pl.ANY),
            ],
            out_specs=pl.BlockSpec((tm, D), lambda i, d: (i, 0)),
            scratch_shapes=[pltpu.VMEM((TOP_K, tm, D), F32), pltpu.SemaphoreType.DMA(())],
        ),
        compiler_params=_cparams(("arbitrary",)),
        name="combine",
    )(dest_flat, gates, x1, fw, y)


def _rope_angles(pos):
    inv = ROPE_THETA ** (-np.arange(0, ROPE_DIM, 2, dtype=np.float64) / ROPE_DIM)
    return pos.astype(np.float64)[:, None] * inv[None, :]


def _rope_tables128(S):
    ang = _rope_angles(np.arange(S))
    c, s = np.cos(ang), np.sin(ang)
    one = np.ones((S, 64 - ROPE_DIM))
    zero = np.zeros((S, 64 - ROPE_DIM))
    z8 = np.zeros((S, 8))
    cos64 = np.concatenate([c, c, one], axis=1)
    sina64 = np.concatenate([z8, s, zero], axis=1)
    sinb64 = np.concatenate([-s, z8, zero], axis=1)
    t2 = lambda t: jnp.asarray(np.concatenate([t, t], axis=1), F32)
    return t2(cos64), t2(sina64), t2(sinb64)


def _rope_tables_cmp(ncp):
    ang = _rope_angles(np.arange(ncp) * CMP_STRIDE + CMP_BLOCK - 1)
    c, s = np.cos(ang), np.sin(ang)
    cos64 = np.concatenate([c, c, np.ones((ncp, 64 - ROPE_DIM))], axis=1)
    sin64 = np.concatenate([s, s, np.zeros((ncp, 64 - ROPE_DIM))], axis=1)
    rot = np.zeros((64, 64), np.float32)
    for d in range(8):
        rot[d + 8, d] = -1.0
        rot[d, d + 8] = 1.0
    return jnp.asarray(cos64, F32), jnp.asarray(sin64, F32), jnp.asarray(rot, BF16)


def _overlap_matrix_t(S):
    ncp = S // CMP_STRIDE
    n_c = (S - CMP_BLOCK) // CMP_STRIDE + 1
    ovl = np.zeros((LANES, ncp), np.float32)
    for c in range(n_c):
        for l in range(CMP_BLOCK):
            ovl[(c * CMP_STRIDE + l) // SEL_BLOCK, c] += 1.0
    return jnp.asarray(ovl, BF16)


def _pack_segments():
    widths = [SSM_D_INNER, SSM_CONV_DIM, SSM_HEADS, NSA_D] + [KV_D] * 6 + [3 * NSA_HEADS]
    cuts = [int(c) for c in np.cumsum([0] + widths)]
    return ((COL_Q, cuts[3], NSA_D), (COL_Z, cuts[0], SSM_D_INNER), (COL_XBC, cuts[1], SSM_CONV_DIM),
            (COL_KV, cuts[4], 6 * KV_D), (COL_DT, cuts[2], SSM_HEADS), (COL_GATE, cuts[10], 3 * NSA_HEADS))


def _pack_w_in_kernel(w_ref, o_ref):
    cols = w_ref.shape[1]
    covered = np.zeros(PROJ_W, bool)
    for dst, src, width in _pack_segments():
        covered[dst:dst + width] = True
        for c in range(0, width, 512):
            n = min(512, width - c)
            o_ref[dst + c:dst + c + n, :] = w_ref[src + c:src + c + n, :].astype(BF16)
    row = 0
    while row < PROJ_W:
        if covered[row]:
            row += 1
            continue
        end = row
        while end < PROJ_W and not covered[end]:
            end += 1
        o_ref[row:end, :] = jnp.zeros((end - row, cols), BF16)
        row = end


def _pack_w_in(w_in):
    wt = jnp.swapaxes(w_in, 1, 2)
    _, W, D = wt.shape
    tc = 512
    return pl.pallas_call(
        _pack_w_in_kernel,
        out_shape=jax.ShapeDtypeStruct((PROJ_W, D), BF16),
        grid=(D // tc,),
        in_specs=[pl.BlockSpec((None, W, tc), lambda i: (0, 0, i))],
        out_specs=pl.BlockSpec((PROJ_W, tc), lambda i: (0, i)),
        compiler_params=_cparams(("parallel",)),
        name="pack_w_in",
    )(wt)


def _sched_columns(tab, n_blocks):
    per_tile = lambda c: tab[:n_blocks, c]
    once = lambda c: tab[0:1, c]
    return (per_tile(0), once(1), per_tile(2), per_tile(3), per_tile(4), per_tile(5), once(6), per_tile(7))


def kernel(x, attn_norm_w, w_in, conv_w, conv_b, dt_bias, a_log, d_skip, ssm_norm_w, cmp_pos_emb, cmp_w1,
           cmp_b1, cmp_w2, cmp_b2, w_out, moe_norm_w, w_router, b_router, w_gate_up, b_gate_up, w_down,
           b_down, final_norm_w):
    B, S, D = x.shape
    T = B * S
    x2d = x.reshape(T, D)
    row = lambda v: v.reshape(1, -1)
    padl = lambda v: jnp.pad(v.reshape(1, -1), ((0, 0), (0, LANES - v.size)))

    proj = _in_proj(x2d, row(attn_norm_w[0]), _pack_w_in(w_in))
    y_ssd = _ssd(proj, B, S, conv_w[0], row(conv_b[0]), padl(dt_bias[0]), padl(a_log[0]),
                 row(jnp.repeat(d_skip[0], SSM_HEAD_DIM)), row(ssm_norm_w[0]))
    cos, sina, sinb = _rope_tables128(S)
    q_rot, kse, vst, kw, vwt, sg = _nsa_prep(proj, B, S, cos, sina, sinb)
    cosc, sinc, rot = _rope_tables_cmp(S // CMP_STRIDE)
    kvc, kvct = _compress(proj, B, S, cmp_pos_emb[0], cmp_w1[0], cmp_b1[0][:, None, :], cmp_w2[0],
                          cmp_b2[0][:, None, :], cosc, sinc, rot)
    y_nsa = _nsa_attn(q_rot, kvc, kvct, kse, vst, kw, vwt, sg, _overlap_matrix_t(S), B, S)

    wr = jnp.pad(w_router[0], ((0, 0), (0, LANES - N_EXPERTS)))
    x1, h2, logits = _out_proj(y_ssd, y_nsa.reshape(T, NSA_D), x2d, w_out[0].astype(BF16),
                                row(moe_norm_w[0]), wr, padl(b_router[0]))

    n_pad = T * TOP_K + N_EXPERTS * MOE_TM
    n_blocks = n_pad // MOE_TM
    idx, rank, gates, cnt = _route(logits)
    dest, be_tab, pads = _route_fin(cnt, idx, rank, n_blocks)
    dest_flat = dest[:, :TOP_K].reshape(-1)
    sched = _sched_columns(be_tab, n_blocks)
    xs = _dispatch(dest_flat, pads, sched[1], h2, n_pad)
    hmid = _gate_up(sched, xs, w_gate_up[0], b_gate_up[0][:, None, :])
    y = _down(sched, hmid, w_down[0], b_down[0][:, None, :])
    out = _combine(dest_flat, gates, x1, row(final_norm_w), y)
    return out.reshape(B, S, D)
```

```python
import functools
import math

import jax
import jax.numpy as jnp
import numpy as np
from jax import lax
from jax.experimental import pallas as pl
from jax.experimental.pallas import tpu as pltpu

F32 = jnp.float32
BF16 = jnp.bfloat16
I32 = jnp.int32
U32 = jnp.uint32

NORM_EPS = 1e-5
SSM_D_INNER = 1024
SSM_HEAD_DIM = 64
SSM_HEADS = 16
SSM_GROUPS = 4
SSM_STATE = 128
SSM_CONV = 4
SSM_CHUNK = 128
SSM_CONV_DIM = 2048
NSA_HEADS = 16
NSA_KV_HEADS = 4
NSA_HEAD_DIM = 64
NSA_D = 1024
KV_D = 256
CMP_BLOCK = 32
CMP_STRIDE = 16
CMP_HIDDEN = 512
SEL_BLOCK = 64
N_SELECT = 16
WINDOW = 512
ROPE_THETA = 500000.0
ROPE_DIM = 16
SEL_FORCE_SCORE = 1.0e4
N_EXPERTS = 32
TOP_K = 4
D_FF = 2048
SWIGLU_LIMIT = 7.0
SWIGLU_ALPHA = 1.702

LANES = 128
SUBLANES = 8
VMEM_LIMIT = 56 * 1024 * 1024

NEG = -1.0e30

COL_Q = 0
COL_Z = 1024
COL_XBC = 2048
COL_KV = 4096
COL_DT = 5632
COL_GATE = 5760
PROJ_W = 6144

MOE_TM = 256
Q_TILE = 256
SLC_TILE = 512
KE_W = LANES + 64
V_ROWS = 80


def _cparams(sem, vmem=VMEM_LIMIT):
    return pltpu.CompilerParams(dimension_semantics=sem, vmem_limit_bytes=vmem)


def _dot(a, b):
    return jnp.dot(a, b, preferred_element_type=F32)


def _dot_nt(a, b):
    return lax.dot_general(a, b, (((1,), (1,)), ((), ())), preferred_element_type=F32)


def _split3(b):
    b1 = b.astype(BF16)
    r1 = b - b1.astype(F32)
    b2 = r1.astype(BF16)
    r2 = r1 - b2.astype(F32)
    return b1, b2, r2.astype(BF16)


def _dot_exact_lhs(a_bf16, b):
    b1, b2, b3 = _split3(b)
    return _dot(a_bf16, b1) + _dot(a_bf16, b2) + _dot(a_bf16, b3)


def _dot_exact_rhs(a, b_bf16):
    a1, a2, a3 = _split3(a)
    return _dot(a1, b_bf16) + _dot(a2, b_bf16) + _dot(a3, b_bf16)


def _sigmoid(x):
    return 1.0 / (1.0 + jnp.exp(-x))


def _silu(x):
    return x * _sigmoid(x)


def _softplus(x):
    return jnp.maximum(x, 0.0) + jnp.log(1.0 + jnp.exp(-jnp.abs(x)))


def _gelu_tanh(x):
    c = math.sqrt(2.0 / math.pi)
    return 0.5 * x * (1.0 + jnp.tanh(c * (x + 0.044715 * (x * x * x))))


def _rms_rows(x, w):
    ms = jnp.mean(x * x, axis=-1, keepdims=True)
    return x * lax.rsqrt(ms + NORM_EPS) * w


def _in_proj_kernel(x_ref, nw_ref, w_ref, o_ref, hn_ref, *, tm):
    @pl.when(pl.program_id(1) == 0)
    def _():
        def body(c, carry):
            r = pl.multiple_of(c * 128, 128)
            hn_ref[pl.ds(r, 128), :] = _rms_rows(x_ref[pl.ds(r, 128), :], nw_ref[...]).astype(BF16)
            return carry
        lax.fori_loop(0, tm // 128, body, 0)

    o_ref[...] = _dot_nt(hn_ref[...], w_ref[...])


def _in_proj(x2d, nw, wpt):
    T, D = x2d.shape
    NP = wpt.shape[0]
    tm = min(1024, T)
    tn = 1024
    return pl.pallas_call(
        functools.partial(_in_proj_kernel, tm=tm),
        out_shape=jax.ShapeDtypeStruct((T, NP), F32),
        grid=(T // tm, NP // tn),
        in_specs=[
            pl.BlockSpec((tm, D), lambda i, j: (i, 0)),
            pl.BlockSpec((1, D), lambda i, j: (0, 0)),
            pl.BlockSpec((tn, D), lambda i, j: (j, 0)),
        ],
        out_specs=pl.BlockSpec((tm, tn), lambda i, j: (i, j)),
        scratch_shapes=[pltpu.VMEM((tm, D), BF16)],
        compiler_params=_cparams(("parallel", "arbitrary")),
        name="in_proj",
    )(x2d, nw, wpt)


def _rope128(x, cos, sina, sinb):
    return x * cos + pltpu.roll(x, 8, 1) * sina + pltpu.roll(x, 120, 1) * sinb


def _nsa_prep_kernel(q_ref, ks_ref, kw_ref, gl_ref, cos_ref, sina_ref, sinb_ref,
                     qo_ref, kso_ref, vso_ref, kwo_ref, vwo_ref, sg_ref):
    cos = cos_ref[...]
    sina = sina_ref[...]
    sinb = sinb_ref[...]
    scale = NSA_HEAD_DIM ** -0.5
    for c in range(NSA_D // LANES):
        xq = q_ref[:, c * LANES:(c + 1) * LANES]
        qo_ref[:, c * LANES:(c + 1) * LANES] = (_rope128(xq, cos, sina, sinb) * scale).astype(BF16)
    ts = q_ref.shape[0]
    key = pl.program_id(1) * ts + lax.broadcasted_iota(I32, (ts, LANES), 0)
    blk1h = jnp.where(lax.shift_right_logical(key, 6) == lax.broadcasted_iota(I32, (ts, LANES), 1), 1.0, 0.0)
    ones = jnp.ones((V_ROWS - 64, LANES), BF16)
    for src, ko, vo, ext in ((ks_ref, kso_ref, vso_ref, True), (kw_ref, kwo_ref, vwo_ref, False)):
        for c in range(KV_D // LANES):
            kr = _rope128(src[:, c * LANES:(c + 1) * LANES], cos, sina, sinb)
            vt = src[:, KV_D + c * LANES:KV_D + (c + 1) * LANES].T
            for half in range(2):
                g = 2 * c + half
                kg = kr[:, half * 64:(half + 1) * 64].astype(BF16)
                if ext:
                    ko[g, :, 0:LANES] = blk1h.astype(BF16)
                    ko[g, :, LANES:LANES + 64] = kg
                else:
                    ko[g] = kg
                for j in range(ts // LANES):
                    vo[g, j, 0:64, :] = vt[half * 64:(half + 1) * 64, j * LANES:(j + 1) * LANES].astype(BF16)
                    vo[g, j, 64:V_ROWS, :] = ones
    sg_ref[...] = _sigmoid(gl_ref[...])


def _nsa_prep(proj, B, S, cos, sina, sinb):
    ts = min(512, S)
    nst = S // ts
    G = NSA_KV_HEADS
    kv_spec = pl.BlockSpec((None, G, ts, 64), lambda b, s: (b, 0, s, 0))
    kv_shape = jax.ShapeDtypeStruct((B, G, S, 64), BF16)
    ke_spec = pl.BlockSpec((None, G, ts, KE_W), lambda b, s: (b, 0, s, 0))
    ke_shape = jax.ShapeDtypeStruct((B, G, S, KE_W), BF16)
    vt_spec = pl.BlockSpec((None, G, ts // LANES, V_ROWS, LANES), lambda b, s: (b, 0, s, 0, 0))
    vt_shape = jax.ShapeDtypeStruct((B, G, S // LANES, V_ROWS, LANES), BF16)
    return pl.pallas_call(
        _nsa_prep_kernel,
        out_shape=(
            jax.ShapeDtypeStruct((B, S, NSA_D), BF16),
            ke_shape, vt_shape, kv_shape, vt_shape,
            jax.ShapeDtypeStruct((B, S, LANES), F32),
        ),
        grid=(B, nst),
        in_specs=[
            pl.BlockSpec((ts, NSA_D), lambda b, s: (b * nst + s, COL_Q // NSA_D)),
            pl.BlockSpec((ts, 512), lambda b, s: (b * nst + s, (COL_KV + 512) // 512)),
            pl.BlockSpec((ts, 512), lambda b, s: (b * nst + s, (COL_KV + 1024) // 512)),
            pl.BlockSpec((ts, LANES), lambda b, s: (b * nst + s, COL_GATE // LANES)),
            pl.BlockSpec((ts, LANES), lambda b, s: (s, 0)),
            pl.BlockSpec((ts, LANES), lambda b, s: (s, 0)),
            pl.BlockSpec((ts, LANES), lambda b, s: (s, 0)),
        ],
        out_specs=(
            pl.BlockSpec((None, ts, NSA_D), lambda b, s: (b, s, 0)),
            ke_spec, vt_spec, kv_spec, vt_spec,
            pl.BlockSpec((None, ts, LANES), lambda b, s: (b, s, 0)),
        ),
        compiler_params=_cparams(("parallel", "parallel")),
        name="nsa_prep",
    )(proj, proj, proj, proj, cos, sina, sinb)


def _ssd_kernel(z_ref, xbc_ref, dt_ref, cw_ref, cb_ref, dtb_ref, alog_ref, dsk_ref, nw_ref,
                y_ref, buf, st):
    L = SSM_CHUNK
    c = pl.program_id(1)

    @pl.when(c == 0)
    def _():
        buf[...] = jnp.zeros(buf.shape, F32)
        st[...] = jnp.zeros(st.shape, F32)

    cur = xbc_ref[...]
    tail = buf[...]
    row8 = lax.broadcasted_iota(I32, (SUBLANES, SSM_CONV_DIM), 0)
    acc = cb_ref[...] + cw_ref[SSM_CONV - 1:SSM_CONV, :] * cur
    for s in range(1, SSM_CONV):
        rolled = pltpu.roll(cur, s, 0)
        head = jnp.where(row8 < s, pltpu.roll(tail, s, 0), rolled[0:SUBLANES, :])
        shifted = jnp.concatenate([head, rolled[SUBLANES:, :]], axis=0)
        acc = acc + cw_ref[SSM_CONV - 1 - s:SSM_CONV - s, :] * shifted
    xc = _silu(acc)
    buf[...] = xbc_ref[L - SUBLANES:L, :]

    lane = lax.broadcasted_iota(I32, (L, LANES), 1)
    row = lax.broadcasted_iota(I32, (L, LANES), 0)
    lo = lane < 64
    dtv = jnp.where(lane < SSM_HEADS, _softplus(dt_ref[...] + dtb_ref[...]), 0.0)
    a = -jnp.exp(alog_ref[...])
    tri = jnp.where(row >= lane, 1.0, 0.0).astype(BF16)
    acs = _dot_exact_lhs(tri, dtv * a)
    acs_t = acs.T
    causal = row >= lane

    ys = []
    for g in range(SSM_GROUPS):
        bg = xc[:, SSM_D_INNER + g * SSM_STATE:SSM_D_INNER + (g + 1) * SSM_STATE]
        cg = xc[:, SSM_D_INNER + (SSM_GROUPS + g) * SSM_STATE:SSM_D_INNER + (SSM_GROUPS + g + 1) * SSM_STATE]
        bgt = bg.T.astype(BF16)
        cgb = cg.astype(BF16)
        gmat = _dot(cgb, bgt)
        for p in (2 * g, 2 * g + 1):
            h0, h1 = 2 * p, 2 * p + 1
            xs_pair = xc[:, p * LANES:(p + 1) * LANES]
            col0 = acs[:, h0:h0 + 1]
            col1 = acs[:, h1:h1 + 1]
            colp = jnp.where(lo, col0, col1)
            last = jnp.where(lo[0:1, :], acs[L - 1:L, h0:h0 + 1], acs[L - 1:L, h1:h1 + 1])
            x = xs_pair * jnp.where(lo, dtv[:, h0:h0 + 1], dtv[:, h1:h1 + 1])
            m0 = (gmat * jnp.where(causal, jnp.exp(col0 - acs_t[h0:h0 + 1, :]), 0.0)).astype(BF16)
            m1 = (gmat * jnp.where(causal, jnp.exp(col1 - acs_t[h1:h1 + 1, :]), 0.0)).astype(BF16)
            y_diag = _dot(m0, jnp.where(lo, x, 0.0).astype(BF16)) + _dot(m1, jnp.where(lo, 0.0, x).astype(BF16))
            s_prev = st[p]
            y_off = _dot(cgb, s_prev.astype(BF16)) * jnp.exp(colp)
            w = (x * jnp.exp(last - colp)).astype(BF16)
            st[p] = jnp.exp(last) * s_prev + _dot(bgt, w)
            ys.append(y_diag + y_off + dsk_ref[:, p * LANES:(p + 1) * LANES] * xs_pair)
    y = jnp.concatenate(ys, axis=1)
    gte = y * _silu(z_ref[...])
    gw = SSM_D_INNER // SSM_GROUPS
    outs = []
    for k in range(SSM_GROUPS):
        gk = gte[:, k * gw:(k + 1) * gw]
        ms = jnp.mean(gk * gk, axis=-1, keepdims=True)
        outs.append(gk * lax.rsqrt(ms + NORM_EPS))
    y_ref[...] = (jnp.concatenate(outs, axis=1) * nw_ref[...]).astype(BF16)


def _ssd(proj, B, S, conv_w, conv_b, dtb, alog, dskip, nw):
    L = SSM_CHUNK
    nc = S // L
    small = lambda shp: pl.BlockSpec(shp, lambda b, c: (0, 0))
    return pl.pallas_call(
        _ssd_kernel,
        out_shape=jax.ShapeDtypeStruct((B * S, SSM_D_INNER), BF16),
        grid=(B, nc),
        in_specs=[
            pl.BlockSpec((L, SSM_D_INNER), lambda b, c: (b * nc + c, COL_Z // SSM_D_INNER)),
            pl.BlockSpec((L, SSM_CONV_DIM), lambda b, c: (b * nc + c, COL_XBC // SSM_CONV_DIM)),
            pl.BlockSpec((L, LANES), lambda b, c: (b * nc + c, COL_DT // LANES)),
            small((SSM_CONV, SSM_CONV_DIM)),
            small((1, SSM_CONV_DIM)),
            small((1, LANES)),
            small((1, LANES)),
            small((1, SSM_D_INNER)),
            small((1, SSM_D_INNER)),
        ],
        out_specs=pl.BlockSpec((L, SSM_D_INNER), lambda b, c: (b * nc + c, 0)),
        scratch_shapes=[
            pltpu.VMEM((SUBLANES, SSM_CONV_DIM), F32),
            pltpu.VMEM((SSM_HEADS // 2, SSM_STATE, LANES), F32),
        ],
        compiler_params=_cparams(("parallel", "arbitrary")),
        name="ssd",
    )(proj, proj, proj, conv_w, conv_b, dtb, alog, dskip, nw)


def _compress_kernel(xa_ref, xb_ref, pos_ref, w1_ref, b1_ref, w2_ref, b2_ref, cos_ref, sin_ref, rot_ref,
                     o_ref, ot_ref, w1b, feat, *, ncp):
    kv = pl.program_id(1)
    outs = []
    half = CMP_STRIDE
    hw = half * 64
    w1b[...] = w1_ref[...].astype(BF16)
    w2 = w2_ref[...].astype(BF16)
    G = NSA_KV_HEADS
    is_k = kv == 0
    for g in range(G):
        x_ref = (xa_ref, xb_ref)[g // 2]
        for l in range(half):
            xg = x_ref[pl.ds(l, ncp, stride=half), :][:, (g % 2) * 64:(g % 2 + 1) * 64]
            feat[:, l * 64:(l + 1) * 64] = xg + pos_ref[l:l + 1, :]
            feat[:, hw + l * 64:hw + (l + 1) * 64] = xg + pos_ref[half + l:half + l + 1, :]
        first = _dot(feat[:, 0:hw].astype(BF16), w1b[0:hw, :])
        second = _dot(feat[:, hw:2 * hw].astype(BF16), w1b[hw:2 * hw, :])
        h = first + pltpu.roll(second, ncp - 1, 0) + b1_ref[...]
        h = _gelu_tanh(h)
        o = _dot(h.astype(BF16), w2) + b2_ref[...]
        roped = o * cos_ref[...] + _dot_exact_rhs(o, rot_ref[...]) * sin_ref[...]
        outs.append(jnp.where(is_k, roped, o))
        o_ref[g] = outs[g]
    for c in range(G // 2):
        t = jnp.concatenate([outs[2 * c], outs[2 * c + 1]], axis=1).T
        ot_ref[2 * c] = t[0:64, :]
        ot_ref[2 * c + 1] = t[64:128, :]


def _compress(proj, B, S, pos, w1, b1, w2, b2, cosc, sinc, rot):
    ncp = S // CMP_STRIDE
    G = NSA_KV_HEADS
    kvblk = COL_KV // KV_D
    return pl.pallas_call(
        functools.partial(_compress_kernel, ncp=ncp),
        out_shape=(jax.ShapeDtypeStruct((B, 2, G, ncp, 64), F32),
                   jax.ShapeDtypeStruct((B, 2, G, 64, ncp), F32)),
        grid=(B, 2),
        in_specs=[
            pl.BlockSpec((S, LANES), lambda b, k: (b, 2 * (kvblk + k))),
            pl.BlockSpec((S, LANES), lambda b, k: (b, 2 * (kvblk + k) + 1)),
            pl.BlockSpec((None, CMP_BLOCK, 64), lambda b, k: (k, 0, 0)),
            pl.BlockSpec((None, CMP_BLOCK * 64, CMP_HIDDEN), lambda b, k: (k, 0, 0)),
            pl.BlockSpec((None, 1, CMP_HIDDEN), lambda b, k: (k, 0, 0)),
            pl.BlockSpec((None, CMP_HIDDEN, 64), lambda b, k: (k, 0, 0)),
            pl.BlockSpec((None, 1, 64), lambda b, k: (k, 0, 0)),
            pl.BlockSpec((ncp, 64), lambda b, k: (0, 0)),
            pl.BlockSpec((ncp, 64), lambda b, k: (0, 0)),
            pl.BlockSpec((64, 64), lambda b, k: (0, 0)),
        ],
        out_specs=(pl.BlockSpec((None, None, G, ncp, 64), lambda b, k: (b, k, 0, 0, 0)),
                   pl.BlockSpec((None, None, G, 64, ncp), lambda b, k: (b, k, 0, 0, 0))),
        scratch_shapes=[pltpu.VMEM((CMP_BLOCK * 64, CMP_HIDDEN), BF16),
                        pltpu.VMEM((ncp, CMP_BLOCK * 64), F32)],
        compiler_params=_cparams(("parallel", "parallel")),
        name="compress",
    )(proj, proj, pos, w1, b1, w2, b2, cosc, sinc, rot)


def _nsa_attn_kernel(q_ref, kc_ref, vct_ref, ks_ref, vst_ref, kw_ref, vwt_ref, sg_ref, ovlt_ref,
                     o_ref, *, ncp, n_c, n_sel, k_sel):
    g = pl.program_id(1)
    qi = pl.program_id(2)
    tq = Q_TILE
    R = NSA_HEADS // NSA_KV_HEADS
    t0 = qi * tq
    qt = q_ref[...].astype(F32).T
    q4t = jnp.concatenate([qt[r * 64:(r + 1) * 64, :] for r in range(R)], axis=1).astype(BF16)
    rep = lambda a: jnp.concatenate([a] * R, axis=1)
    tpos = t0 + lax.broadcasted_iota(I32, (1, tq), 1)
    cmax = lambda a: jnp.max(a, axis=0, keepdims=True)
    csum = lambda a: jnp.sum(a, axis=0, keepdims=True)
    jl = lax.broadcasted_iota(I32, (LANES, tq), 0)

    def online(carry, s, vt):
        m_old, acc = carry
        m_new = jnp.maximum(m_old, cmax(s))
        alpha = jnp.exp(m_old - m_new)
        return m_new, alpha * acc + _dot(vt, jnp.exp(s - m_new).astype(BF16))

    sgt = sg_ref[...].T
    gts = [[csum(jnp.where(jl == 3 * (g * R + r) + k, sgt, 0.0)) for k in range(3)] for r in range(R)]

    span = min(WINDOW + tq, kw_ref.shape[0])
    w0 = pl.multiple_of(jnp.maximum(t0 + tq - span, 0), tq)
    jw = lax.shift_right_logical(w0, int(math.log2(LANES)))
    vt = jnp.concatenate([vwt_ref[jw + c] for c in range(span // LANES)], axis=1)
    dist = tpos - (w0 + lax.broadcasted_iota(I32, (span, tq), 0))
    s = _dot(kw_ref[pl.ds(w0, span), :], q4t) + rep(jnp.where((dist >= 0) & (dist < WINDOW), 0.0, NEG))
    acc_win = _dot(vt, jnp.exp(s - cmax(s)).astype(BF16))
    o_win = acc_win[0:64] / acc_win[64:65]

    cidx = lax.broadcasted_iota(I32, (ncp, tq), 0)
    maskc = rep(jnp.where((cidx * CMP_STRIDE + (CMP_BLOCK - 1) <= tpos) & (cidx < n_c), 1.0, 0.0)) > 0.5
    s = _dot(kc_ref[...].astype(BF16), q4t)
    e = jnp.where(maskc, jnp.exp(s - cmax(jnp.where(maskc, s, NEG))), 0.0)
    p = e / jnp.maximum(csum(e), 1e-30)
    o_cmp = _dot(vct_ref[...].astype(BF16), p.astype(BF16))
    psum = p[:, 0:tq]
    for r in range(1, R):
        psum = psum + p[:, r * tq:(r + 1) * tq]

    p_hi = psum.astype(BF16)
    p_lo = (psum - p_hi.astype(F32)).astype(BF16)
    imp = _dot(ovlt_ref[...], p_hi) + _dot(ovlt_ref[...], p_lo)
    cur = lax.shift_right_logical(tpos, 6)
    forced = (jl == 0) | (jl == cur) | (jl == cur - 1)
    st = jnp.where(forced, SEL_FORCE_SCORE, jnp.where(jl <= cur, imp, -SEL_FORCE_SCORE))
    nsp = max(n_sel, SUBLANES)
    sel_rows = []
    for v in range(nsp // SUBLANES):
        gv = st[v * SUBLANES:(v + 1) * SUBLANES, :]
        jrow = v * SUBLANES + lax.broadcasted_iota(I32, (SUBLANES, tq), 0)
        cnt = jnp.zeros((SUBLANES, tq), F32)
        for i in range(n_sel):
            ri = st[i:i + 1, :]
            if v * SUBLANES + SUBLANES - 1 < i:
                beats = ri > gv
            elif v * SUBLANES > i:
                beats = ri >= gv
            else:
                beats = ((ri >= gv) & (jrow > i)) | ((ri > gv) & (jrow < i))
            cnt = cnt + jnp.where(beats, 1.0, 0.0)
        sel_rows.append(jnp.where((cnt < float(k_sel)) & (jrow < n_sel), 1.0, 0.0))
    sel_rows.append(jnp.zeros((LANES - nsp, tq), F32))
    sel = jnp.concatenate(sel_rows, axis=0)

    tk = min(SLC_TILE, ks_ref.shape[0])
    nsub = tk // LANES
    picked = sel > 0.5
    before = jl < lax.shift_right_logical(t0, 6)
    qext_diag = jnp.concatenate([rep(jnp.where(picked, 0.0, NEG)).astype(BF16), q4t], axis=0)
    qext = jnp.concatenate([rep(jnp.where(before & picked, 0.0, NEG)).astype(BF16), q4t], axis=0)
    r_i = lax.broadcasted_iota(I32, (tq, tq), 0)
    c_i = lax.broadcasted_iota(I32, (tq, tq), 1)
    s = _dot(ks_ref[pl.ds(pl.multiple_of(t0, tq), tq), :], qext_diag) + rep(jnp.where(r_i <= c_i, 0.0, NEG))
    nd = tq // LANES
    vd = jnp.concatenate([vst_ref[qi * nd + c] for c in range(nd)], axis=1)
    m_diag, acc_diag = online((jnp.full((1, R * tq), NEG, F32), jnp.zeros((V_ROWS, R * tq), F32)), s, vd)
    n_tiles = ks_ref.shape[0] // tk

    def scores(j):
        jj = jnp.minimum(j, n_tiles - 1)
        return _dot(ks_ref[pl.ds(pl.multiple_of(jj * tk, tk), tk), :], qext)

    def slc_body(j, carry):
        vt = jnp.concatenate([vst_ref[j * nsub + c] for c in range(nsub)], axis=1)
        return online(carry, scores(j), vt)

    n_main = lax.shift_right_logical(t0 + (tk - 1), int(math.log2(tk)))
    _, acc_slc = lax.fori_loop(0, n_main, slc_body, (m_diag, acc_diag))
    o_slc = acc_slc[0:64] / acc_slc[64:65]

    outs = []
    for r in range(R):
        cols = slice(r * tq, (r + 1) * tq)
        outs.append(gts[r][0] * o_cmp[:, cols] + gts[r][1] * o_slc[:, cols] + gts[r][2] * o_win[:, cols])
    for c in range(R // 2):
        o_ref[:, c * LANES:(c + 1) * LANES] = jnp.concatenate(outs[2 * c:2 * c + 2], axis=0).T.astype(BF16)


def _nsa_attn(q_rot, kvc, kvct, kse, vst, kw, vwt, sg, ovlt, B, S):
    G = NSA_KV_HEADS
    nq = S // Q_TILE
    ncp = S // CMP_STRIDE
    n_c = (S - CMP_BLOCK) // CMP_STRIDE + 1
    n_sel = S // SEL_BLOCK
    k_sel = min(N_SELECT, n_sel)
    keys = pl.BlockSpec((None, None, S, 64), lambda b, g, i: (b, g, 0, 0))
    keys_ext = pl.BlockSpec((None, None, S, KE_W), lambda b, g, i: (b, g, 0, 0))
    vals = pl.BlockSpec((None, None, S // LANES, V_ROWS, LANES), lambda b, g, i: (b, g, 0, 0, 0))
    return pl.pallas_call(
        functools.partial(_nsa_attn_kernel, ncp=ncp, n_c=n_c, n_sel=n_sel, k_sel=k_sel),
        out_shape=jax.ShapeDtypeStruct((B, S, NSA_D), BF16),
        grid=(B, G, nq),
        in_specs=[
            pl.BlockSpec((None, Q_TILE, KV_D), lambda b, g, i: (b, i, g)),
            pl.BlockSpec((None, None, None, ncp, 64), lambda b, g, i: (b, 0, g, 0, 0)),
            pl.BlockSpec((None, None, None, 64, ncp), lambda b, g, i: (b, 1, g, 0, 0)),
            keys_ext, vals, keys, vals,
            pl.BlockSpec((None, Q_TILE, LANES), lambda b, g, i: (b, i, 0)),
            pl.BlockSpec(ovlt.shape, lambda b, g, i: (0, 0)),
        ],
        out_specs=pl.BlockSpec((None, Q_TILE, KV_D), lambda b, g, i: (b, i, g)),
        compiler_params=_cparams(("parallel", "parallel", "arbitrary")),
        name="nsa_attn",
    )(q_rot, kvc, kvct, kse, vst, kw, vwt, sg, ovlt)


def _out_proj_kernel(ys_ref, yn_ref, x_ref, w_ref, nw_ref, wr_ref, br_ref, x1_ref, h2_ref, lg_ref):
    half = ys_ref.shape[1]
    y = _dot(ys_ref[...], w_ref[0:half, :]) + _dot(yn_ref[...], w_ref[half:2 * half, :])
    x1 = x_ref[...] + y
    x1_ref[...] = x1
    hn = _rms_rows(x1, nw_ref[...])
    h2_ref[...] = hn
    tm = hn.shape[0]
    h1, h2, _ = _split3(hn)
    w1, w2, _ = _split3(wr_ref[...])
    r = _dot(jnp.concatenate([h1, h2], axis=0), jnp.concatenate([w1, w2], axis=1))
    lg_ref[...] = r[0:tm, 0:LANES] + r[0:tm, LANES:2 * LANES] + r[tm:2 * tm, 0:LANES] + br_ref[...]


def _out_proj(y_ssd, y_nsa, x2d, w_out_b, nw, wr, br):
    T, D = x2d.shape
    tm = min(512, T)
    half = y_ssd.shape[1]
    return pl.pallas_call(
        _out_proj_kernel,
        out_shape=(
            jax.ShapeDtypeStruct((T, D), F32),
            jax.ShapeDtypeStruct((T, D), F32),
            jax.ShapeDtypeStruct((T, LANES), F32),
        ),
        grid=(T // tm,),
        in_specs=[
            pl.BlockSpec((tm, half), lambda i: (i, 0)),
            pl.BlockSpec((tm, half), lambda i: (i, 0)),
            pl.BlockSpec((tm, D), lambda i: (i, 0)),
            pl.BlockSpec((D, D), lambda i: (0, 0)),
            pl.BlockSpec((1, D), lambda i: (0, 0)),
            pl.BlockSpec((D, LANES), lambda i: (0, 0)),
            pl.BlockSpec((1, LANES), lambda i: (0, 0)),
        ],
        out_specs=(
            pl.BlockSpec((tm, D), lambda i: (i, 0)),
            pl.BlockSpec((tm, D), lambda i: (i, 0)),
            pl.BlockSpec((tm, LANES), lambda i: (i, 0)),
        ),
        compiler_params=_cparams(("parallel",)),
        name="out_proj",
    )(y_ssd, y_nsa, x2d, w_out_b, nw, wr, br)


def _route_kernel(lg_ref, idx_ref, rank_ref, gate_ref, cnt_ref, carry, *, tr):
    i = pl.program_id(0)

    @pl.when(i == 0)
    def _():
        carry[...] = jnp.zeros(carry.shape, F32)

    lane = lax.broadcasted_iota(I32, (tr, LANES), 1)
    lanef = lane.astype(F32)
    l = jnp.where(lane < N_EXPERTS, lg_ref[...], NEG)
    ohs, vals, idxs = [], [], []
    for _ in range(TOP_K):
        m = jnp.max(l, axis=-1, keepdims=True)
        idx = jnp.min(jnp.where(l == m, lanef, float(LANES)), axis=-1, keepdims=True)
        oh = lanef == idx
        l = jnp.where(oh, 2.0 * NEG, l)
        ohs.append(oh)
        vals.append(m)
        idxs.append(idx)
    es = [jnp.exp(v - vals[0]) for v in vals]
    den = es[0] + es[1] + es[2] + es[3]
    oh_sum = jnp.zeros((tr, LANES), F32)
    for oh in ohs:
        oh_sum = oh_sum + jnp.where(oh, 1.0, 0.0)
    r_i = lax.broadcasted_iota(I32, (tr, tr), 0)
    c_i = lax.broadcasted_iota(I32, (tr, tr), 1)
    strict = jnp.where(r_i > c_i, 1.0, 0.0).astype(BF16)
    base = carry[0:1, :] + _dot(strict, oh_sum.astype(BF16))
    idx_o = jnp.zeros((tr, LANES), F32)
    rank_o = jnp.zeros((tr, LANES), F32)
    gate_o = jnp.zeros((tr, LANES), F32)
    for k in range(TOP_K):
        rk = jnp.sum(jnp.where(ohs[k], base, 0.0), axis=-1, keepdims=True)
        idx_o = jnp.where(lane == k, idxs[k], idx_o)
        rank_o = jnp.where(lane == k, rk, rank_o)
        gate_o = jnp.where(lane == k, es[k] / den, gate_o)
    idx_ref[...] = idx_o.astype(I32)
    rank_ref[...] = rank_o.astype(I32)
    gate_ref[...] = gate_o
    carry[...] = carry[...] + jnp.sum(oh_sum, axis=0, keepdims=True)
    cnt_ref[...] = carry[...]


def _route(logits):
    T = logits.shape[0]
    tr = min(512, T)
    tok = pl.BlockSpec((tr, LANES), lambda i: (i, 0))
    return pl.pallas_call(
        functools.partial(_route_kernel, tr=tr),
        out_shape=(
            jax.ShapeDtypeStruct((T, LANES), I32),
            jax.ShapeDtypeStruct((T, LANES), I32),
            jax.ShapeDtypeStruct((T, LANES), F32),
            jax.ShapeDtypeStruct((SUBLANES, LANES), F32),
        ),
        grid=(T // tr,),
        in_specs=[tok],
        out_specs=(tok, tok, tok, pl.BlockSpec((SUBLANES, LANES), lambda i: (0, 0))),
        scratch_shapes=[pltpu.VMEM((SUBLANES, LANES), F32)],
        compiler_params=_cparams(("arbitrary",)),
        name="route",
    )(logits)


def _route_fin_kernel(cnt_ref, idx_ref, rank_ref, dest_ref, be_ref, pads_ref, *, tr, nbp, tm_shift):
    lane = lax.broadcasted_iota(I32, (SUBLANES, LANES), 1)
    cnt = cnt_ref[...].astype(I32)
    tm = 1 << tm_shift
    nblk = jnp.where(lane < N_EXPERTS, lax.shift_right_logical(cnt + (tm - 1), tm_shift), 0)
    r_i = lax.broadcasted_iota(I32, (LANES, LANES), 0)
    c_i = lax.broadcasted_iota(I32, (LANES, LANES), 1)
    upper = jnp.where(r_i <= c_i, 1.0, 0.0).astype(BF16)
    end_blk = _dot(nblk.astype(F32).astype(BF16), upper)
    start_row = (end_blk - nblk.astype(F32)) * float(tm)
    idx = idx_ref[...].astype(F32)
    lane_t = lax.broadcasted_iota(I32, (tr, LANES), 1)
    lane_f = lane_t.astype(F32)
    dest = jnp.zeros((tr, LANES), F32)
    for k in range(TOP_K):
        e_k = jnp.sum(jnp.where(lane_t == k, idx, 0.0), axis=-1, keepdims=True)
        s_k = jnp.sum(jnp.where(lane_f == e_k, start_row[0:1, :], 0.0), axis=-1, keepdims=True)
        dest = jnp.where(lane_t == k, s_k, dest)
    dest_ref[...] = dest.astype(I32) + jnp.where(lane_t < TOP_K, rank_ref[...], 0)
    blk = lax.broadcasted_iota(I32, (nbp, LANES), 0).astype(F32)
    lane_b = lax.broadcasted_iota(I32, (nbp, LANES), 1)
    lane_bf = lane_b.astype(F32)
    rsum = lambda v: jnp.sum(v, axis=-1, keepdims=True)
    end_row = end_blk[0:1, :]
    nblk_row = nblk.astype(F32)[0:1, :]
    is_exp = lane_b < N_EXPERTS
    nonempty = is_exp & (nblk_row > 0.0)
    be = jnp.minimum(rsum(jnp.where(is_exp & (end_row <= blk), 1.0, 0.0)), float(N_EXPERTS - 1))
    n_used = rsum(jnp.where(lane_b == N_EXPERTS - 1, end_row, 0.0))
    start_of = rsum(jnp.where(lane_bf == be, end_row - nblk_row, 0.0))
    first = jnp.where((start_of == blk[:, 0:1]) & (blk[:, 0:1] < n_used), 1.0, 0.0)
    run = rsum(jnp.where(nonempty & (lane_bf < be), 1.0, 0.0))
    none = float(LANES)
    nxt = jnp.min(jnp.where(nonempty & (lane_bf > be), lane_bf, none), axis=-1, keepdims=True)
    first_e = jnp.min(jnp.where(nonempty, lane_bf, none), axis=-1, keepdims=True)
    last = jnp.where(nxt == none, 1.0, 0.0)
    nxt = jnp.where(nxt == none, first_e, nxt)
    n_runs = rsum(jnp.where(nonempty, 1.0, 0.0))
    cnt_of = rsum(jnp.where(lane_bf == be, cnt.astype(F32)[0:1, :], 0.0))
    vrows = jnp.clip(cnt_of - (blk[:, 0:1] - start_of) * float(tm), 0.0, float(tm))
    tab = jnp.zeros((nbp, LANES), F32)
    for c, v in enumerate((be, n_used, first, run, nxt, last, n_runs, vrows)):
        tab = jnp.where(lane_b == c, v, tab)
    be_ref[...] = tab.astype(I32)
    sub = lax.broadcasted_iota(I32, (SUBLANES, LANES), 0)
    pad_start = start_row.astype(I32) + cnt
    pad_len = jnp.where(lane < N_EXPERTS, nblk * tm - cnt, 0)
    pads_ref[...] = jnp.where(sub == 0, pad_start, jnp.where(sub == 1, pad_len, 0))


def _route_fin(cnt, idx, rank, n_blocks):
    T = idx.shape[0]
    tr = min(512, T)
    nbp = ((n_blocks + SUBLANES - 1) // SUBLANES) * SUBLANES
    tok = pl.BlockSpec((tr, LANES), lambda i: (i, 0))
    return pl.pallas_call(
        functools.partial(_route_fin_kernel, tr=tr, nbp=nbp, tm_shift=int(math.log2(MOE_TM))),
        out_shape=(
            jax.ShapeDtypeStruct((T, LANES), I32),
            jax.ShapeDtypeStruct((nbp, LANES), I32),
            jax.ShapeDtypeStruct((SUBLANES, LANES), I32),
        ),
        grid=(T // tr,),
        in_specs=[pl.BlockSpec((SUBLANES, LANES), lambda i: (0, 0)), tok, tok],
        out_specs=(tok, pl.BlockSpec((nbp, LANES), lambda i: (0, 0)),
                   pl.BlockSpec((SUBLANES, LANES), lambda i: (0, 0))),
        compiler_params=_cparams(("arbitrary",)),
        name="route_fin",
    )(cnt, idx, rank)


def _row_copy(src_ref, src_row, dst_ref, dst_row, sem):
    return pltpu.make_async_copy(src_ref.at[pl.ds(src_row, 1)], dst_ref.at[pl.ds(dst_row, 1)], sem)


def _dispatch_kernel(dest_ref, pstart_ref, plen_ref, nu_ref, h2_ref, xs_out, zeros, sem, *, tm, nb):
    base = pl.program_id(0) * (tm * TOP_K)

    @pl.when(pl.program_id(0) == 0)
    def _():
        zeros[...] = jnp.zeros(zeros.shape, F32)

        def pad_rows(start):
            def per_expert(e, c):
                def one(r, c2):
                    cp = _row_copy(zeros, 0, xs_out, pstart_ref[e] + r, sem)
                    cp.start() if start else cp.wait()
                    return c2
                lax.fori_loop(0, plen_ref[e], one, 0)
                return c
            lax.fori_loop(0, N_EXPERTS, per_expert, 0)

            def tail(t, c):
                row0 = pl.multiple_of((nu_ref[0] + t) * MOE_TM, MOE_TM)
                cp = pltpu.make_async_copy(zeros, xs_out.at[pl.ds(row0, MOE_TM)], sem)
                cp.start() if start else cp.wait()
                return c
            lax.fori_loop(0, nb - nu_ref[0], tail, 0)

        pad_rows(True)
        pad_rows(False)

    def issue(r, c):
        for k in range(TOP_K):
            _row_copy(h2_ref, r, xs_out, dest_ref[base + r * TOP_K + k], sem).start(priority=k % 2)
        return c

    lax.fori_loop(0, tm, issue, 0, unroll=4)

    def drain(r, c):
        for k in range(TOP_K):
            _row_copy(h2_ref, r, xs_out, dest_ref[base + r * TOP_K + k], sem).wait()
        return c

    lax.fori_loop(0, tm, drain, 0, unroll=4)


def _dispatch(dest_flat, pads, nu, h2, n_pad):
    T, W = h2.shape
    tm = min(256, T)
    return pl.pallas_call(
        functools.partial(_dispatch_kernel, tm=tm, nb=n_pad // MOE_TM),
        out_shape=jax.ShapeDtypeStruct((n_pad, W), F32),
        grid_spec=pltpu.PrefetchScalarGridSpec(
            num_scalar_prefetch=4,
            grid=(T // tm,),
            in_specs=[pl.BlockSpec((tm, W), lambda i, *_: (i, 0))],
            out_specs=pl.BlockSpec(memory_space=pl.ANY),
            scratch_shapes=[pltpu.VMEM((MOE_TM, W), F32), pltpu.SemaphoreType.DMA(())],
        ),
        compiler_params=_cparams(("arbitrary",)),
        name="dispatch",
    )(dest_flat, pads[0, :N_EXPERTS], pads[1, :N_EXPERTS], nu, h2)


def _gate_up_kernel(be_ref, nu_ref, first_ref, run_ref, nxt_ref, last_ref, nr_ref, vr_ref,
                    x_ref, bg_ref, bu_ref, w_hbm, o_ref, wbuf, sem, *, tf, nf):
    f = pl.program_id(0)
    i = pl.program_id(1)
    used = i < nu_ref[0]

    def w_copy(e, ff, slot, part):
        col = pl.multiple_of(part * D_FF + ff * tf, tf)
        return pltpu.make_async_copy(w_hbm.at[e, :, pl.ds(col, tf)], wbuf.at[slot, part], sem.at[slot])

    @pl.when(used & (first_ref[i] == 1))
    def _():
        run = f * nr_ref[0] + run_ref[i]
        slot = run & 1
        e = be_ref[i]

        @pl.when(run == 0)
        def _():
            w_copy(e, f, slot, 0).start()
            w_copy(e, f, slot, 1).start()

        w_copy(e, f, slot, 0).wait()
        w_copy(e, f, slot, 1).wait()
        last = last_ref[i]

        @pl.when((last == 0) | (f < nf - 1))
        def _():
            w_copy(nxt_ref[i], f + last, 1 - slot, 0).start()
            w_copy(nxt_ref[i], f + last, 1 - slot, 1).start()

    def swiglu(x, slot):
        gate = jnp.minimum(_dot(x, wbuf[slot, 0]) + bg_ref[...], SWIGLU_LIMIT)
        up = jnp.clip(_dot(x, wbuf[slot, 1]) + bu_ref[...], -SWIGLU_LIMIT, SWIGLU_LIMIT)
        return ((up + 1.0) * (gate * _sigmoid(SWIGLU_ALPHA * gate))).astype(BF16)

    half = x_ref.shape[0] // 2
    full = vr_ref[i] > half

    @pl.when(used & full)
    def _():
        o_ref[...] = swiglu(x_ref[...], (f * nr_ref[0] + run_ref[i]) & 1)

    @pl.when(used & jnp.logical_not(full))
    def _():
        o_ref[0:half, :] = swiglu(x_ref[0:half, :], (f * nr_ref[0] + run_ref[i]) & 1)
        o_ref[half:2 * half, :] = jnp.zeros((half, o_ref.shape[1]), BF16)

    @pl.when(jnp.logical_not(used))
    def _():
        o_ref[...] = jnp.zeros(o_ref.shape, BF16)


def _gate_up(sched, xs, w_gate_up, b_gate_up3):
    n_pad, D = xs.shape
    tm = MOE_TM
    tf = 1024
    nf = D_FF // tf
    nb = n_pad // tm
    eff = lambda i, nu: jnp.minimum(i, nu[0] - 1)
    return pl.pallas_call(
        functools.partial(_gate_up_kernel, tf=tf, nf=nf),
        out_shape=jax.ShapeDtypeStruct((n_pad, D_FF), BF16),
        grid_spec=pltpu.PrefetchScalarGridSpec(
            num_scalar_prefetch=len(sched),
            grid=(nf, nb),
            in_specs=[
                pl.BlockSpec((tm, D), lambda f, i, be, nu, *_: (eff(i, nu), 0)),
                pl.BlockSpec((None, 1, tf), lambda f, i, be, nu, *_: (be[eff(i, nu)], 0, f)),
                pl.BlockSpec((None, 1, tf), lambda f, i, be, nu, *_: (be[eff(i, nu)], 0, nf + f)),
                pl.BlockSpec(memory_space=pl.ANY),
            ],
            out_specs=pl.BlockSpec((tm, tf), lambda f, i, *_: (i, f)),
            scratch_shapes=[
                pltpu.VMEM((2, 2, D, tf), F32),
                pltpu.SemaphoreType.DMA((2,)),
            ],
        ),
        compiler_params=_cparams(("arbitrary", "arbitrary")),
        name="gate_up",
    )(*sched, xs, b_gate_up3, b_gate_up3, w_gate_up)


def _down_kernel(be_ref, nu_ref, first_ref, run_ref, nxt_ref, last_ref, nr_ref, vr_ref,
                 h_ref, b_ref, w_hbm, o_ref, wbuf, sem):
    i = pl.program_id(0)
    used = i < nu_ref[0]

    def w_copy(e, slot):
        return pltpu.make_async_copy(w_hbm.at[e], wbuf.at[slot], sem.at[slot])

    @pl.when(used & (first_ref[i] == 1))
    def _():
        run = run_ref[i]
        slot = run & 1
        e = be_ref[i]

        @pl.when(run == 0)
        def _():
            w_copy(e, slot).start()

        w_copy(e, slot).wait()

        @pl.when(last_ref[i] == 0)
        def _():
            w_copy(nxt_ref[i], 1 - slot).start()

    half = h_ref.shape[0] // 2
    full = vr_ref[i] > half

    @pl.when(used & full)
    def _():
        o_ref[...] = _dot(h_ref[...].astype(F32), wbuf[run_ref[i] & 1]) + b_ref[...]

    @pl.when(used & jnp.logical_not(full))
    def _():
        o_ref[0:half, :] = _dot(h_ref[0:half, :].astype(F32), wbuf[run_ref[i] & 1]) + b_ref[...]
        o_ref[half:2 * half, :] = jnp.zeros((half, o_ref.shape[1]), F32)

    @pl.when(jnp.logical_not(used))
    def _():
        o_ref[...] = jnp.zeros(o_ref.shape, F32)


def _down(sched, h, w_down, b_down3):
    n_pad, F = h.shape
    D = w_down.shape[2]
    tm = MOE_TM
    nb = n_pad // tm
    eff = lambda i, nu: jnp.minimum(i, nu[0] - 1)
    return pl.pallas_call(
        _down_kernel,
        out_shape=jax.ShapeDtypeStruct((n_pad, D), F32),
        grid_spec=pltpu.PrefetchScalarGridSpec(
            num_scalar_prefetch=len(sched),
            grid=(nb,),
            in_specs=[
                pl.BlockSpec((tm, F), lambda i, be, nu, *_: (eff(i, nu), 0)),
                pl.BlockSpec((None, 1, D), lambda i, be, nu, *_: (be[eff(i, nu)], 0, 0)),
                pl.BlockSpec(memory_space=pl.ANY),
            ],
            out_specs=pl.BlockSpec((tm, D), lambda i, *_: (i, 0)),
            scratch_shapes=[
                pltpu.VMEM((2, F, D), F32),
                pltpu.SemaphoreType.DMA((2,)),
            ],
        ),
        compiler_params=_cparams(("arbitrary",)),
        name="down",
    )(*sched, h, b_down3, w_down)


def _combine_kernel(dest_ref, g_ref, x1_ref, fw_ref, y_hbm, o_ref, buf, sem, *, tm):
    base = pl.program_id(0) * (tm * TOP_K)

    def issue(r, c):
        for k in range(TOP_K):
            _row_copy(y_hbm, dest_ref[base + r * TOP_K + k], buf.at[k], r, sem).start(priority=k % 2)
        return c

    lax.fori_loop(0, tm, issue, 0, unroll=4)

    def drain(r, c):
        for k in range(TOP_K):
            _row_copy(y_hbm, dest_ref[base + r * TOP_K + k], buf.at[k], r, sem).wait()
        return c

    lax.fori_loop(0, tm, drain, 0, unroll=4)

    acc = x1_ref[...]
    gts = g_ref[...]
    for k in range(TOP_K):
        acc = acc + gts[:, k:k + 1] * buf[k]
    o_ref[...] = _rms_rows(acc, fw_ref[...])


def _combine(dest_flat, gates, x1, fw, y):
    T, D = x1.shape
    tm = min(256, T)
    return pl.pallas_call(
        functools.partial(_combine_kernel, tm=tm),
        out_shape=jax.ShapeDtypeStruct((T, D), F32),
        grid_spec=pltpu.PrefetchScalarGridSpec(
            num_scalar_prefetch=1,
            grid=(T // tm,),
            in_specs=[
                pl.BlockSpec((tm, LANES), lambda i, d: (i, 0)),
                pl.BlockSpec((tm, D), lambda i, d: (i, 0)),
                pl.BlockSpec((1, D), lambda i, d: (0, 0)),
                pl.BlockSpec(memory_space=pl.ANY),
            ],
            out_specs=pl.BlockSpec((tm, D), lambda i, d: (i, 0)),
            scratch_shapes=[pltpu.VMEM((TOP_K, tm, D), F32), pltpu.SemaphoreType.DMA(())],
        ),
        compiler_params=_cparams(("arbitrary",)),
        name="combine",
    )(dest_flat, gates, x1, fw, y)


def _rope_angles(pos):
    inv = ROPE_THETA ** (-np.arange(0, ROPE_DIM, 2, dtype=np.float64) / ROPE_DIM)
    return pos.astype(np.float64)[:, None] * inv[None, :]


def _rope_tables128(S):
    ang = _rope_angles(np.arange(S))
    c, s = np.cos(ang), np.sin(ang)
    one = np.ones((S, 64 - ROPE_DIM))
    zero = np.zeros((S, 64 - ROPE_DIM))
    z8 = np.zeros((S, 8))
    cos64 = np.concatenate([c, c, one], axis=1)
    sina64 = np.concatenate([z8, s, zero], axis=1)
    sinb64 = np.concatenate([-s, z8, zero], axis=1)
    t2 = lambda t: jnp.asarray(np.concatenate([t, t], axis=1), F32)
    return t2(cos64), t2(sina64), t2(sinb64)


def _rope_tables_cmp(ncp):
    ang = _rope_angles(np.arange(ncp) * CMP_STRIDE + CMP_BLOCK - 1)
    c, s = np.cos(ang), np.sin(ang)
    cos64 = np.concatenate([c, c, np.ones((ncp, 64 - ROPE_DIM))], axis=1)
    sin64 = np.concatenate([s, s, np.zeros((ncp, 64 - ROPE_DIM))], axis=1)
    rot = np.zeros((64, 64), np.float32)
    for d in range(8):
        rot[d + 8, d] = -1.0
        rot[d, d + 8] = 1.0
    return jnp.asarray(cos64, F32), jnp.asarray(sin64, F32), jnp.asarray(rot, BF16)


def _overlap_matrix_t(S):
    ncp = S // CMP_STRIDE
    n_c = (S - CMP_BLOCK) // CMP_STRIDE + 1
    ovl = np.zeros((LANES, ncp), np.float32)
    for c in range(n_c):
        for l in range(CMP_BLOCK):
            ovl[(c * CMP_STRIDE + l) // SEL_BLOCK, c] += 1.0
    return jnp.asarray(ovl, BF16)


def _pack_segments():
    widths = [SSM_D_INNER, SSM_CONV_DIM, SSM_HEADS, NSA_D] + [KV_D] * 6 + [3 * NSA_HEADS]
    cuts = [int(c) for c in np.cumsum([0] + widths)]
    return ((COL_Q, cuts[3], NSA_D), (COL_Z, cuts[0], SSM_D_INNER), (COL_XBC, cuts[1], SSM_CONV_DIM),
            (COL_KV, cuts[4], 6 * KV_D), (COL_DT, cuts[2], SSM_HEADS), (COL_GATE, cuts[10], 3 * NSA_HEADS))


def _pack_w_in_kernel(w_ref, o_ref):
    cols = w_ref.shape[1]
    covered = np.zeros(PROJ_W, bool)
    for dst, src, width in _pack_segments():
        covered[dst:dst + width] = True
        for c in range(0, width, 512):
            n = min(512, width - c)
            o_ref[dst + c:dst + c + n, :] = w_ref[src + c:src + c + n, :].astype(BF16)
    row = 0
    while row < PROJ_W:
        if covered[row]:
            row += 1
            continue
        end = row
        while end < PROJ_W and not covered[end]:
            end += 1
        o_ref[row:end, :] = jnp.zeros((end - row, cols), BF16)
        row = end


def _pack_w_in(w_in):
    wt = jnp.swapaxes(w_in, 1, 2)
    _, W, D = wt.shape
    tc = 512
    return pl.pallas_call(
        _pack_w_in_kernel,
        out_shape=jax.ShapeDtypeStruct((PROJ_W, D), BF16),
        grid=(D // tc,),
        in_specs=[pl.BlockSpec((None, W, tc), lambda i: (0, 0, i))],
        out_specs=pl.BlockSpec((PROJ_W, tc), lambda i: (0, i)),
        compiler_params=_cparams(("parallel",)),
        name="pack_w_in",
    )(wt)


def _sched_columns(tab, n_blocks):
    per_tile = lambda c: tab[:n_blocks, c]
    once = lambda c: tab[0:1, c]
    return (per_tile(0), once(1), per_tile(2), per_tile(3), per_tile(4), per_tile(5), once(6), per_tile(7))


def kernel(x, attn_norm_w, w_in, conv_w, conv_b, dt_bias, a_log, d_skip, ssm_norm_w, cmp_pos_emb, cmp_w1,
           cmp_b1, cmp_w2, cmp_b2, w_out, moe_norm_w, w_router, b_router, w_gate_up, b_gate_up, w_down,
           b_down, final_norm_w):
    B, S, D = x.shape
    T = B * S
    x2d = x.reshape(T, D)
    row = lambda v: v.reshape(1, -1)
    padl = lambda v: jnp.pad(v.reshape(1, -1), ((0, 0), (0, LANES - v.size)))

    proj = _in_proj(x2d, row(attn_norm_w[0]), _pack_w_in(w_in))
    y_ssd = _ssd(proj, B, S, conv_w[0], row(conv_b[0]), padl(dt_bias[0]), padl(a_log[0]),
                 row(jnp.repeat(d_skip[0], SSM_HEAD_DIM)), row(ssm_norm_w[0]))
    cos, sina, sinb = _rope_tables128(S)
    q_rot, kse, vst, kw, vwt, sg = _nsa_prep(proj, B, S, cos, sina, sinb)
    cosc, sinc, rot = _rope_tables_cmp(S // CMP_STRIDE)
    kvc, kvct = _compress(proj, B, S, cmp_pos_emb[0], cmp_w1[0], cmp_b1[0][:, None, :], cmp_w2[0],
                          cmp_b2[0][:, None, :], cosc, sinc, rot)
    y_nsa = _nsa_attn(q_rot, kvc, kvct, kse, vst, kw, vwt, sg, _overlap_matrix_t(S), B, S)

    wr = jnp.pad(w_router[0], ((0, 0), (0, LANES - N_EXPERTS)))
    x1, h2, logits = _out_proj(y_ssd, y_nsa.reshape(T, NSA_D), x2d, w_out[0].astype(BF16),
                                row(moe_norm_w[0]), wr, padl(b_router[0]))

    n_pad = T * TOP_K + N_EXPERTS * MOE_TM
    n_blocks = n_pad // MOE_TM
    idx, rank, gates, cnt = _route(logits)
    dest, be_tab, pads = _route_fin(cnt, idx, rank, n_blocks)
    dest_flat = dest[:, :TOP_K].reshape(-1)
    sched = _sched_columns(be_tab, n_blocks)
    xs = _dispatch(dest_flat, pads, sched[1], h2, n_pad)
    hmid = _gate_up(sched, xs, w_gate_up[0], b_gate_up[0][:, None, :])
    y = _down(sched, hmid, w_down[0], b_down[0][:, None, :])
    out = _combine(dest_flat, gates, x1, row(final_norm_w), y)
    return out.reshape(B, S, D)
```
